```python
import math
import jax
import jax.numpy as jnp
from jax import lax
import numpy as np


D_MODEL = 1024
BATCH = 8
SEQ = 2048
DEPTH = 2

GRID_W = 64
CTX_LEN = 256
D_MIX = D_MODEL
D_CONV = D_MIX // 4
CONV_WIDTH = 3
D_SSM = D_MIX // 4
SSM_GROUP = 16
SSM_GROUPS = D_SSM // SSM_GROUP
SSM_STATE = 64
HEAD_DIM = 64
D_ATTN = D_MIX - D_CONV - D_SSM
N_Q_HEADS = D_ATTN // HEAD_DIM
N_KV_HEADS = 2
Q_PER_KV = N_Q_HEADS // N_KV_HEADS
D_KV = N_KV_HEADS * HEAD_DIM
AXIS_PAIRS = HEAD_DIM // 4
ROPE_BASE = 10000.0
WINDOW = 128
BLOCK = 128
N_EXPERTS = 32
TOP_K = 4
D_EXPERT = 128
D_SHARED = 128
ROUTED_SCALE = 2.5
NORM_EPS = 1e-6
NEG_INF = -1e30
PROJ_SPLITS = (D_CONV, 2 * D_CONV, 3 * D_CONV, 3 * D_CONV + D_SSM,
               3 * D_CONV + D_SSM + D_ATTN, 3 * D_CONV + D_SSM + D_ATTN + D_KV)
D_IN_PROJ = 3 * D_CONV + D_SSM + D_ATTN + 2 * D_KV

kernel_name = 'hybrid_conv_s5_swa_moe_dit_block'


def rms_norm(x, g):
    xf = x.astype(jnp.float32)
    y = xf * lax.rsqrt(jnp.mean(xf * xf, axis=-1, keepdims=True) + NORM_EPS)
    return (y * g.astype(jnp.float32)).astype(x.dtype)


def modulate(x, shift, scale):
    return x * (1.0 + scale) + shift


def axial_rope_angles(n_tokens):
    rows = n_tokens // GRID_W
    row = jnp.repeat(jnp.arange(rows, dtype=jnp.float32), GRID_W)
    col = jnp.tile(jnp.arange(GRID_W, dtype=jnp.float32), rows)
    inv_freq = ROPE_BASE ** (-jnp.arange(AXIS_PAIRS, dtype=jnp.float32) / AXIS_PAIRS)
    return jnp.stack([row[:, None] * inv_freq, col[:, None] * inv_freq], axis=1)


def apply_axial_rope(t, ang):
    b, l, h, _ = t.shape
    t4 = t.reshape(b, l, h, 2, 2, AXIS_PAIRS)
    t1, t2 = t4[..., 0, :], t4[..., 1, :]
    cos = jnp.cos(ang)[None, :, None]
    sin = jnp.sin(ang)[None, :, None]
    out = jnp.stack([t1 * cos - t2 * sin, t1 * sin + t2 * cos], axis=-2)
    return out.reshape(b, l, h, HEAD_DIM)


def head_rms(t, n_heads, g):
    b, l, _ = t.shape
    return rms_norm(t.reshape(b, l, n_heads, HEAD_DIM), g).astype(jnp.float32)


def short_conv_mixer(conv_x, conv_b, conv_c, conv_w):
    v = conv_c * conv_x
    y = lax.conv_general_dilated(v, conv_w.astype(v.dtype)[:, None, :], window_strides=(1,),
                                 padding=((1, 1),), dimension_numbers=('NWC', 'WIO', 'NWC'),
                                 feature_group_count=D_CONV)
    return conv_b * y


def ssm_drive(u, b_bar):
    b, l, _ = u.shape
    ug = u.reshape(b, l, SSM_GROUPS, SSM_GROUP).astype(jnp.complex64)
    return jnp.einsum('blgh,gph->blgp', ug, b_bar)


def ssm_readout(h, c_mat):
    b, l = h.shape[:2]
    return jnp.einsum('blgp,ghp->blgh', h, c_mat).real.reshape(b, l, D_SSM)


def diag_scan(lam_bar, bu, reverse):
    a = jnp.broadcast_to(lam_bar, bu.shape)

    def combine(e1, e2):
        a1, b1 = e1
        a2, b2 = e2
        return a1 * a2, a2 * b1 + b2

    _, h = lax.associative_scan(combine, (a, bu), reverse=reverse, axis=1)
    return h


def ssm_glu(y, w_glu, b_glu):
    g = jax.nn.gelu(y)
    return g * jax.nn.sigmoid(g @ w_glu + b_glu)


def ssm_mixer(u_lat, u_ctx, lam_re, lam_im, log_dt, b_re, b_im, c_re, c_im, d_skip, w_glu, b_glu,
              with_ctx_out):
    f32 = jnp.float32
    d = d_skip.astype(f32)
    ul = u_lat.astype(f32)
    uc = u_ctx.astype(f32)
    y_lat = d * ul
    y_ctx = d * uc if with_ctx_out else None
    for direction, reverse in ((0, False), (1, True)):
        lam = lax.complex(lam_re[direction].astype(f32), lam_im[direction].astype(f32))
        dt = jnp.exp(log_dt[direction].astype(f32))[:, None]
        lam_bar = jnp.exp(lam * dt)
        b_mat = lax.complex(b_re[direction].astype(f32), b_im[direction].astype(f32))
        b_bar = ((lam_bar - 1.0) / lam)[..., None] * b_mat
        c_mat = lax.complex(c_re[direction].astype(f32), c_im[direction].astype(f32))
        h_ctx = diag_scan(lam_bar, ssm_drive(uc, b_bar), reverse)
        h0 = h_ctx[:, 0] if reverse else h_ctx[:, -1]
        bu = ssm_drive(ul, b_bar)
        bu = bu.at[:, -1 if reverse else 0].add(lam_bar * h0)
        y_lat = y_lat + ssm_readout(diag_scan(lam_bar, bu, reverse), c_mat)
        if with_ctx_out:
            y_ctx = y_ctx + ssm_readout(h_ctx, c_mat)
    out_lat = ssm_glu(y_lat, w_glu, b_glu)
    out_ctx = ssm_glu(y_ctx, w_glu, b_glu) if with_ctx_out else None
    return out_lat, out_ctx


def window_attention(q, k, v, k_ctx, v_ctx, sink):
    b, l = q.shape[:2]
    n_blk = l // BLOCK
    n_ctx = k_ctx.shape[1]
    scale = HEAD_DIM ** -0.5
    qb = q.reshape(b, n_blk, BLOCK, N_KV_HEADS, Q_PER_KV, HEAD_DIM)

    def band(t):
        tp = jnp.pad(t, ((0, 0), (BLOCK, BLOCK), (0, 0), (0, 0)))
        tp = tp.reshape(b, n_blk + 2, BLOCK, N_KV_HEADS, HEAD_DIM)
        return jnp.concatenate([tp[:, :-2], tp[:, 1:-1], tp[:, 2:]], axis=2)

    kb, vb = band(k), band(v)
    s_win = jnp.einsum('bnqgrd,bnkgd->bgrnqk', qb, kb) * scale
    blk = jnp.arange(n_blk)[:, None, None]
    qpos = blk * BLOCK + jnp.arange(BLOCK)[None, :, None]
    kpos = (blk - 1) * BLOCK + jnp.arange(3 * BLOCK)[None, None, :]
    valid = (jnp.abs(qpos - kpos) <= WINDOW) & (kpos >= 0) & (kpos < l)
    s_win = jnp.where(valid, s_win, NEG_INF)
    s_ctx = jnp.einsum('bnqgrd,bcgd->bgrnqc', qb, k_ctx) * scale
    s_sink = jnp.broadcast_to(
        sink.astype(jnp.float32).reshape(N_KV_HEADS, Q_PER_KV)[None, :, :, None, None, None],
        s_win.shape[:-1] + (1,))
    p = jax.nn.softmax(jnp.concatenate([s_win, s_ctx, s_sink], axis=-1), axis=-1)
    p_win = p[..., :3 * BLOCK]
    p_ctx = p[..., 3 * BLOCK:3 * BLOCK + n_ctx]
    o = (jnp.einsum('bgrnqk,bnkgd->bnqgrd', p_win, vb)
         + jnp.einsum('bgrnqc,bcgd->bnqgrd', p_ctx, v_ctx))
    return o.reshape(b, l, D_ATTN)


def context_attention(q_ctx, k_ctx, v_ctx, sink):
    b, n_ctx = q_ctx.shape[:2]
    qg = q_ctx.reshape(b, n_ctx, N_KV_HEADS, Q_PER_KV, HEAD_DIM)
    s = jnp.einsum('bqgrd,bkgd->bgrqk', qg, k_ctx) * HEAD_DIM ** -0.5
    s_sink = jnp.broadcast_to(
        sink.astype(jnp.float32).reshape(N_KV_HEADS, Q_PER_KV)[None, :, :, None, None],
        s.shape[:-1] + (1,))
    p = jax.nn.softmax(jnp.concatenate([s, s_sink], axis=-1), axis=-1)
    o = jnp.einsum('bgrqk,bkgd->bqgrd', p[..., :n_ctx], v_ctx)
    return o.reshape(b, n_ctx, D_ATTN)


def moe_ffn(x, router_w, router_bias, w_exp_gate, w_exp_up, w_exp_down, w_sh_gate, w_sh_up, w_sh_down):
    shp = x.shape
    t = x.reshape(-1, D_MODEL)
    scores = jax.nn.sigmoid(t.astype(jnp.float32) @ router_w.astype(jnp.float32))
    _, idx = lax.top_k(scores + router_bias.astype(jnp.float32), TOP_K)
    sel = jnp.take_along_axis(scores, idx, axis=-1)
    wts = sel / jnp.sum(sel, axis=-1, keepdims=True) * ROUTED_SCALE
    gates = jnp.sum(jax.nn.one_hot(idx, N_EXPERTS, dtype=jnp.float32) * wts[..., None], axis=1)
    hg = jnp.einsum('td,def->tef', t, w_exp_gate)
    hu = jnp.einsum('td,def->tef', t, w_exp_up)
    hidden = jax.nn.silu(hg) * hu * gates[..., None]
    y = jnp.einsum('tef,efd->td', hidden, w_exp_down)
    y = y + (jax.nn.silu(t @ w_sh_gate) * (t @ w_sh_up)) @ w_sh_down
    return y.reshape(shp)


def setup_inputs(seed: int = 0) -> dict:
    key = jax.random.key(seed)
    ks = jax.random.split(key, 32)
    f32 = jnp.float32

    def nrm(i, shape, scale):
        return scale * jax.random.normal(ks[i], shape, f32)

    n_idx = jnp.arange(SSM_STATE, dtype=f32)
    ssm_shape = (DEPTH, 2, SSM_GROUPS, SSM_STATE)
    return {
        'x': nrm(0, (BATCH, SEQ, D_MODEL), 1.0),
        'c': nrm(1, (BATCH, D_MODEL), 1.0),
        'ctx': nrm(2, (BATCH, CTX_LEN, D_MODEL), 1.0),
        'c_ctx': nrm(3, (D_MODEL,), 1.0),
        'w_mod': nrm(4, (DEPTH, D_MODEL, 6 * D_MODEL), 0.5 * D_MODEL ** -0.5),
        'b_mod': nrm(5, (DEPTH, 6 * D_MODEL), 0.01),
        'norm1_g': 1.0 + nrm(6, (DEPTH, D_MODEL), 0.02),
        'norm2_g': 1.0 + nrm(7, (DEPTH, D_MODEL), 0.02),
        'w_in': nrm(8, (DEPTH, D_MODEL, D_IN_PROJ), D_MODEL ** -0.5),
        'conv_w': nrm(9, (DEPTH, CONV_WIDTH, D_CONV), CONV_WIDTH ** -0.5),
        'ssm_lam_re': -0.5 + nrm(10, ssm_shape, 0.01),
        'ssm_lam_im': jnp.pi * n_idx + nrm(11, ssm_shape, 0.01),
        'ssm_log_dt': jax.random.uniform(ks[12], (DEPTH, 2, SSM_GROUPS), dtype=f32,
                                         minval=math.log(1e-3), maxval=math.log(1e-1)),
        'ssm_b_re': nrm(13, (DEPTH, 2, SSM_GROUPS, SSM_STATE, SSM_GROUP), (2 * SSM_GROUP) ** -0.5),
        'ssm_b_im': nrm(14, (DEPTH, 2, SSM_GROUPS, SSM_STATE, SSM_GROUP), (2 * SSM_GROUP) ** -0.5),
        'ssm_c_re': nrm(15, (DEPTH, 2, SSM_GROUPS, SSM_GROUP, SSM_STATE), SSM_STATE ** -0.5),
        'ssm_c_im': nrm(16, (DEPTH, 2, SSM_GROUPS, SSM_GROUP, SSM_STATE), SSM_STATE ** -0.5),
        'ssm_d': nrm(17, (DEPTH, D_SSM), 1.0),
        'w_glu': nrm(18, (DEPTH, D_SSM, D_SSM), D_SSM ** -0.5),
        'b_glu': nrm(19, (DEPTH, D_SSM), 0.01),
        'q_norm_g': 1.0 + nrm(20, (DEPTH, HEAD_DIM), 0.02),
        'k_norm_g': 1.0 + nrm(21, (DEPTH, HEAD_DIM), 0.02),
        'attn_sink': nrm(22, (DEPTH, N_Q_HEADS), 1.0),
        'w_out': nrm(23, (DEPTH, D_MIX, D_MODEL), D_MIX ** -0.5),
        'router_w': nrm(24, (DEPTH, D_MODEL, N_EXPERTS), D_MODEL ** -0.5),
        'router_bias': nrm(25, (DEPTH, N_EXPERTS), 0.01),
        'w_exp_gate': nrm(26, (DEPTH, D_MODEL, N_EXPERTS, D_EXPERT), D_MODEL ** -0.5),
        'w_exp_up': nrm(27, (DEPTH, D_MODEL, N_EXPERTS, D_EXPERT), D_MODEL ** -0.5),
        'w_exp_down': nrm(28, (DEPTH, N_EXPERTS, D_EXPERT, D_MODEL), D_EXPERT ** -0.5),
        'w_sh_gate': nrm(29, (DEPTH, D_MODEL, D_SHARED), D_MODEL ** -0.5),
        'w_sh_up': nrm(30, (DEPTH, D_MODEL, D_SHARED), D_MODEL ** -0.5),
        'w_sh_down': nrm(31, (DEPTH, D_SHARED, D_MODEL), D_SHARED ** -0.5),
    }


def reference(x, c, ctx, c_ctx, w_mod, b_mod, norm1_g, norm2_g, w_in, conv_w,
              ssm_lam_re, ssm_lam_im, ssm_log_dt, ssm_b_re, ssm_b_im, ssm_c_re, ssm_c_im,
              ssm_d, w_glu, b_glu, q_norm_g, k_norm_g, attn_sink, w_out,
              router_w, router_bias, w_exp_gate, w_exp_up, w_exp_down,
              w_sh_gate, w_sh_up, w_sh_down):
    n_lat = x.shape[1]
    rope_ang = axial_rope_angles(n_lat)
    silu_c = jax.nn.silu(c)[:, None, :]
    silu_cc = jax.nn.silu(c_ctx)[None, None, :]
    h, hc = x, ctx
    for l in range(DEPTH):
        ctx_out = l < DEPTH - 1
        sh1, sc1, g1, sh2, sc2, g2 = jnp.split(silu_c @ w_mod[l] + b_mod[l], 6, axis=-1)
        csh1, csc1, cg1, csh2, csc2, cg2 = jnp.split(silu_cc @ w_mod[l] + b_mod[l], 6, axis=-1)

        p = modulate(rms_norm(h, norm1_g[l]), sh1, sc1) @ w_in[l]
        pc = modulate(rms_norm(hc, norm1_g[l]), csh1, csc1) @ w_in[l]
        conv_x, conv_b, conv_c, u, q, k, v = jnp.split(p, PROJ_SPLITS, axis=-1)
        conv_xc, conv_bc, conv_cc, uc, qc, kc, vc = jnp.split(pc, PROJ_SPLITS, axis=-1)

        y_conv = short_conv_mixer(conv_x, conv_b, conv_c, conv_w[l])
        y_ssm, y_ssm_c = ssm_mixer(u, uc, ssm_lam_re[l], ssm_lam_im[l], ssm_log_dt[l], ssm_b_re[l],
                                   ssm_b_im[l], ssm_c_re[l], ssm_c_im[l], ssm_d[l], w_glu[l], b_glu[l],
                                   ctx_out)
        q_lat = apply_axial_rope(head_rms(q, N_Q_HEADS, q_norm_g[l]), rope_ang)
        k_lat = apply_axial_rope(head_rms(k, N_KV_HEADS, k_norm_g[l]), rope_ang)
        v_lat = v.reshape(v.shape[0], n_lat, N_KV_HEADS, HEAD_DIM)
        k_ctx = head_rms(kc, N_KV_HEADS, k_norm_g[l])
        v_ctx = vc.reshape(vc.shape[0], vc.shape[1], N_KV_HEADS, HEAD_DIM)
        y_att = window_attention(q_lat, k_lat, v_lat, k_ctx, v_ctx, attn_sink[l])

        mix = jnp.concatenate([y_conv, y_ssm, y_att], axis=-1) @ w_out[l]
        h_mid = h + g1 * mix
        h_new = h_mid + g2 * moe_ffn(modulate(rms_norm(h_mid, norm2_g[l]), sh2, sc2),
                                     router_w[l], router_bias[l], w_exp_gate[l], w_exp_up[l],
                                     w_exp_down[l], w_sh_gate[l], w_sh_up[l], w_sh_down[l])

        if ctx_out:
            y_conv_c = short_conv_mixer(conv_xc, conv_bc, conv_cc, conv_w[l])
            q_ctx = head_rms(qc, N_Q_HEADS, q_norm_g[l])
            y_att_c = context_attention(q_ctx, k_ctx, v_ctx, attn_sink[l])
            mix_c = jnp.concatenate([y_conv_c, y_ssm_c, y_att_c], axis=-1) @ w_out[l]
            hc_mid = hc + cg1 * mix_c
            hc = hc_mid + cg2 * moe_ffn(modulate(rms_norm(hc_mid, norm2_g[l]), csh2, csc2),
                                        router_w[l], router_bias[l], w_exp_gate[l], w_exp_up[l],
                                        w_exp_down[l], w_sh_gate[l], w_sh_up[l], w_sh_down[l])
        h = h_new
    return h
```

```python
import functools
import math

import jax
import jax.numpy as jnp
from jax import lax
from jax.experimental import pallas as pl
from jax.experimental.pallas import tpu as pltpu

F32 = jnp.float32
BF16 = jnp.bfloat16
HIGHEST = lax.Precision.HIGHEST

D_MODEL = 1024
D_CONV = 256
D_SSM = 256
SSM_GROUP = 16
SSM_GROUPS = 16
SSM_STATE = 64
N_STATE = SSM_GROUPS * SSM_STATE
HEAD_DIM = 64
D_ATTN = 512
N_Q_HEADS = 8
N_KV_HEADS = 2
AXIS_PAIRS = 16
ROPE_BASE = 10000.0
GRID_W = 64
BLOCK = 128
N_EXPERTS = 32
TOP_K = 4
D_EXPERT = 128
ROUTED_SCALE = 2.5
NORM_EPS = 1e-6
NEG_INF = -1e30
D_PROJ = 3 * D_CONV + D_SSM + D_ATTN + 4 * N_KV_HEADS * HEAD_DIM
LANES = 128
SUBLANES = 8
VMEM_LIMIT = 56 * 1024 * 1024


def _cparams(n_axes):
    return pltpu.CompilerParams(dimension_semantics=("arbitrary",) * n_axes,
                                vmem_limit_bytes=VMEM_LIMIT)


def _const_spec(shape):
    nd = len(shape)
    return pl.BlockSpec(shape, lambda *_: (0,) * nd, pipeline_mode=pl.Buffered(1))


def _silu(x):
    return x * jax.nn.sigmoid(x)


def _rms_modulate(x, gain, shift, scale):
    ms = jnp.mean(x * x, axis=-1, keepdims=True)
    return (x * lax.rsqrt(ms + NORM_EPS) * gain) * (1.0 + scale) + shift


def _mod_kernel(c_ref, w_ref, b_ref, o_ref):
    cv = c_ref[...]
    o_ref[...] = jnp.dot(_silu(cv), w_ref[...], precision=HIGHEST,
                         preferred_element_type=F32) + b_ref[...]


def _modulation(cvec, w_mod, b_mod):
    depth, d, n = w_mod.shape
    rows = cvec.shape[0]
    tn = 1536
    return pl.pallas_call(
        _mod_kernel,
        out_shape=jax.ShapeDtypeStruct((depth, rows, n), F32),
        grid=(depth, n // tn),
        in_specs=[pl.BlockSpec((rows, d), lambda l, j: (0, 0)),
                  pl.BlockSpec((None, d, tn), lambda l, j: (l, 0, j)),
                  pl.BlockSpec((None, 1, tn), lambda l, j: (l, 0, j))],
        out_specs=pl.BlockSpec((None, rows, tn), lambda l, j: (l, 0, j)),
        compiler_params=_cparams(2),
        name="modulation",
    )(cvec, w_mod, b_mod.reshape(depth, 1, n))


def _head_norm(xj, gain):
    lane = lax.broadcasted_iota(jnp.int32, (1, LANES), 1)
    lo = lane < HEAD_DIM
    sq = xj * xj
    s_lo = jnp.sum(jnp.where(lo, sq, 0.0), axis=-1, keepdims=True)
    s_hi = jnp.sum(jnp.where(lo, 0.0, sq), axis=-1, keepdims=True)
    r = jnp.where(lo, lax.rsqrt(s_lo * (1.0 / HEAD_DIM) + NORM_EPS),
                  lax.rsqrt(s_hi * (1.0 / HEAD_DIM) + NORM_EPS))
    return xj * r * gain


def _rope(xj, cos, sin_signed):
    lane = lax.broadcasted_iota(jnp.int32, (1, LANES), 1)
    upper = (lane & AXIS_PAIRS) != 0
    partner = jnp.where(upper, pltpu.roll(xj, AXIS_PAIRS, 1), pltpu.roll(xj, LANES - AXIS_PAIRS, 1))
    return xj * cos + partner * sin_signed


def _inproj_kernel(*refs, rope):
    if rope:
        (x_ref, mod_ref, g_ref, w_ref, qg_ref, kg_ref, cos_ref, sin_ref,
         cv_ref, cb_ref, u_ref, q_ref, k_ref, v_ref) = refs
    else:
        (x_ref, mod_ref, g_ref, w_ref, qg_ref, kg_ref,
         cv_ref, cb_ref, u_ref, q_ref, k_ref, v_ref) = refs
    mod = mod_ref[...]
    m = _rms_modulate(x_ref[...], g_ref[...], mod[:, 0:D_MODEL], mod[:, D_MODEL:2 * D_MODEL])
    p = jnp.dot(m.astype(BF16), w_ref[...], preferred_element_type=F32)
    cv_ref[...] = p[:, 2 * D_CONV:3 * D_CONV] * p[:, 0:D_CONV]
    cb_ref[...] = p[:, D_CONV:2 * D_CONV]
    u_ref[...] = p[:, 3 * D_CONV:3 * D_CONV + D_SSM]
    q0 = 3 * D_CONV + D_SSM
    k0 = q0 + D_ATTN
    v0 = k0 + 2 * LANES
    for j in range(D_ATTN // LANES):
        xj = _head_norm(p[:, q0 + j * LANES:q0 + (j + 1) * LANES], qg_ref[...])
        if rope:
            xj = _rope(xj, cos_ref[...], sin_ref[...])
        q_ref[:, j * LANES:(j + 1) * LANES] = (xj * HEAD_DIM ** -0.5).astype(BF16)
    for j in range(N_KV_HEADS):
        xj = _head_norm(p[:, k0 + j * LANES:k0 + (j + 1) * LANES], kg_ref[...])
        if rope:
            xj = _rope(xj, cos_ref[...], sin_ref[...])
        k_ref[:, j * LANES:(j + 1) * LANES] = xj.astype(BF16)
    v_ref[...] = p[:, v0:v0 + 2 * LANES].astype(BF16)


def _in_projection(h, mod_l, mod_row, norm_g, w_in2, qg, kg, rope_tabs, tm):
    bsz, t, d = h.shape
    rope = rope_tabs is not None
    row_map = lambda b, i: (b, i, 0)
    in_specs = [pl.BlockSpec((None, tm, d), row_map),
                pl.BlockSpec((None, 1, 6 * d), lambda b, i: (mod_row(b), 0, 0)),
                _const_spec((1, d)),
                _const_spec((d, D_PROJ)),
                _const_spec((1, LANES)),
                _const_spec((1, LANES))]
    args = [h, mod_l, norm_g, w_in2, qg, kg]
    if rope:
        in_specs += [pl.BlockSpec((tm, LANES), lambda b, i: (i, 0))] * 2
        args += list(rope_tabs)
    out_shape = (jax.ShapeDtypeStruct((bsz, t, D_CONV), F32),
                 jax.ShapeDtypeStruct((bsz, t, D_CONV), F32),
                 jax.ShapeDtypeStruct((t, bsz * D_SSM), F32),
                 jax.ShapeDtypeStruct((bsz, t, D_ATTN), BF16),
                 jax.ShapeDtypeStruct((bsz, t, 2 * LANES), BF16),
                 jax.ShapeDtypeStruct((bsz, t, 2 * LANES), BF16))
    out_specs = (pl.BlockSpec((None, tm, D_CONV), row_map),
                 pl.BlockSpec((None, tm, D_CONV), row_map),
                 pl.BlockSpec((tm, D_SSM), lambda b, i: (i, b)),
                 pl.BlockSpec((None, tm, D_ATTN), row_map),
                 pl.BlockSpec((None, tm, 2 * LANES), row_map),
                 pl.BlockSpec((None, tm, 2 * LANES), row_map))
    return pl.pallas_call(
        functools.partial(_inproj_kernel, rope=rope),
        out_shape=out_shape, grid=(bsz, t // tm), in_specs=in_specs, out_specs=out_specs,
        compiler_params=_cparams(2),
        name="in_projection_lat" if rope else "in_projection_ctx",
    )(*args)


def _scan_kernel(uf_ref, ur_ref, bmat_ref, lam_ref, cmat_ref, yf_ref, yr_ref,
                 sf_ref, sr_ref, hf_ref, hr_ref, *, steps):
    @pl.when(pl.program_id(0) == 0)
    def _():
        hf_ref[...] = jnp.zeros_like(hf_ref)
        hr_ref[...] = jnp.zeros_like(hr_ref)

    sf_ref[...] = jnp.dot(uf_ref[...].astype(BF16), bmat_ref[0], preferred_element_type=F32)
    sr_ref[...] = jnp.dot(ur_ref[...].astype(BF16), bmat_ref[1], preferred_element_type=F32)

    def lam_rows(r):
        return jnp.broadcast_to(lam_ref[r:r + 1, :], (SUBLANES, N_STATE))

    lf_re, lf_im, lr_re, lr_im = lam_rows(0), lam_rows(1), lam_rows(2), lam_rows(3)

    def step(s_ref, row, l_re, l_im, h_re, h_im):
        row = pl.multiple_of(row, SUBLANES)
        n_re = l_re * h_re - l_im * h_im + s_ref[pl.ds(row, SUBLANES), 0:N_STATE]
        n_im = l_re * h_im + l_im * h_re + s_ref[pl.ds(row, SUBLANES), N_STATE:2 * N_STATE]
        s_ref[pl.ds(row, SUBLANES), 0:N_STATE] = n_re
        s_ref[pl.ds(row, SUBLANES), N_STATE:2 * N_STATE] = n_im
        return n_re, n_im

    def body(t, carry):
        f_re, f_im, r_re, r_im = carry
        f_re, f_im = step(sf_ref, t * SUBLANES, lf_re, lf_im, f_re, f_im)
        r_re, r_im = step(sr_ref, (steps - 1 - t) * SUBLANES, lr_re, lr_im, r_re, r_im)
        return f_re, f_im, r_re, r_im

    init = (hf_ref[:, 0:N_STATE], hf_ref[:, N_STATE:2 * N_STATE],
            hr_ref[:, 0:N_STATE], hr_ref[:, N_STATE:2 * N_STATE])
    f_re, f_im, r_re, r_im = lax.fori_loop(0, steps, body, init)
    hf_ref[:, 0:N_STATE] = f_re
    hf_ref[:, N_STATE:2 * N_STATE] = f_im
    hr_ref[:, 0:N_STATE] = r_re
    hr_ref[:, N_STATE:2 * N_STATE] = r_im

    yf_ref[...] = jnp.dot(sf_ref[...].astype(BF16), cmat_ref[0], preferred_element_type=F32)
    yr_ref[...] = jnp.dot(sr_ref[...].astype(BF16), cmat_ref[1], preferred_element_type=F32)


def _ssm_scan(u_all, bmat, lamv, cmat, n_lat, n_ctx, bsz, steps):
    rows = steps * bsz
    nl, nc = n_lat // steps, n_ctx // steps
    fwd = lambda s: (jnp.where(s < nc, nl + s, s - nc), 0)
    rev = lambda s: (nl + nc - 1 - s, 0)
    total = (n_lat + n_ctx) * bsz
    return pl.pallas_call(
        functools.partial(_scan_kernel, steps=steps),
        out_shape=(jax.ShapeDtypeStruct((total, D_SSM), F32),) * 2,
        grid=(nl + nc,),
        in_specs=[pl.BlockSpec((rows, D_SSM), fwd),
                  pl.BlockSpec((rows, D_SSM), rev),
                  _const_spec((2, D_SSM, 2 * N_STATE)),
                  _const_spec((4, N_STATE)),
                  _const_spec((2, 2 * N_STATE, D_SSM))],
        out_specs=(pl.BlockSpec((rows, D_SSM), fwd), pl.BlockSpec((rows, D_SSM), rev)),
        scratch_shapes=[pltpu.VMEM((rows, 2 * N_STATE), F32), pltpu.VMEM((rows, 2 * N_STATE), F32),
                        pltpu.VMEM((SUBLANES, 2 * N_STATE), F32), pltpu.VMEM((SUBLANES, 2 * N_STATE), F32)],
        compiler_params=_cparams(1),
        name="s5_scan",
    )(u_all, u_all, bmat, lamv, cmat)


def _attn_kernel(sink_ref, q_ref, *refs, window):
    if window:
        kp_ref, ko_ref, kn_ref, vp_ref, vo_ref, vn_ref, kc_ref, vc_ref, o_ref = refs
    else:
        kc_ref, vc_ref, o_ref = refs
    n = pl.program_id(1)
    last = pl.num_programs(1) - 1
    rows = (N_Q_HEADS // N_KV_HEADS) * BLOCK
    lane = lax.broadcasted_iota(jnp.int32, (1, LANES), 1)
    lo = lane < HEAD_DIM
    qi = lax.broadcasted_iota(jnp.int32, (rows, BLOCK), 0) & (BLOCK - 1)
    kj = lax.broadcasted_iota(jnp.int32, (rows, BLOCK), 1)
    head_of_row = lax.broadcasted_iota(jnp.int32, (rows, 1), 0) // BLOCK
    zero = jnp.zeros((), BF16)
    nt = (((1,), (1,)), ((), ()))
    for g in range(N_KV_HEADS):
        cols = slice(g * LANES, (g + 1) * LANES)
        stacked = []
        for j in (2 * g, 2 * g + 1):
            qp = q_ref[:, j * LANES:(j + 1) * LANES]
            stacked += [jnp.where(lo, qp, zero), jnp.where(lo, zero, qp)]
        qs = jnp.concatenate(stacked, axis=0)
        scores, values = [], []
        if window:
            s = lax.dot_general(qs, kp_ref[:, cols], nt, preferred_element_type=F32)
            scores.append(jnp.where((kj >= qi) & (n > 0), s, NEG_INF))
            scores.append(lax.dot_general(qs, ko_ref[:, cols], nt, preferred_element_type=F32))
            s = lax.dot_general(qs, kn_ref[:, cols], nt, preferred_element_type=F32)
            scores.append(jnp.where((kj <= qi) & (n < last), s, NEG_INF))
            values += [vp_ref[:, cols], vo_ref[:, cols], vn_ref[:, cols]]
        scores.append(lax.dot_general(qs, kc_ref[:, cols], nt, preferred_element_type=F32))
        values.append(vc_ref[:, cols])
        sink = jnp.zeros((rows, 1), F32)
        for r in range(N_Q_HEADS // N_KV_HEADS):
            sink = jnp.where(head_of_row == r, sink_ref[g * (N_Q_HEADS // N_KV_HEADS) + r], sink)
        m = sink
        for s in scores:
            m = jnp.maximum(m, jnp.max(s, axis=-1, keepdims=True))
        denom = jnp.exp(sink - m)
        acc = jnp.zeros((rows, LANES), F32)
        for s, v in zip(scores, values):
            e = jnp.exp(s - m)
            denom = denom + jnp.sum(e, axis=-1, keepdims=True)
            acc = acc + jnp.dot(e.astype(BF16), v, preferred_element_type=F32)
        out = acc / denom
        for jj in range(2):
            blk = jnp.where(lo, out[(2 * jj) * BLOCK:(2 * jj + 1) * BLOCK],
                            out[(2 * jj + 1) * BLOCK:(2 * jj + 2) * BLOCK])
            o_ref[:, (2 * g + jj) * LANES:(2 * g + jj + 1) * LANES] = blk.astype(BF16)


def _attention(sink, q, k2, v2, kc2, vc2, window):
    bsz, t, _ = q.shape
    n_ctx = kc2.shape[1]
    nb = t // BLOCK
    kw = 2 * LANES
    blk = lambda f: pl.BlockSpec((None, BLOCK, kw), f)
    in_specs = [pl.BlockSpec(memory_space=pltpu.SMEM),
                pl.BlockSpec((None, BLOCK, D_ATTN), lambda b, n: (b, n, 0))]
    args = [sink, q]
    if window:
        prev = lambda b, n: (b, jnp.maximum(n - 1, 0), 0)
        own = lambda b, n: (b, n, 0)
        nxt = lambda b, n: (b, jnp.minimum(n + 1, nb - 1), 0)
        in_specs += [blk(prev), blk(own), blk(nxt)] * 2
        args += [k2, k2, k2, v2, v2, v2]
    ctx_spec = pl.BlockSpec((None, n_ctx, kw), lambda b, n: (b, 0, 0))
    in_specs += [ctx_spec, ctx_spec]
    args += [kc2, vc2]
    return pl.pallas_call(
        functools.partial(_attn_kernel, window=window),
        out_shape=jax.ShapeDtypeStruct((bsz, t, D_ATTN), BF16),
        grid=(bsz, nb), in_specs=in_specs,
        out_specs=pl.BlockSpec((None, BLOCK, D_ATTN), lambda b, n: (b, n, 0)),
        compiler_params=_cparams(2),
        name="window_attention" if window else "context_attention",
    )(*args)


def _mixout_kernel(h_ref, mod_ref, cv_ref, cvp_ref, cvn_ref, cb_ref, u_ref, yf_ref, yr_ref, att_ref,
                   convw_ref, d_ref, wglu_ref, bglu_ref, wout_ref, g2_ref, rw_ref, rb_ref,
                   hmid_ref, t_ref, gates_ref):
    i = pl.program_id(1)
    last = pl.num_programs(1) - 1
    tm = cv_ref.shape[0]
    cv = cv_ref[...]
    row = lax.broadcasted_iota(jnp.int32, (tm, 1), 0)
    before = jnp.where(i > 0, cvp_ref[SUBLANES - 1:SUBLANES, :], 0.0)
    after = jnp.where(i < last, cvn_ref[0:1, :], 0.0)
    down = jnp.where(row == 0, before, pltpu.roll(cv, 1, 0))
    up = jnp.where(row == tm - 1, after, pltpu.roll(cv, tm - 1, 0))
    w = convw_ref[...]
    y_conv = cb_ref[...] * (w[0:1, :] * down + w[1:2, :] * cv + w[2:3, :] * up)
    y = d_ref[...] * u_ref[...] + yf_ref[...] + yr_ref[...]
    gl = jax.nn.gelu(y)
    z = jnp.dot(gl.astype(BF16), wglu_ref[...], preferred_element_type=F32) + bglu_ref[...]
    y_ssm = gl * jax.nn.sigmoid(z)
    mix = (jnp.dot(y_conv.astype(BF16), wout_ref[0:D_CONV, :], preferred_element_type=F32)
           + jnp.dot(y_ssm.astype(BF16), wout_ref[D_CONV:D_CONV + D_SSM, :], preferred_element_type=F32)
           + jnp.dot(att_ref[...], wout_ref[D_CONV + D_SSM:, :], preferred_element_type=F32))
    mod = mod_ref[...]
    h_mid = h_ref[...] + mod[:, 2 * D_MODEL:3 * D_MODEL] * mix
    hmid_ref[...] = h_mid
    t = _rms_modulate(h_mid, g2_ref[...], mod[:, 3 * D_MODEL:4 * D_MODEL], mod[:, 4 * D_MODEL:5 * D_MODEL])
    t_ref[...] = t.astype(BF16)
    scores = jax.nn.sigmoid(jnp.dot(t, rw_ref[...], precision=HIGHEST, preferred_element_type=F32))
    biased = scores + rb_ref[...]
    eidx = lax.broadcasted_iota(jnp.int32, (1, N_EXPERTS), 1).astype(F32)
    sel = jnp.zeros(scores.shape, F32)
    for _ in range(TOP_K):
        best = jnp.max(biased, axis=-1, keepdims=True)
        first = jnp.min(jnp.where(biased == best, eidx, float(N_EXPERTS)), axis=-1, keepdims=True)
        pick = eidx == first
        sel = jnp.where(pick, scores, sel)
        biased = jnp.where(pick, -jnp.inf, biased)
    gates_ref[...] = sel / jnp.sum(sel, axis=-1, keepdims=True) * ROUTED_SCALE


def _mixer_output(h, mod_l, mod_row, cv, cb, tm_arr, tm_off, yf, yr, att,
                  conv_w, ssm_d, w_glu, b_glu, w_out, norm_g, router_w, router_b, tm):
    bsz, t, d = h.shape
    halo = tm // SUBLANES
    n_halo = t // SUBLANES
    row_map = lambda b, i: (b, i, 0)
    tmaj = pl.BlockSpec((tm, D_SSM), lambda b, i: (tm_off + i, b))
    in_specs = [pl.BlockSpec((None, tm, d), row_map),
                pl.BlockSpec((None, 1, 6 * d), lambda b, i: (mod_row(b), 0, 0)),
                pl.BlockSpec((None, tm, D_CONV), row_map),
                pl.BlockSpec((None, SUBLANES, D_CONV), lambda b, i: (b, jnp.maximum(i * halo - 1, 0), 0)),
                pl.BlockSpec((None, SUBLANES, D_CONV), lambda b, i: (b, jnp.minimum((i + 1) * halo, n_halo - 1), 0)),
                pl.BlockSpec((None, tm, D_CONV), row_map),
                tmaj, tmaj, tmaj,
                pl.BlockSpec((None, tm, D_ATTN), row_map),
                _const_spec((3, D_CONV)),
                _const_spec((1, D_SSM)),
                _const_spec((D_SSM, D_SSM)),
                _const_spec((1, D_SSM)),
                _const_spec((d, d)),
                _const_spec((1, d)),
                _const_spec((d, N_EXPERTS)),
                _const_spec((1, N_EXPERTS))]
    out_shape = (jax.ShapeDtypeStruct((bsz, t, d), F32),
                 jax.ShapeDtypeStruct((bsz, t, d), BF16),
                 jax.ShapeDtypeStruct((bsz, t, N_EXPERTS), F32))
    out_specs = (pl.BlockSpec((None, tm, d), row_map),
                 pl.BlockSpec((None, tm, d), row_map),
                 pl.BlockSpec((None, tm, N_EXPERTS), row_map))
    return pl.pallas_call(
        _mixout_kernel, out_shape=out_shape, grid=(bsz, t // tm),
        in_specs=in_specs, out_specs=out_specs,
        compiler_params=_cparams(2),
        name="mixer_output",
    )(h, mod_l, cv, cv, cv, cb, tm_arr, yf, yr, att,
      conv_w, ssm_d, w_glu, b_glu, w_out, norm_g, router_w, router_b)


EXPERTS_PER_CHUNK = 4


def _moe_kernel(t_ref, gates_ref, hmid_ref, mod_ref, wg_ref, wu_ref, wd_ref,
                wsg_ref, wsu_ref, wsd_ref, o_ref, acc_ref):
    t = t_ref[...]
    gates = gates_ref[...]
    hs = _silu(jnp.dot(t, wsg_ref[...], preferred_element_type=F32)) * jnp.dot(
        t, wsu_ref[...], preferred_element_type=F32)
    acc_ref[...] = jnp.dot(hs.astype(BF16), wsd_ref[...], preferred_element_type=F32)
    width = EXPERTS_PER_CHUNK * D_EXPERT
    for c in range(N_EXPERTS // EXPERTS_PER_CHUNK):
        cols = slice(c * width, (c + 1) * width)
        hid = _silu(jnp.dot(t, wg_ref[:, cols], preferred_element_type=F32)) * jnp.dot(
            t, wu_ref[:, cols], preferred_element_type=F32)
        gated = []
        for e in range(EXPERTS_PER_CHUNK):
            k = c * EXPERTS_PER_CHUNK + e
            gated.append((hid[:, e * D_EXPERT:(e + 1) * D_EXPERT] * gates[:, k:k + 1]).astype(BF16))
        acc_ref[...] += jnp.dot(jnp.concatenate(gated, axis=1), wd_ref[cols, :],
                                preferred_element_type=F32)
    mod = mod_ref[...]
    o_ref[...] = hmid_ref[...] + mod[:, 5 * D_MODEL:6 * D_MODEL] * acc_ref[...]


def _moe(t, gates, hmid, mod_l, mod_row, wg, wu, wd, wsg, wsu, wsd, tm):
    bsz, n, d = hmid.shape
    row_map = lambda b, i: (b, i, 0)
    return pl.pallas_call(
        _moe_kernel,
        out_shape=jax.ShapeDtypeStruct((bsz, n, d), F32),
        grid=(bsz, n // tm),
        in_specs=[pl.BlockSpec((None, tm, d), row_map),
                  pl.BlockSpec((None, tm, N_EXPERTS), row_map),
                  pl.BlockSpec((None, tm, d), row_map),
                  pl.BlockSpec((None, 1, 6 * d), lambda b, i: (mod_row(b), 0, 0)),
                  _const_spec(wg.shape), _const_spec(wu.shape), _const_spec(wd.shape),
                  _const_spec(wsg.shape), _const_spec(wsu.shape), _const_spec(wsd.shape)],
        out_specs=pl.BlockSpec((None, tm, d), row_map),
        scratch_shapes=[pltpu.VMEM((tm, d), F32)],
        compiler_params=_cparams(2),
        name="moe",
    )(t, gates, hmid, mod_l, wg, wu, wd, wsg, wsu, wsd)


def _rope_tables(n_tokens):
    rows = n_tokens // GRID_W
    row = jnp.repeat(jnp.arange(rows, dtype=F32), GRID_W)
    col = jnp.tile(jnp.arange(GRID_W, dtype=F32), rows)
    inv_freq = ROPE_BASE ** (-jnp.arange(AXIS_PAIRS, dtype=F32) / AXIS_PAIRS)
    ar, ac = row[:, None] * inv_freq, col[:, None] * inv_freq
    cos = jnp.concatenate([jnp.cos(ar), jnp.cos(ar), jnp.cos(ac), jnp.cos(ac)], axis=1)
    sin = jnp.concatenate([-jnp.sin(ar), jnp.sin(ar), -jnp.sin(ac), jnp.sin(ac)], axis=1)
    return jnp.tile(cos, (1, LANES // HEAD_DIM)), jnp.tile(sin, (1, LANES // HEAD_DIM))


def _ssm_params(lam_re, lam_im, log_dt, b_re, b_im, c_re, c_im):
    lam = lax.complex(lam_re.astype(F32), lam_im.astype(F32))
    dt = jnp.exp(log_dt.astype(F32))[..., None]
    lam_bar = jnp.exp(lam * dt)
    b_bar = ((lam_bar - 1.0) / lam)[..., None] * lax.complex(b_re.astype(F32), b_im.astype(F32))
    eye = jnp.eye(SSM_GROUPS, dtype=F32)
    drive = lambda m: jnp.einsum('gk,dgph->dghkp', eye, m).reshape(2, D_SSM, N_STATE)
    read = lambda m: jnp.einsum('gk,dghp->dgpkh', eye, m.astype(F32)).reshape(2, N_STATE, D_SSM)
    bmat = jnp.concatenate([drive(jnp.real(b_bar)), drive(jnp.imag(b_bar))], axis=2)
    cmat = jnp.concatenate([read(c_re), -read(c_im)], axis=1)
    lam_flat = lambda m: m.reshape(2, N_STATE)
    lr, li = lam_flat(jnp.real(lam_bar)), lam_flat(jnp.imag(lam_bar))
    lamv = jnp.stack([lr[0], li[0], lr[1], li[1]])
    return bmat.astype(BF16), lamv, cmat.astype(BF16)


def _dup_heads(w):
    d = w.shape[0]
    w = w.reshape(d, N_KV_HEADS, 1, HEAD_DIM)
    return jnp.broadcast_to(w, (d, N_KV_HEADS, 2, HEAD_DIM)).reshape(d, 2 * N_KV_HEADS * HEAD_DIM)


def kernel(x, c, ctx, c_ctx, w_mod, b_mod, norm1_g, norm2_g, w_in, conv_w, ssm_lam_re, ssm_lam_im, ssm_log_dt, ssm_b_re, ssm_b_im, ssm_c_re, ssm_c_im, ssm_d, w_glu, b_glu, q_norm_g, k_norm_g, attn_sink, w_out, router_w, router_bias, w_exp_gate, w_exp_up, w_exp_down, w_sh_gate, w_sh_up, w_sh_down):
    bsz, n_lat, d = x.shape
    n_ctx = ctx.shape[1]
    depth = w_mod.shape[0]
    assert bsz == SUBLANES and d == D_MODEL

    mod_rows = 2 * SUBLANES
    cvec = jnp.zeros((mod_rows, d), F32).at[:bsz].set(c).at[bsz].set(c_ctx)
    mod = _modulation(cvec, w_mod, b_mod).reshape(depth, mod_rows, 1, 6 * d)
    lat_row = lambda b: b
    ctx_row = lambda b: bsz
    rope_tabs = _rope_tables(n_lat)

    tm_lat, tm_ctx, scan_steps = 512, 256, 64
    kv0 = 3 * D_CONV + D_SSM + D_ATTN
    h, hc = x, ctx
    for l in range(depth):
        ctx_out = l < depth - 1
        mod_l = mod[l]
        w_in2 = jnp.concatenate([w_in[l][:, :kv0], _dup_heads(w_in[l][:, kv0:kv0 + 2 * HEAD_DIM]),
                                 _dup_heads(w_in[l][:, kv0 + 2 * HEAD_DIM:])], axis=1).astype(BF16)
        qg = jnp.tile(q_norm_g[l].astype(F32), LANES // HEAD_DIM)[None, :]
        kg = jnp.tile(k_norm_g[l].astype(F32), LANES // HEAD_DIM)[None, :]
        n1 = norm1_g[l][None, :]
        cv, cb, u_lat, q, k2, v2 = _in_projection(h, mod_l, lat_row, n1, w_in2, qg, kg, rope_tabs, tm_lat)
        cvc, cbc, u_ctx, qc, kc2, vc2 = _in_projection(hc, mod_l, ctx_row, n1, w_in2, qg, kg, None, tm_ctx)

        bmat, lamv, cmat = _ssm_params(ssm_lam_re[l], ssm_lam_im[l], ssm_log_dt[l], ssm_b_re[l],
                                       ssm_b_im[l], ssm_c_re[l], ssm_c_im[l])
        u_all = jnp.concatenate([u_lat, u_ctx], axis=0)
        yf, yr = _ssm_scan(u_all.reshape(-1, D_SSM), bmat, lamv, cmat, n_lat, n_ctx, bsz, scan_steps)
        yf, yr = yf.reshape(u_all.shape), yr.reshape(u_all.shape)

        sink = attn_sink[l].astype(F32)
        att = _attention(sink, q, k2, v2, kc2, vc2, window=True)

        post = dict(conv_w=conv_w[l], ssm_d=ssm_d[l][None, :], w_glu=w_glu[l].astype(BF16),
                    b_glu=b_glu[l][None, :], w_out=w_out[l].astype(BF16), norm_g=norm2_g[l][None, :],
                    router_w=router_w[l], router_b=router_bias[l][None, :])
        experts = (w_exp_gate[l].reshape(d, -1).astype(BF16), w_exp_up[l].reshape(d, -1).astype(BF16),
                   w_exp_down[l].reshape(-1, d).astype(BF16), w_sh_gate[l].astype(BF16),
                   w_sh_up[l].astype(BF16), w_sh_down[l].astype(BF16))
        hmid, t, gates = _mixer_output(h, mod_l, lat_row, cv, cb, u_all, 0, yf, yr, att, tm=tm_lat, **post)
        h_new = _moe(t, gates, hmid, mod_l, lat_row, *experts, tm=tm_lat)
        if ctx_out:
            attc = _attention(sink, qc, None, None, kc2, vc2, window=False)
            hmid_c, tc, gates_c = _mixer_output(hc, mod_l, ctx_row, cvc, cbc, u_all, n_lat // tm_ctx,
                                                yf, yr, attc, tm=tm_ctx, **post)
            hc = _moe(tc, gates_c, hmid_c, mod_l, ctx_row, *experts, tm=tm_ctx)
        h = h_new
    return h
```

```python
import functools
import math

import jax
import jax.numpy as jnp
from jax import lax
from jax.experimental import pallas as pl
from jax.experimental.pallas import tpu as pltpu

F32 = jnp.float32
BF16 = jnp.bfloat16
HIGHEST = lax.Precision.HIGHEST

D_MODEL = 1024
D_CONV = 256
D_SSM = 256
SSM_GROUP = 16
SSM_GROUPS = 16
SSM_STATE = 64
N_STATE = SSM_GROUPS * SSM_STATE
HEAD_DIM = 64
D_ATTN = 512
N_Q_HEADS = 8
N_KV_HEADS = 2
AXIS_PAIRS = 16
ROPE_BASE = 10000.0
GRID_W = 64
BLOCK = 128
N_EXPERTS = 32
TOP_K = 4
D_EXPERT = 128
ROUTED_SCALE = 2.5
NORM_EPS = 1e-6
NEG_INF = -1e30
D_PROJ = 3 * D_CONV + D_SSM + D_ATTN + 4 * N_KV_HEADS * HEAD_DIM
LANES = 128
SUBLANES = 8
VMEM_LIMIT = 56 * 1024 * 1024


def _cparams(n_axes):
    return pltpu.CompilerParams(dimension_semantics=("arbitrary",) * n_axes,
                                vmem_limit_bytes=VMEM_LIMIT)


def _const_spec(shape):
    nd = len(shape)
    return pl.BlockSpec(shape, lambda *_: (0,) * nd, pipeline_mode=pl.Buffered(1))


def _silu(x):
    return x * jax.nn.sigmoid(x)


def _rms_modulate(x, gain, shift, scale):
    ms = jnp.mean(x * x, axis=-1, keepdims=True)
    return (x * lax.rsqrt(ms + NORM_EPS) * gain) * (1.0 + scale) + shift


def _mod_kernel(c_ref, w_ref, b_ref, o_ref):
    cv = c_ref[...]
    o_ref[...] = jnp.dot(_silu(cv), w_ref[...], precision=HIGHEST,
                         preferred_element_type=F32) + b_ref[...]


def _modulation(cvec, w_mod, b_mod):
    depth, d, n = w_mod.shape
    rows = cvec.shape[0]
    tn = 1536
    return pl.pallas_call(
        _mod_kernel,
        out_shape=jax.ShapeDtypeStruct((depth, rows, n), F32),
        grid=(depth, n // tn),
        in_specs=[pl.BlockSpec((rows, d), lambda l, j: (0, 0)),
                  pl.BlockSpec((None, d, tn), lambda l, j: (l, 0, j)),
                  pl.BlockSpec((None, 1, tn), lambda l, j: (l, 0, j))],
        out_specs=pl.BlockSpec((None, rows, tn), lambda l, j: (l, 0, j)),
        compiler_params=_cparams(2),
        name="modulation",
    )(cvec, w_mod, b_mod.reshape(depth, 1, n))


def _head_norm(xj, gain):
    lane = lax.broadcasted_iota(jnp.int32, (1, LANES), 1)
    lo = lane < HEAD_DIM
    sq = xj * xj
    s_lo = jnp.sum(jnp.where(lo, sq, 0.0), axis=-1, keepdims=True)
    s_hi = jnp.sum(jnp.where(lo, 0.0, sq), axis=-1, keepdims=True)
    r = jnp.where(lo, lax.rsqrt(s_lo * (1.0 / HEAD_DIM) + NORM_EPS),
                  lax.rsqrt(s_hi * (1.0 / HEAD_DIM) + NORM_EPS))
    return xj * r * gain


def _rope(xj, cos, sin_signed):
    lane = lax.broadcasted_iota(jnp.int32, (1, LANES), 1)
    upper = (lane & AXIS_PAIRS) != 0
    partner = jnp.where(upper, pltpu.roll(xj, AXIS_PAIRS, 1), pltpu.roll(xj, LANES - AXIS_PAIRS, 1))
    return xj * cos + partner * sin_signed


def _inproj_kernel(*refs, rope):
    if rope:
        (x_ref, mod_ref, g_ref, w_ref, qg_ref, kg_ref, cos_ref, sin_ref,
         cv_ref, cb_ref, u_ref, q_ref, k_ref, v_ref) = refs
    else:
        (x_ref, mod_ref, g_ref, w_ref, qg_ref, kg_ref,
         cv_ref, cb_ref, u_ref, q_ref, k_ref, v_ref) = refs
    mod = mod_ref[...]
    m = _rms_modulate(x_ref[...], g_ref[...], mod[:, 0:D_MODEL], mod[:, D_MODEL:2 * D_MODEL])
    p = jnp.dot(m.astype(BF16), w_ref[...], preferred_element_type=F32)
    cv_ref[...] = p[:, 2 * D_CONV:3 * D_CONV] * p[:, 0:D_CONV]
    cb_ref[...] = p[:, D_CONV:2 * D_CONV]
    u_ref[...] = p[:, 3 * D_CONV:3 * D_CONV + D_SSM]
    q0 = 3 * D_CONV + D_SSM
    k0 = q0 + D_ATTN
    v0 = k0 + 2 * LANES
    for j in range(D_ATTN // LANES):
        xj = _head_norm(p[:, q0 + j * LANES:q0 + (j + 1) * LANES], qg_ref[...])
        if rope:
            xj = _rope(xj, cos_ref[...], sin_ref[...])
        q_ref[:, j * LANES:(j + 1) * LANES] = (xj * HEAD_DIM ** -0.5).astype(BF16)
    for j in range(N_KV_HEADS):
        xj = _head_norm(p[:, k0 + j * LANES:k0 + (j + 1) * LANES], kg_ref[...])
        if rope:
            xj = _rope(xj, cos_ref[...], sin_ref[...])
        k_ref[:, j * LANES:(j + 1) * LANES] = xj.astype(BF16)
    v_ref[...] = p[:, v0:v0 + 2 * LANES].astype(BF16)


def _in_projection(h, mod_l, mod_row, norm_g, w_in2, qg, kg, rope_tabs, tm):
    bsz, t, d = h.shape
    rope = rope_tabs is not None
    row_map = lambda b, i: (b, i, 0)
    in_specs = [pl.BlockSpec((None, tm, d), row_map),
                pl.BlockSpec((None, 1, 6 * d), lambda b, i: (mod_row(b), 0, 0)),
                _const_spec((1, d)),
                _const_spec((d, D_PROJ)),
                _const_spec((1, LANES)),
                _const_spec((1, LANES))]
    args = [h, mod_l, norm_g, w_in2, qg, kg]
    if rope:
        in_specs += [pl.BlockSpec((tm, LANES), lambda b, i: (i, 0))] * 2
        args += list(rope_tabs)
    out_shape = (jax.ShapeDtypeStruct((bsz, t, D_CONV), F32),
                 jax.ShapeDtypeStruct((bsz, t, D_CONV), F32),
                 jax.ShapeDtypeStruct((bsz, t, D_SSM), F32),
                 jax.ShapeDtypeStruct((bsz, t, D_ATTN), BF16),
                 jax.ShapeDtypeStruct((bsz, t, 2 * LANES), BF16),
                 jax.ShapeDtypeStruct((bsz, t, 2 * LANES), BF16))
    out_specs = (pl.BlockSpec((None, tm, D_CONV), row_map),
                 pl.BlockSpec((None, tm, D_CONV), row_map),
                 pl.BlockSpec((None, tm, D_SSM), row_map),
                 pl.BlockSpec((None, tm, D_ATTN), row_map),
                 pl.BlockSpec((None, tm, 2 * LANES), row_map),
                 pl.BlockSpec((None, tm, 2 * LANES), row_map))
    return pl.pallas_call(
        functools.partial(_inproj_kernel, rope=rope),
        out_shape=out_shape, grid=(bsz, t // tm), in_specs=in_specs, out_specs=out_specs,
        compiler_params=_cparams(2),
        name="in_projection_lat" if rope else "in_projection_ctx",
    )(*args)


def _scan_kernel(ulf_ref, ucf_ref, ulr_ref, ucr_ref, bmat_ref, lam_ref, cmat_ref,
                 yfl_ref, yrl_ref, yfc_ref, yrc_ref,
                 uf_ref, ur_ref, sf_ref, sr_ref, hf_ref, hr_ref, *, steps, ctx_chunks):
    @pl.when(pl.program_id(0) == 0)
    def _():
        hf_ref[...] = jnp.zeros_like(hf_ref)
        hr_ref[...] = jnp.zeros_like(hr_ref)

    in_ctx = pl.program_id(0) < ctx_chunks
    for t in range(steps):
        rows = slice(t * SUBLANES, (t + 1) * SUBLANES)
        uf_ref[rows, :] = jnp.where(in_ctx, ucf_ref[:, t, :], ulf_ref[:, t, :])
        ur_ref[rows, :] = jnp.where(in_ctx, ucr_ref[:, t, :], ulr_ref[:, t, :])
    sf_ref[...] = jnp.dot(uf_ref[...].astype(BF16), bmat_ref[0], preferred_element_type=F32)
    sr_ref[...] = jnp.dot(ur_ref[...].astype(BF16), bmat_ref[1], preferred_element_type=F32)

    def lam_rows(r):
        return jnp.broadcast_to(lam_ref[r:r + 1, :], (SUBLANES, N_STATE))

    lf_re, lf_im, lr_re, lr_im = lam_rows(0), lam_rows(1), lam_rows(2), lam_rows(3)

    def step(s_ref, row, l_re, l_im, h_re, h_im):
        row = pl.multiple_of(row, SUBLANES)
        n_re = l_re * h_re - l_im * h_im + s_ref[pl.ds(row, SUBLANES), 0:N_STATE]
        n_im = l_re * h_im + l_im * h_re + s_ref[pl.ds(row, SUBLANES), N_STATE:2 * N_STATE]
        s_ref[pl.ds(row, SUBLANES), 0:N_STATE] = n_re
        s_ref[pl.ds(row, SUBLANES), N_STATE:2 * N_STATE] = n_im
        return n_re, n_im

    def body(t, carry):
        f_re, f_im, r_re, r_im = carry
        f_re, f_im = step(sf_ref, t * SUBLANES, lf_re, lf_im, f_re, f_im)
        r_re, r_im = step(sr_ref, (steps - 1 - t) * SUBLANES, lr_re, lr_im, r_re, r_im)
        return f_re, f_im, r_re, r_im

    init = (hf_ref[:, 0:N_STATE], hf_ref[:, N_STATE:2 * N_STATE],
            hr_ref[:, 0:N_STATE], hr_ref[:, N_STATE:2 * N_STATE])
    f_re, f_im, r_re, r_im = lax.fori_loop(0, steps, body, init)
    hf_ref[:, 0:N_STATE] = f_re
    hf_ref[:, N_STATE:2 * N_STATE] = f_im
    hr_ref[:, 0:N_STATE] = r_re
    hr_ref[:, N_STATE:2 * N_STATE] = r_im

    uf_ref[...] = jnp.dot(sf_ref[...].astype(BF16), cmat_ref[0], preferred_element_type=F32)
    ur_ref[...] = jnp.dot(sr_ref[...].astype(BF16), cmat_ref[1], preferred_element_type=F32)

    def write_out(yf_out, yr_out):
        for t in range(steps):
            rows = slice(t * SUBLANES, (t + 1) * SUBLANES)
            yf_out[:, t, :] = uf_ref[rows, :]
            yr_out[:, t, :] = ur_ref[rows, :]

    @pl.when(in_ctx)
    def _():
        write_out(yfc_ref, yrc_ref)

    @pl.when(jnp.logical_not(in_ctx))
    def _():
        write_out(yfl_ref, yrl_ref)


def _ssm_scan(u_lat, u_ctx, bmat, lamv, cmat, steps):
    bsz = u_lat.shape[0]
    rows = steps * bsz
    nl, nc = u_lat.shape[1] // steps, u_ctx.shape[1] // steps
    blk = lambda f: pl.BlockSpec((bsz, steps, D_SSM), lambda s: (0, f(s), 0))
    lat_f = lambda s: jnp.maximum(s - nc, 0)
    ctx_f = lambda s: jnp.minimum(s, nc - 1)
    lat_r = lambda s: jnp.minimum(nl + nc - 1 - s, nl - 1)
    ctx_r = lambda s: jnp.maximum(nc - 1 - s, 0)
    return pl.pallas_call(
        functools.partial(_scan_kernel, steps=steps, ctx_chunks=nc),
        out_shape=(jax.ShapeDtypeStruct(u_lat.shape, F32),) * 2 + (jax.ShapeDtypeStruct(u_ctx.shape, F32),) * 2,
        grid=(nl + nc,),
        in_specs=[blk(lat_f), blk(ctx_f), blk(lat_r), blk(ctx_r),
                  _const_spec((2, D_SSM, 2 * N_STATE)),
                  _const_spec((4, N_STATE)),
                  _const_spec((2, 2 * N_STATE, D_SSM))],
        out_specs=(blk(lat_f), blk(lat_r), blk(ctx_f), blk(ctx_r)),
        scratch_shapes=[pltpu.VMEM((rows, D_SSM), F32), pltpu.VMEM((rows, D_SSM), F32),
                        pltpu.VMEM((rows, 2 * N_STATE), F32), pltpu.VMEM((rows, 2 * N_STATE), F32),
                        pltpu.VMEM((SUBLANES, 2 * N_STATE), F32), pltpu.VMEM((SUBLANES, 2 * N_STATE), F32)],
        compiler_params=_cparams(1),
        name="s5_scan",
    )(u_lat, u_ctx, u_lat, u_ctx, bmat, lamv, cmat)


def _attn_kernel(sink_ref, q_ref, *refs, window):
    if window:
        kp_ref, ko_ref, kn_ref, vp_ref, vo_ref, vn_ref, kc_ref, vc_ref, o_ref = refs
    else:
        kc_ref, vc_ref, o_ref = refs
    n = pl.program_id(1)
    last = pl.num_programs(1) - 1
    rows = (N_Q_HEADS // N_KV_HEADS) * BLOCK
    lane = lax.broadcasted_iota(jnp.int32, (1, LANES), 1)
    lo = lane < HEAD_DIM
    qi = lax.broadcasted_iota(jnp.int32, (rows, BLOCK), 0) & (BLOCK - 1)
    kj = lax.broadcasted_iota(jnp.int32, (rows, BLOCK), 1)
    head_of_row = lax.broadcasted_iota(jnp.int32, (rows, 1), 0) // BLOCK
    zero = jnp.zeros((), BF16)
    nt = (((1,), (1,)), ((), ()))
    for g in range(N_KV_HEADS):
        cols = slice(g * LANES, (g + 1) * LANES)
        stacked = []
        for j in (2 * g, 2 * g + 1):
            qp = q_ref[:, j * LANES:(j + 1) * LANES]
            stacked += [jnp.where(lo, qp, zero), jnp.where(lo, zero, qp)]
        qs = jnp.concatenate(stacked, axis=0)
        scores, values = [], []
        if window:
            s = lax.dot_general(qs, kp_ref[:, cols], nt, preferred_element_type=F32)
            scores.append(jnp.where((kj >= qi) & (n > 0), s, NEG_INF))
            scores.append(lax.dot_general(qs, ko_ref[:, cols], nt, preferred_element_type=F32))
            s = lax.dot_general(qs, kn_ref[:, cols], nt, preferred_element_type=F32)
            scores.append(jnp.where((kj <= qi) & (n < last), s, NEG_INF))
            values += [vp_ref[:, cols], vo_ref[:, cols], vn_ref[:, cols]]
        scores.append(lax.dot_general(qs, kc_ref[:, cols], nt, preferred_element_type=F32))
        values.append(vc_ref[:, cols])
        sink = jnp.zeros((rows, 1), F32)
        for r in range(N_Q_HEADS // N_KV_HEADS):
            sink = jnp.where(head_of_row == r, sink_ref[g * (N_Q_HEADS // N_KV_HEADS) + r], sink)
        s_all = jnp.concatenate(scores, axis=1)
        m = jnp.maximum(sink, jnp.max(s_all, axis=-1, keepdims=True))
        e = jnp.exp(s_all - m)
        denom = jnp.exp(sink - m) + jnp.sum(e, axis=-1, keepdims=True)
        acc = jnp.dot(e.astype(BF16), jnp.concatenate(values, axis=0), preferred_element_type=F32)
        out = acc / denom
        for jj in range(2):
            blk = jnp.where(lo, out[(2 * jj) * BLOCK:(2 * jj + 1) * BLOCK],
                            out[(2 * jj + 1) * BLOCK:(2 * jj + 2) * BLOCK])
            o_ref[:, (2 * g + jj) * LANES:(2 * g + jj + 1) * LANES] = blk.astype(BF16)


def _attention(sink, q, k2, v2, kc2, vc2, window):
    bsz, t, _ = q.shape
    n_ctx = kc2.shape[1]
    nb = t // BLOCK
    kw = 2 * LANES
    blk = lambda f: pl.BlockSpec((None, BLOCK, kw), f)
    in_specs = [pl.BlockSpec(memory_space=pltpu.SMEM),
                pl.BlockSpec((None, BLOCK, D_ATTN), lambda b, n: (b, n, 0))]
    args = [sink, q]
    if window:
        prev = lambda b, n: (b, jnp.maximum(n - 1, 0), 0)
        own = lambda b, n: (b, n, 0)
        nxt = lambda b, n: (b, jnp.minimum(n + 1, nb - 1), 0)
        in_specs += [blk(prev), blk(own), blk(nxt)] * 2
        args += [k2, k2, k2, v2, v2, v2]
    ctx_spec = pl.BlockSpec((None, n_ctx, kw), lambda b, n: (b, 0, 0))
    in_specs += [ctx_spec, ctx_spec]
    args += [kc2, vc2]
    return pl.pallas_call(
        functools.partial(_attn_kernel, window=window),
        out_shape=jax.ShapeDtypeStruct((bsz, t, D_ATTN), BF16),
        grid=(bsz, nb), in_specs=in_specs,
        out_specs=pl.BlockSpec((None, BLOCK, D_ATTN), lambda b, n: (b, n, 0)),
        compiler_params=_cparams(2),
        name="window_attention" if window else "context_attention",
    )(*args)


def _mixout_kernel(h_ref, mod_ref, cv_ref, cvp_ref, cvn_ref, cb_ref, u_ref, yf_ref, yr_ref, att_ref,
                   convw_ref, d_ref, wglu_ref, bglu_ref, wout_ref, g2_ref, rw_ref, rb_ref,
                   hmid_ref, t_ref, gates_ref):
    i = pl.program_id(1)
    last = pl.num_programs(1) - 1
    tm = cv_ref.shape[0]
    cv = cv_ref[...]
    row = lax.broadcasted_iota(jnp.int32, (tm, 1), 0)
    before = jnp.where(i > 0, cvp_ref[SUBLANES - 1:SUBLANES, :], 0.0)
    after = jnp.where(i < last, cvn_ref[0:1, :], 0.0)
    down = jnp.where(row == 0, before, pltpu.roll(cv, 1, 0))
    up = jnp.where(row == tm - 1, after, pltpu.roll(cv, tm - 1, 0))
    w = convw_ref[...]
    y_conv = cb_ref[...] * (w[0:1, :] * down + w[1:2, :] * cv + w[2:3, :] * up)
    y = d_ref[...] * u_ref[...] + yf_ref[...] + yr_ref[...]
    gl = jax.nn.gelu(y)
    z = jnp.dot(gl.astype(BF16), wglu_ref[...], preferred_element_type=F32) + bglu_ref[...]
    y_ssm = gl * jax.nn.sigmoid(z)
    mix = (jnp.dot(y_conv.astype(BF16), wout_ref[0:D_CONV, :], preferred_element_type=F32)
           + jnp.dot(y_ssm.astype(BF16), wout_ref[D_CONV:D_CONV + D_SSM, :], preferred_element_type=F32)
           + jnp.dot(att_ref[...], wout_ref[D_CONV + D_SSM:, :], preferred_element_type=F32))
    mod = mod_ref[...]
    h_mid = h_ref[...] + mod[:, 2 * D_MODEL:3 * D_MODEL] * mix
    hmid_ref[...] = h_mid
    t = _rms_modulate(h_mid, g2_ref[...], mod[:, 3 * D_MODEL:4 * D_MODEL], mod[:, 4 * D_MODEL:5 * D_MODEL])
    t_ref[...] = t.astype(BF16)
    t_hi = t.astype(BF16)
    t_lo = (t - t_hi.astype(F32)).astype(BF16)
    a = jnp.dot(t_hi, rw_ref[...], preferred_element_type=F32)
    logits = a + pltpu.roll(a, LANES - N_EXPERTS, 1) + jnp.dot(t_lo, rw_ref[...], preferred_element_type=F32)
    scores = jax.nn.sigmoid(logits)
    lane = lax.broadcasted_iota(jnp.int32, (1, LANES), 1)
    eidx = lane.astype(F32)
    biased = jnp.where(lane < N_EXPERTS, scores + rb_ref[...], -jnp.inf)
    sel = jnp.zeros(scores.shape, F32)
    for _ in range(TOP_K):
        best = jnp.max(biased, axis=-1, keepdims=True)
        first = jnp.min(jnp.where(biased == best, eidx, float(LANES)), axis=-1, keepdims=True)
        pick = eidx == first
        sel = jnp.where(pick, scores, sel)
        biased = jnp.where(pick, -jnp.inf, biased)
    gates_ref[...] = sel / jnp.sum(sel, axis=-1, keepdims=True) * ROUTED_SCALE


def _mixer_output(h, mod_l, mod_row, cv, cb, u, yf, yr, att,
                  conv_w, ssm_d, w_glu, b_glu, w_out, norm_g, router_w, router_b, tm):
    bsz, t, d = h.shape
    halo = tm // SUBLANES
    n_halo = t // SUBLANES
    row_map = lambda b, i: (b, i, 0)
    u_spec = pl.BlockSpec((None, tm, D_SSM), row_map)
    in_specs = [pl.BlockSpec((None, tm, d), row_map),
                pl.BlockSpec((None, 1, 6 * d), lambda b, i: (mod_row(b), 0, 0)),
                pl.BlockSpec((None, tm, D_CONV), row_map),
                pl.BlockSpec((None, SUBLANES, D_CONV), lambda b, i: (b, jnp.maximum(i * halo - 1, 0), 0)),
                pl.BlockSpec((None, SUBLANES, D_CONV), lambda b, i: (b, jnp.minimum((i + 1) * halo, n_halo - 1), 0)),
                pl.BlockSpec((None, tm, D_CONV), row_map),
                u_spec, u_spec, u_spec,
                pl.BlockSpec((None, tm, D_ATTN), row_map),
                _const_spec((3, D_CONV)),
                _const_spec((1, D_SSM)),
                _const_spec((D_SSM, D_SSM)),
                _const_spec((1, D_SSM)),
                _const_spec((d, d)),
                _const_spec((1, d)),
                _const_spec((d, LANES)),
                _const_spec((1, LANES))]
    out_shape = (jax.ShapeDtypeStruct((bsz, t, d), F32),
                 jax.ShapeDtypeStruct((bsz, t, d), BF16),
                 jax.ShapeDtypeStruct((bsz, t, LANES), F32))
    out_specs = (pl.BlockSpec((None, tm, d), row_map),
                 pl.BlockSpec((None, tm, d), row_map),
                 pl.BlockSpec((None, tm, LANES), row_map))
    return pl.pallas_call(
        _mixout_kernel, out_shape=out_shape, grid=(bsz, t // tm),
        in_specs=in_specs, out_specs=out_specs,
        compiler_params=_cparams(2),
        name="mixer_output",
    )(h, mod_l, cv, cv, cv, cb, u, yf, yr, att,
      conv_w, ssm_d, w_glu, b_glu, w_out, norm_g, router_w, router_b)


EXPERTS_PER_CHUNK = 4


def _moe_kernel(t_ref, gates_ref, hmid_ref, mod_ref, wg_ref, wu_ref, wd_ref,
                wsg_ref, wsu_ref, wsd_ref, o_ref, acc_ref):
    t = t_ref[...]
    gates = gates_ref[...]
    hs = _silu(jnp.dot(t, wsg_ref[...], preferred_element_type=F32)) * jnp.dot(
        t, wsu_ref[...], preferred_element_type=F32)
    acc_ref[...] = jnp.dot(hs.astype(BF16), wsd_ref[...], preferred_element_type=F32)
    width = EXPERTS_PER_CHUNK * D_EXPERT
    for c in range(N_EXPERTS // EXPERTS_PER_CHUNK):
        cols = slice(c * width, (c + 1) * width)
        hid = _silu(jnp.dot(t, wg_ref[:, cols], preferred_element_type=F32)) * jnp.dot(
            t, wu_ref[:, cols], preferred_element_type=F32)
        gated = []
        for e in range(EXPERTS_PER_CHUNK):
            k = c * EXPERTS_PER_CHUNK + e
            gated.append((hid[:, e * D_EXPERT:(e + 1) * D_EXPERT] * gates[:, k:k + 1]).astype(BF16))
        acc_ref[...] += jnp.dot(jnp.concatenate(gated, axis=1), wd_ref[cols, :],
                                preferred_element_type=F32)
    mod = mod_ref[...]
    o_ref[...] = hmid_ref[...] + mod[:, 5 * D_MODEL:6 * D_MODEL] * acc_ref[...]


def _moe(t, gates, hmid, mod_l, mod_row, wg, wu, wd, wsg, wsu, wsd, tm):
    bsz, n, d = hmid.shape
    row_map = lambda b, i: (b, i, 0)
    return pl.pallas_call(
        _moe_kernel,
        out_shape=jax.ShapeDtypeStruct((bsz, n, d), F32),
        grid=(bsz, n // tm),
        in_specs=[pl.BlockSpec((None, tm, d), row_map),
                  pl.BlockSpec((None, tm, LANES), row_map),
                  pl.BlockSpec((None, tm, d), row_map),
                  pl.BlockSpec((None, 1, 6 * d), lambda b, i: (mod_row(b), 0, 0)),
                  _const_spec(wg.shape), _const_spec(wu.shape), _const_spec(wd.shape),
                  _const_spec(wsg.shape), _const_spec(wsu.shape), _const_spec(wsd.shape)],
        out_specs=pl.BlockSpec((None, tm, d), row_map),
        scratch_shapes=[pltpu.VMEM((tm, d), F32)],
        compiler_params=_cparams(2),
        name="moe",
    )(t, gates, hmid, mod_l, wg, wu, wd, wsg, wsu, wsd)


def _rope_tables(n_tokens):
    rows = n_tokens // GRID_W
    row = jnp.repeat(jnp.arange(rows, dtype=F32), GRID_W)
    col = jnp.tile(jnp.arange(GRID_W, dtype=F32), rows)
    inv_freq = ROPE_BASE ** (-jnp.arange(AXIS_PAIRS, dtype=F32) / AXIS_PAIRS)
    ar, ac = row[:, None] * inv_freq, col[:, None] * inv_freq
    cos = jnp.concatenate([jnp.cos(ar), jnp.cos(ar), jnp.cos(ac), jnp.cos(ac)], axis=1)
    sin = jnp.concatenate([-jnp.sin(ar), jnp.sin(ar), -jnp.sin(ac), jnp.sin(ac)], axis=1)
    return jnp.tile(cos, (1, LANES // HEAD_DIM)), jnp.tile(sin, (1, LANES // HEAD_DIM))


def _ssm_params(lam_re, lam_im, log_dt, b_re, b_im, c_re, c_im):
    lr, li = lam_re.astype(F32), lam_im.astype(F32)
    dt = jnp.exp(log_dt.astype(F32))[..., None]
    mag = jnp.exp(lr * dt)
    ar, ai = mag * jnp.cos(li * dt), mag * jnp.sin(li * dt)
    den = lr * lr + li * li
    qr = ((ar - 1.0) * lr + ai * li) / den
    qi = (ai * lr - (ar - 1.0) * li) / den
    br, bi = b_re.astype(F32), b_im.astype(F32)
    bbar_re = qr[..., None] * br - qi[..., None] * bi
    bbar_im = qr[..., None] * bi + qi[..., None] * br
    eye = jnp.eye(SSM_GROUPS, dtype=F32)
    drive = lambda m: jnp.einsum('gk,dgph->dghkp', eye, m).reshape(2, D_SSM, N_STATE)
    read = lambda m: jnp.einsum('gk,dghp->dgpkh', eye, m.astype(F32)).reshape(2, N_STATE, D_SSM)
    bmat = jnp.concatenate([drive(bbar_re), drive(bbar_im)], axis=2)
    cmat = jnp.concatenate([read(c_re), -read(c_im)], axis=1)
    lam_flat = lambda m: m.reshape(2, N_STATE)
    ar, ai = lam_flat(ar), lam_flat(ai)
    lamv = jnp.stack([ar[0], ai[0], ar[1], ai[1]])
    return bmat.astype(BF16), lamv, cmat.astype(BF16)


def _dup_heads(w):
    d = w.shape[0]
    w = w.reshape(d, N_KV_HEADS, 1, HEAD_DIM)
    return jnp.broadcast_to(w, (d, N_KV_HEADS, 2, HEAD_DIM)).reshape(d, 2 * N_KV_HEADS * HEAD_DIM)


def kernel(x, c, ctx, c_ctx, w_mod, b_mod, norm1_g, norm2_g, w_in, conv_w, ssm_lam_re, ssm_lam_im, ssm_log_dt, ssm_b_re, ssm_b_im, ssm_c_re, ssm_c_im, ssm_d, w_glu, b_glu, q_norm_g, k_norm_g, attn_sink, w_out, router_w, router_bias, w_exp_gate, w_exp_up, w_exp_down, w_sh_gate, w_sh_up, w_sh_down):
    bsz, n_lat, d = x.shape
    n_ctx = ctx.shape[1]
    depth = w_mod.shape[0]
    assert bsz == SUBLANES and d == D_MODEL

    mod_rows = 2 * SUBLANES
    cvec = jnp.zeros((mod_rows, d), F32).at[:bsz].set(c).at[bsz].set(c_ctx)
    mod = _modulation(cvec, w_mod, b_mod).reshape(depth, mod_rows, 1, 6 * d)
    lat_row = lambda b: b
    ctx_row = lambda b: bsz
    rope_tabs = _rope_tables(n_lat)

    tm_lat, tm_ctx, scan_steps = 512, 256, 64
    kv0 = 3 * D_CONV + D_SSM + D_ATTN
    h, hc = x, ctx
    for l in range(depth):
        ctx_out = l < depth - 1
        mod_l = mod[l]
        w_in2 = jnp.concatenate([w_in[l][:, :kv0], _dup_heads(w_in[l][:, kv0:kv0 + 2 * HEAD_DIM]),
                                 _dup_heads(w_in[l][:, kv0 + 2 * HEAD_DIM:])], axis=1).astype(BF16)
        qg = jnp.tile(q_norm_g[l].astype(F32), LANES // HEAD_DIM)[None, :]
        kg = jnp.tile(k_norm_g[l].astype(F32), LANES // HEAD_DIM)[None, :]
        n1 = norm1_g[l][None, :]
        cv, cb, u_lat, q, k2, v2 = _in_projection(h, mod_l, lat_row, n1, w_in2, qg, kg, rope_tabs, tm_lat)
        cvc, cbc, u_ctx, qc, kc2, vc2 = _in_projection(hc, mod_l, ctx_row, n1, w_in2, qg, kg, None, tm_ctx)

        bmat, lamv, cmat = _ssm_params(ssm_lam_re[l], ssm_lam_im[l], ssm_log_dt[l], ssm_b_re[l],
                                       ssm_b_im[l], ssm_c_re[l], ssm_c_im[l])
        yf, yr, yfc, yrc = _ssm_scan(u_lat, u_ctx, bmat, lamv, cmat, scan_steps)

        sink = attn_sink[l].astype(F32)
        att = _attention(sink, q, k2, v2, kc2, vc2, window=True)

        rw = router_w[l].astype(F32)
        rw_hi = rw.astype(BF16)
        rw_lo = (rw - rw_hi.astype(F32)).astype(BF16)
        rw2 = jnp.concatenate([rw_hi, rw_lo, jnp.zeros((d, LANES - 2 * N_EXPERTS), BF16)], axis=1)
        rb = jnp.zeros((1, LANES), F32).at[0, :N_EXPERTS].set(router_bias[l].astype(F32))
        post = dict(conv_w=conv_w[l], ssm_d=ssm_d[l][None, :], w_glu=w_glu[l].astype(BF16),
                    b_glu=b_glu[l][None, :], w_out=w_out[l].astype(BF16), norm_g=norm2_g[l][None, :],
                    router_w=rw2, router_b=rb)
        experts = (w_exp_gate[l].astype(BF16).reshape(d, -1), w_exp_up[l].astype(BF16).reshape(d, -1),
                   w_exp_down[l].astype(BF16).reshape(-1, d), w_sh_gate[l].astype(BF16),
                   w_sh_up[l].astype(BF16), w_sh_down[l].astype(BF16))
        hmid, t, gates = _mixer_output(h, mod_l, lat_row, cv, cb, u_lat, yf, yr, att, tm=tm_lat, **post)
        h_new = _moe(t, gates, hmid, mod_l, lat_row, *experts, tm=tm_lat)
        if ctx_out:
            attc = _attention(sink, qc, None, None, kc2, vc2, window=False)
            hmid_c, tc, gates_c = _mixer_output(hc, mod_l, ctx_row, cvc, cbc, u_ctx, yfc, yrc, attc,
                                                tm=tm_ctx, **post)
            hc = _moe(tc, gates_c, hmid_c, mod_l, ctx_row, *experts, tm=tm_ctx)
        h = h_new
    return h
```

```python
import functools
import math

import jax
import jax.numpy as jnp
from jax import lax
from jax.experimental import pallas as pl
from jax.experimental.pallas import tpu as pltpu

F32 = jnp.float32
BF16 = jnp.bfloat16
HIGHEST = lax.Precision.HIGHEST

D_MODEL = 1024
D_CONV = 256
D_SSM = 256
SSM_GROUP = 16
SSM_GROUPS = 16
SSM_STATE = 64
N_STATE = SSM_GROUPS * SSM_STATE
HEAD_DIM = 64
D_ATTN = 512
N_Q_HEADS = 8
N_KV_HEADS = 2
AXIS_PAIRS = 16
ROPE_BASE = 10000.0
GRID_W = 64
BLOCK = 128
N_EXPERTS = 32
TOP_K = 4
D_EXPERT = 128
ROUTED_SCALE = 2.5
NORM_EPS = 1e-6
NEG_INF = -1e30
D_PROJ = 3 * D_CONV + D_SSM + D_ATTN + 4 * N_KV_HEADS * HEAD_DIM
LANES = 128
SUBLANES = 8
VMEM_LIMIT = 56 * 1024 * 1024


def _cparams(n_axes):
    return pltpu.CompilerParams(dimension_semantics=("arbitrary",) * n_axes,
                                vmem_limit_bytes=VMEM_LIMIT)


def _const_spec(shape):
    nd = len(shape)
    return pl.BlockSpec(shape, lambda *_: (0,) * nd, pipeline_mode=pl.Buffered(1))


def _silu(x):
    return x * jax.nn.sigmoid(x)


def _rms_modulate(x, gain, shift, scale):
    ms = jnp.mean(x * x, axis=-1, keepdims=True)
    return (x * lax.rsqrt(ms + NORM_EPS) * gain) * (1.0 + scale) + shift


def _mod_kernel(c_ref, w_ref, b_ref, o_ref):
    cv = c_ref[...]
    o_ref[...] = jnp.dot(_silu(cv), w_ref[...], precision=HIGHEST,
                         preferred_element_type=F32) + b_ref[...]


def _modulation(cvec, w_mod, b_mod):
    depth, d, n = w_mod.shape
    rows = cvec.shape[0]
    tn = 1536
    return pl.pallas_call(
        _mod_kernel,
        out_shape=jax.ShapeDtypeStruct((depth, rows, n), F32),
        grid=(depth, n // tn),
        in_specs=[pl.BlockSpec((rows, d), lambda l, j: (0, 0)),
                  pl.BlockSpec((None, d, tn), lambda l, j: (l, 0, j)),
                  pl.BlockSpec((None, 1, tn), lambda l, j: (l, 0, j))],
        out_specs=pl.BlockSpec((None, rows, tn), lambda l, j: (l, 0, j)),
        compiler_params=_cparams(2),
        name="modulation",
    )(cvec, w_mod, b_mod.reshape(depth, 1, n))


def _head_norm(xj, gain):
    lane = lax.broadcasted_iota(jnp.int32, (1, LANES), 1)
    lo = lane < HEAD_DIM
    sq = xj * xj
    s_lo = jnp.sum(jnp.where(lo, sq, 0.0), axis=-1, keepdims=True)
    s_hi = jnp.sum(jnp.where(lo, 0.0, sq), axis=-1, keepdims=True)
    r = jnp.where(lo, lax.rsqrt(s_lo * (1.0 / HEAD_DIM) + NORM_EPS),
                  lax.rsqrt(s_hi * (1.0 / HEAD_DIM) + NORM_EPS))
    return xj * r * gain


def _rope(xj, cos, sin_signed):
    lane = lax.broadcasted_iota(jnp.int32, (1, LANES), 1)
    upper = (lane & AXIS_PAIRS) != 0
    partner = jnp.where(upper, pltpu.roll(xj, AXIS_PAIRS, 1), pltpu.roll(xj, LANES - AXIS_PAIRS, 1))
    return xj * cos + partner * sin_signed


def _inproj_kernel(*refs, rope):
    if rope:
        (x_ref, mod_ref, g_ref, w_ref, qg_ref, kg_ref, cos_ref, sin_ref,
         cv_ref, cb_ref, u_ref, q_ref, k_ref, v_ref) = refs
    else:
        (x_ref, mod_ref, g_ref, w_ref, qg_ref, kg_ref,
         cv_ref, cb_ref, u_ref, q_ref, k_ref, v_ref) = refs
    mod = mod_ref[...]
    m = _rms_modulate(x_ref[...], g_ref[...], mod[:, 0:D_MODEL], mod[:, D_MODEL:2 * D_MODEL])
    mb = m.astype(BF16)

    def proj(c0, width):
        return jnp.dot(mb, w_ref[:, c0:c0 + width], preferred_element_type=F32)

    cv_ref[...] = proj(2 * D_CONV, D_CONV) * proj(0, D_CONV)
    cb_ref[...] = proj(D_CONV, D_CONV)
    u_ref[...] = proj(3 * D_CONV, D_SSM)
    q0 = 3 * D_CONV + D_SSM
    k0 = q0 + D_ATTN
    v0 = k0 + 2 * LANES

    def heads(c0, gain_ref, out_ref, o0, scale):
        pair = proj(c0, 2 * LANES)
        for j in range(2):
            xj = _head_norm(pair[:, j * LANES:(j + 1) * LANES], gain_ref[...])
            if rope:
                xj = _rope(xj, cos_ref[...], sin_ref[...])
            if scale != 1.0:
                xj = xj * scale
            out_ref[:, o0 + j * LANES:o0 + (j + 1) * LANES] = xj.astype(BF16)

    for jj in range(D_ATTN // (2 * LANES)):
        heads(q0 + jj * 2 * LANES, qg_ref, q_ref, jj * 2 * LANES, HEAD_DIM ** -0.5)
    heads(k0, kg_ref, k_ref, 0, 1.0)
    v_ref[...] = proj(v0, 2 * LANES).astype(BF16)


def _in_projection(h, mod_l, mod_row, norm_g, w_in2, qg, kg, rope_tabs, tm):
    bsz, t, d = h.shape
    rope = rope_tabs is not None
    row_map = lambda b, i: (b, i, 0)
    in_specs = [pl.BlockSpec((None, tm, d), row_map),
                pl.BlockSpec((None, 1, 6 * d), lambda b, i: (mod_row(b), 0, 0)),
                _const_spec((1, d)),
                _const_spec((d, D_PROJ)),
                _const_spec((1, LANES)),
                _const_spec((1, LANES))]
    args = [h, mod_l, norm_g, w_in2, qg, kg]
    if rope:
        in_specs += [pl.BlockSpec((tm, LANES), lambda b, i: (i, 0))] * 2
        args += list(rope_tabs)
    out_shape = (jax.ShapeDtypeStruct((bsz, t, D_CONV), F32),
                 jax.ShapeDtypeStruct((bsz, t, D_CONV), F32),
                 jax.ShapeDtypeStruct((bsz, t, D_SSM), F32),
                 jax.ShapeDtypeStruct((bsz, t, D_ATTN), BF16),
                 jax.ShapeDtypeStruct((bsz, t, 2 * LANES), BF16),
                 jax.ShapeDtypeStruct((bsz, t, 2 * LANES), BF16))
    out_specs = (pl.BlockSpec((None, tm, D_CONV), row_map),
                 pl.BlockSpec((None, tm, D_CONV), row_map),
                 pl.BlockSpec((None, tm, D_SSM), row_map),
                 pl.BlockSpec((None, tm, D_ATTN), row_map),
                 pl.BlockSpec((None, tm, 2 * LANES), row_map),
                 pl.BlockSpec((None, tm, 2 * LANES), row_map))
    return pl.pallas_call(
        functools.partial(_inproj_kernel, rope=rope),
        out_shape=out_shape, grid=(bsz, t // tm), in_specs=in_specs, out_specs=out_specs,
        compiler_params=_cparams(2),
        name="in_projection_lat" if rope else "in_projection_ctx",
    )(*args)


def _scan_kernel(ulf_ref, ucf_ref, ulr_ref, ucr_ref, bmat_ref, lam_ref, cmat_ref,
                 yfl_ref, yrl_ref, yfc_ref, yrc_ref,
                 uf_ref, ur_ref, yf_ref, yr_ref, sf0_ref, sf1_ref, sr0_ref, sr1_ref, hf_ref, hr_ref,
                 *, steps, ctx_chunks, n_chunks):
    j = pl.program_id(0)

    @pl.when(j == 0)
    def _():
        for ref in (sf0_ref, sf1_ref, sr0_ref, sr1_ref):
            ref[...] = jnp.zeros_like(ref)
        hf_ref[...] = jnp.zeros_like(hf_ref)
        hr_ref[...] = jnp.zeros_like(hr_ref)

    drive_ctx = jnp.minimum(j, n_chunks - 1) < ctx_chunks

    def lam_rows(r):
        return jnp.broadcast_to(lam_ref[r:r + 1, :], (SUBLANES, N_STATE))

    def phases(sf_cur, sr_cur, sf_prev, sr_prev):
        yf_ref[...] = jnp.dot(sf_cur[...].astype(BF16), cmat_ref[0], preferred_element_type=F32)
        yr_ref[...] = jnp.dot(sr_cur[...].astype(BF16), cmat_ref[1], preferred_element_type=F32)

        for t in range(steps):
            rows = slice(t * SUBLANES, (t + 1) * SUBLANES)
            uf_ref[rows, :] = jnp.where(drive_ctx, ucf_ref[:, t, :], ulf_ref[:, t, :])
            ur_ref[rows, :] = jnp.where(drive_ctx, ucr_ref[:, t, :], ulr_ref[:, t, :])
        sf_cur[...] = jnp.dot(uf_ref[...].astype(BF16), bmat_ref[0], preferred_element_type=F32)
        sr_cur[...] = jnp.dot(ur_ref[...].astype(BF16), bmat_ref[1], preferred_element_type=F32)

        lf_re, lf_im, lr_re, lr_im = lam_rows(0), lam_rows(1), lam_rows(2), lam_rows(3)

        def step(s_ref, t, l_re, l_im, h_re, h_im):
            rows = slice(t * SUBLANES, (t + 1) * SUBLANES)
            n_re = l_re * h_re - l_im * h_im + s_ref[rows, 0:N_STATE]
            n_im = l_re * h_im + l_im * h_re + s_ref[rows, N_STATE:2 * N_STATE]
            s_ref[rows, 0:N_STATE] = n_re
            s_ref[rows, N_STATE:2 * N_STATE] = n_im
            return n_re, n_im

        f_re, f_im = hf_ref[:, 0:N_STATE], hf_ref[:, N_STATE:2 * N_STATE]
        r_re, r_im = hr_ref[:, 0:N_STATE], hr_ref[:, N_STATE:2 * N_STATE]
        for t in range(steps):
            f_re, f_im = step(sf_prev, t, lf_re, lf_im, f_re, f_im)
            r_re, r_im = step(sr_prev, steps - 1 - t, lr_re, lr_im, r_re, r_im)
        hf_ref[:, 0:N_STATE] = f_re
        hf_ref[:, N_STATE:2 * N_STATE] = f_im
        hr_ref[:, 0:N_STATE] = r_re
        hr_ref[:, N_STATE:2 * N_STATE] = r_im

    @pl.when(lax.rem(j, 2) == 0)
    def _():
        phases(sf0_ref, sr0_ref, sf1_ref, sr1_ref)

    @pl.when(lax.rem(j, 2) == 1)
    def _():
        phases(sf1_ref, sr1_ref, sf0_ref, sr0_ref)

    def write_out(yf_out, yr_out):
        for t in range(steps):
            rows = slice(t * SUBLANES, (t + 1) * SUBLANES)
            yf_out[:, t, :] = yf_ref[rows, :]
            yr_out[:, t, :] = yr_ref[rows, :]

    @pl.when(jnp.logical_and(j >= 2, j - 2 < ctx_chunks))
    def _():
        write_out(yfc_ref, yrc_ref)

    @pl.when(j - 2 >= ctx_chunks)
    def _():
        write_out(yfl_ref, yrl_ref)


def _ssm_scan(u_lat, u_ctx, bmat, lamv, cmat, steps):
    bsz = u_lat.shape[0]
    rows = steps * bsz
    nl, nc = u_lat.shape[1] // steps, u_ctx.shape[1] // steps
    n = nl + nc
    blk = lambda f: pl.BlockSpec((bsz, steps, D_SSM), lambda j: (0, f(j), 0))
    lat_f = lambda p: jnp.maximum(p - nc, 0)
    ctx_f = lambda p: jnp.minimum(p, nc - 1)
    lat_r = lambda p: jnp.minimum(n - 1 - p, nl - 1)
    ctx_r = lambda p: jnp.maximum(nc - 1 - p, 0)
    drive = lambda f: blk(lambda j: f(jnp.minimum(j, n - 1)))
    read = lambda f: blk(lambda j: f(jnp.maximum(j - 2, 0)))
    state = pltpu.VMEM((rows, 2 * N_STATE), F32)
    chunk = pltpu.VMEM((rows, D_SSM), F32)
    carry = pltpu.VMEM((SUBLANES, 2 * N_STATE), F32)
    return pl.pallas_call(
        functools.partial(_scan_kernel, steps=steps, ctx_chunks=nc, n_chunks=n),
        out_shape=(jax.ShapeDtypeStruct(u_lat.shape, F32),) * 2 + (jax.ShapeDtypeStruct(u_ctx.shape, F32),) * 2,
        grid=(n + 2,),
        in_specs=[drive(lat_f), drive(ctx_f), drive(lat_r), drive(ctx_r),
                  _const_spec((2, D_SSM, 2 * N_STATE)),
                  _const_spec((4, N_STATE)),
                  _const_spec((2, 2 * N_STATE, D_SSM))],
        out_specs=(read(lat_f), read(lat_r), read(ctx_f), read(ctx_r)),
        scratch_shapes=[chunk, chunk, chunk, chunk, state, state, state, state, carry, carry],
        compiler_params=_cparams(1),
        name="s5_scan",
    )(u_lat, u_ctx, u_lat, u_ctx, bmat, lamv, cmat)


def _attn_kernel(sink_ref, q_ref, *refs, window):
    if window:
        kp_ref, ko_ref, kn_ref, vp_ref, vo_ref, vn_ref, kc_ref, vc_ref, o_ref = refs
    else:
        kc_ref, vc_ref, o_ref = refs
    n = pl.program_id(1)
    last = pl.num_programs(1) - 1
    rows = (N_Q_HEADS // N_KV_HEADS) * BLOCK
    lane = lax.broadcasted_iota(jnp.int32, (1, LANES), 1)
    lo = lane < HEAD_DIM
    qi = lax.broadcasted_iota(jnp.int32, (rows, BLOCK), 0) & (BLOCK - 1)
    kj = lax.broadcasted_iota(jnp.int32, (rows, BLOCK), 1)
    head_of_row = lax.broadcasted_iota(jnp.int32, (rows, 1), 0) // BLOCK
    zero = jnp.zeros((), BF16)
    nt = (((1,), (1,)), ((), ()))
    for g in range(N_KV_HEADS):
        cols = slice(g * LANES, (g + 1) * LANES)
        stacked = []
        for j in (2 * g, 2 * g + 1):
            qp = q_ref[:, j * LANES:(j + 1) * LANES]
            stacked += [jnp.where(lo, qp, zero), jnp.where(lo, zero, qp)]
        qs = jnp.concatenate(stacked, axis=0)
        scores, values = [], []
        if window:
            s = lax.dot_general(qs, kp_ref[:, cols], nt, preferred_element_type=F32)
            scores.append(jnp.where((kj >= qi) & (n > 0), s, NEG_INF))
            scores.append(lax.dot_general(qs, ko_ref[:, cols], nt, preferred_element_type=F32))
            s = lax.dot_general(qs, kn_ref[:, cols], nt, preferred_element_type=F32)
            scores.append(jnp.where((kj <= qi) & (n < last), s, NEG_INF))
            values += [vp_ref[:, cols], vo_ref[:, cols], vn_ref[:, cols]]
        scores.append(lax.dot_general(qs, kc_ref[:, cols], nt, preferred_element_type=F32))
        values.append(vc_ref[:, cols])
        sink = jnp.zeros((rows, 1), F32)
        for r in range(N_Q_HEADS // N_KV_HEADS):
            sink = jnp.where(head_of_row == r, sink_ref[g * (N_Q_HEADS // N_KV_HEADS) + r], sink)
        s_all = jnp.concatenate(scores, axis=1)
        m = jnp.maximum(sink, jnp.max(s_all, axis=-1, keepdims=True))
        e = jnp.exp(s_all - m)
        denom = jnp.exp(sink - m) + jnp.sum(e, axis=-1, keepdims=True)
        acc = jnp.dot(e.astype(BF16), jnp.concatenate(values, axis=0), preferred_element_type=F32)
        out = acc / denom
        for jj in range(2):
            blk = jnp.where(lo, out[(2 * jj) * BLOCK:(2 * jj + 1) * BLOCK],
                            out[(2 * jj + 1) * BLOCK:(2 * jj + 2) * BLOCK])
            o_ref[:, (2 * g + jj) * LANES:(2 * g + jj + 1) * LANES] = blk.astype(BF16)


def _attention(sink, q, k2, v2, kc2, vc2, window):
    bsz, t, _ = q.shape
    n_ctx = kc2.shape[1]
    nb = t // BLOCK
    kw = 2 * LANES
    blk = lambda f: pl.BlockSpec((None, BLOCK, kw), f)
    in_specs = [pl.BlockSpec(memory_space=pltpu.SMEM),
                pl.BlockSpec((None, BLOCK, D_ATTN), lambda b, n: (b, n, 0))]
    args = [sink, q]
    if window:
        prev = lambda b, n: (b, jnp.maximum(n - 1, 0), 0)
        own = lambda b, n: (b, n, 0)
        nxt = lambda b, n: (b, jnp.minimum(n + 1, nb - 1), 0)
        in_specs += [blk(prev), blk(own), blk(nxt)] * 2
        args += [k2, k2, k2, v2, v2, v2]
    ctx_spec = pl.BlockSpec((None, n_ctx, kw), lambda b, n: (b, 0, 0))
    in_specs += [ctx_spec, ctx_spec]
    args += [kc2, vc2]
    return pl.pallas_call(
        functools.partial(_attn_kernel, window=window),
        out_shape=jax.ShapeDtypeStruct((bsz, t, D_ATTN), BF16),
        grid=(bsz, nb), in_specs=in_specs,
        out_specs=pl.BlockSpec((None, BLOCK, D_ATTN), lambda b, n: (b, n, 0)),
        compiler_params=_cparams(2),
        name="window_attention" if window else "context_attention",
    )(*args)


def _mixout_kernel(h_ref, mod_ref, cv_ref, cvp_ref, cvn_ref, cb_ref, u_ref, yf_ref, yr_ref, att_ref,
                   convw_ref, d_ref, wglu_ref, bglu_ref, wout_ref, g2_ref, rw_ref, rb_ref,
                   hmid_ref, t_ref, gates_ref):
    i = pl.program_id(1)
    last = pl.num_programs(1) - 1
    tm = cv_ref.shape[0]
    cv = cv_ref[...]
    row = lax.broadcasted_iota(jnp.int32, (tm, 1), 0)
    before = jnp.where(i > 0, cvp_ref[SUBLANES - 1:SUBLANES, :], 0.0)
    after = jnp.where(i < last, cvn_ref[0:1, :], 0.0)
    down = jnp.where(row == 0, before, pltpu.roll(cv, 1, 0))
    up = jnp.where(row == tm - 1, after, pltpu.roll(cv, tm - 1, 0))
    w = convw_ref[...]
    y_conv = cb_ref[...] * (w[0:1, :] * down + w[1:2, :] * cv + w[2:3, :] * up)
    y = d_ref[...] * u_ref[...] + yf_ref[...] + yr_ref[...]
    gl = jax.nn.gelu(y)
    z = jnp.dot(gl.astype(BF16), wglu_ref[...], preferred_element_type=F32) + bglu_ref[...]
    y_ssm = gl * jax.nn.sigmoid(z)
    mix = (jnp.dot(y_conv.astype(BF16), wout_ref[0:D_CONV, :], preferred_element_type=F32)
           + jnp.dot(y_ssm.astype(BF16), wout_ref[D_CONV:D_CONV + D_SSM, :], preferred_element_type=F32)
           + jnp.dot(att_ref[...], wout_ref[D_CONV + D_SSM:, :], preferred_element_type=F32))
    mod = mod_ref[...]
    h_mid = h_ref[...] + mod[:, 2 * D_MODEL:3 * D_MODEL] * mix
    hmid_ref[...] = h_mid
    t = _rms_modulate(h_mid, g2_ref[...], mod[:, 3 * D_MODEL:4 * D_MODEL], mod[:, 4 * D_MODEL:5 * D_MODEL])
    t_ref[...] = t.astype(BF16)
    t_hi = t.astype(BF16)
    t_lo = (t - t_hi.astype(F32)).astype(BF16)
    a = jnp.dot(t_hi, rw_ref[...], preferred_element_type=F32)
    logits = a + pltpu.roll(a, LANES - N_EXPERTS, 1) + jnp.dot(t_lo, rw_ref[...], preferred_element_type=F32)
    scores = jax.nn.sigmoid(logits)
    lane = lax.broadcasted_iota(jnp.int32, (1, LANES), 1)
    eidx = lane.astype(F32)
    biased = jnp.where(lane < N_EXPERTS, scores + rb_ref[...], -jnp.inf)
    sel = jnp.zeros(scores.shape, F32)
    for _ in range(TOP_K):
        best = jnp.max(biased, axis=-1, keepdims=True)
        first = jnp.min(jnp.where(biased == best, eidx, float(LANES)), axis=-1, keepdims=True)
        pick = eidx == first
        sel = jnp.where(pick, scores, sel)
        biased = jnp.where(pick, -jnp.inf, biased)
    gates_ref[...] = sel / jnp.sum(sel, axis=-1, keepdims=True) * ROUTED_SCALE


def _mixer_output(h, mod_l, mod_row, cv, cb, u, yf, yr, att,
                  conv_w, ssm_d, w_glu, b_glu, w_out, norm_g, router_w, router_b, tm):
    bsz, t, d = h.shape
    halo = tm // SUBLANES
    n_halo = t // SUBLANES
    row_map = lambda b, i: (b, i, 0)
    u_spec = pl.BlockSpec((None, tm, D_SSM), row_map)
    in_specs = [pl.BlockSpec((None, tm, d), row_map),
                pl.BlockSpec((None, 1, 6 * d), lambda b, i: (mod_row(b), 0, 0)),
                pl.BlockSpec((None, tm, D_CONV), row_map),
                pl.BlockSpec((None, SUBLANES, D_CONV), lambda b, i: (b, jnp.maximum(i * halo - 1, 0), 0)),
                pl.BlockSpec((None, SUBLANES, D_CONV), lambda b, i: (b, jnp.minimum((i + 1) * halo, n_halo - 1), 0)),
                pl.BlockSpec((None, tm, D_CONV), row_map),
                u_spec, u_spec, u_spec,
                pl.BlockSpec((None, tm, D_ATTN), row_map),
                _const_spec((3, D_CONV)),
                _const_spec((1, D_SSM)),
                _const_spec((D_SSM, D_SSM)),
                _const_spec((1, D_SSM)),
                _const_spec((d, d)),
                _const_spec((1, d)),
                _const_spec((d, LANES)),
                _const_spec((1, LANES))]
    out_shape = (jax.ShapeDtypeStruct((bsz, t, d), F32),
                 jax.ShapeDtypeStruct((bsz, t, d), BF16),
                 jax.ShapeDtypeStruct((bsz, t, LANES), F32))
    out_specs = (pl.BlockSpec((None, tm, d), row_map),
                 pl.BlockSpec((None, tm, d), row_map),
                 pl.BlockSpec((None, tm, LANES), row_map))
    return pl.pallas_call(
        _mixout_kernel, out_shape=out_shape, grid=(bsz, t // tm),
        in_specs=in_specs, out_specs=out_specs,
        compiler_params=_cparams(2),
        name="mixer_output",
    )(h, mod_l, cv, cv, cv, cb, u, yf, yr, att,
      conv_w, ssm_d, w_glu, b_glu, w_out, norm_g, router_w, router_b)


EXPERTS_PER_CHUNK = 4


def _moe_kernel(t_ref, gates_ref, hmid_ref, mod_ref, wg_ref, wu_ref, wd_ref,
                wsg_ref, wsu_ref, wsd_ref, o_ref, acc_ref):
    t = t_ref[...]
    gates = gates_ref[...]
    hs = _silu(jnp.dot(t, wsg_ref[...], preferred_element_type=F32)) * jnp.dot(
        t, wsu_ref[...], preferred_element_type=F32)
    acc_ref[...] = jnp.dot(hs.astype(BF16), wsd_ref[...], preferred_element_type=F32)
    width = EXPERTS_PER_CHUNK * D_EXPERT
    for c in range(N_EXPERTS // EXPERTS_PER_CHUNK):
        cols = slice(c * width, (c + 1) * width)
        hid = _silu(jnp.dot(t, wg_ref[:, cols], preferred_element_type=F32)) * jnp.dot(
            t, wu_ref[:, cols], preferred_element_type=F32)
        gated = []
        for e in range(EXPERTS_PER_CHUNK):
            k = c * EXPERTS_PER_CHUNK + e
            gated.append((hid[:, e * D_EXPERT:(e + 1) * D_EXPERT] * gates[:, k:k + 1]).astype(BF16))
        acc_ref[...] += jnp.dot(jnp.concatenate(gated, axis=1), wd_ref[cols, :],
                                preferred_element_type=F32)
    mod = mod_ref[...]
    o_ref[...] = hmid_ref[...] + mod[:, 5 * D_MODEL:6 * D_MODEL] * acc_ref[...]


def _moe(t, gates, hmid, mod_l, mod_row, wg, wu, wd, wsg, wsu, wsd, tm):
    bsz, n, d = hmid.shape
    row_map = lambda b, i: (b, i, 0)
    return pl.pallas_call(
        _moe_kernel,
        out_shape=jax.ShapeDtypeStruct((bsz, n, d), F32),
        grid=(bsz, n // tm),
        in_specs=[pl.BlockSpec((None, tm, d), row_map),
                  pl.BlockSpec((None, tm, LANES), row_map),
                  pl.BlockSpec((None, tm, d), row_map),
                  pl.BlockSpec((None, 1, 6 * d), lambda b, i: (mod_row(b), 0, 0)),
                  _const_spec(wg.shape), _const_spec(wu.shape), _const_spec(wd.shape),
                  _const_spec(wsg.shape), _const_spec(wsu.shape), _const_spec(wsd.shape)],
        out_specs=pl.BlockSpec((None, tm, d), row_map),
        scratch_shapes=[pltpu.VMEM((tm, d), F32)],
        compiler_params=_cparams(2),
        name="moe",
    )(t, gates, hmid, mod_l, wg, wu, wd, wsg, wsu, wsd)


def _rope_tables(n_tokens):
    rows = n_tokens // GRID_W
    row = jnp.repeat(jnp.arange(rows, dtype=F32), GRID_W)
    col = jnp.tile(jnp.arange(GRID_W, dtype=F32), rows)
    inv_freq = ROPE_BASE ** (-jnp.arange(AXIS_PAIRS, dtype=F32) / AXIS_PAIRS)
    ar, ac = row[:, None] * inv_freq, col[:, None] * inv_freq
    cos = jnp.concatenate([jnp.cos(ar), jnp.cos(ar), jnp.cos(ac), jnp.cos(ac)], axis=1)
    sin = jnp.concatenate([-jnp.sin(ar), jnp.sin(ar), -jnp.sin(ac), jnp.sin(ac)], axis=1)
    return jnp.tile(cos, (1, LANES // HEAD_DIM)), jnp.tile(sin, (1, LANES // HEAD_DIM))


def _ssm_params(lam_re, lam_im, log_dt, b_re, b_im, c_re, c_im):
    lr, li = lam_re.astype(F32), lam_im.astype(F32)
    dt = jnp.exp(log_dt.astype(F32))[..., None]
    mag = jnp.exp(lr * dt)
    ar, ai = mag * jnp.cos(li * dt), mag * jnp.sin(li * dt)
    den = lr * lr + li * li
    qr = ((ar - 1.0) * lr + ai * li) / den
    qi = (ai * lr - (ar - 1.0) * li) / den
    br, bi = b_re.astype(F32), b_im.astype(F32)
    bbar_re = qr[..., None] * br - qi[..., None] * bi
    bbar_im = qr[..., None] * bi + qi[..., None] * br
    eye = jnp.eye(SSM_GROUPS, dtype=F32)
    drive = lambda m: jnp.einsum('gk,dgph->dghkp', eye, m).reshape(2, D_SSM, N_STATE)
    read = lambda m: jnp.einsum('gk,dghp->dgpkh', eye, m.astype(F32)).reshape(2, N_STATE, D_SSM)
    bmat = jnp.concatenate([drive(bbar_re), drive(bbar_im)], axis=2)
    cmat = jnp.concatenate([read(c_re), -read(c_im)], axis=1)
    lam_flat = lambda m: m.reshape(2, N_STATE)
    ar, ai = lam_flat(ar), lam_flat(ai)
    lamv = jnp.stack([ar[0], ai[0], ar[1], ai[1]])
    return bmat.astype(BF16), lamv, cmat.astype(BF16)


def _dup_heads(w):
    d = w.shape[0]
    w = w.reshape(d, N_KV_HEADS, 1, HEAD_DIM)
    return jnp.broadcast_to(w, (d, N_KV_HEADS, 2, HEAD_DIM)).reshape(d, 2 * N_KV_HEADS * HEAD_DIM)


def kernel(x, c, ctx, c_ctx, w_mod, b_mod, norm1_g, norm2_g, w_in, conv_w, ssm_lam_re, ssm_lam_im, ssm_log_dt, ssm_b_re, ssm_b_im, ssm_c_re, ssm_c_im, ssm_d, w_glu, b_glu, q_norm_g, k_norm_g, attn_sink, w_out, router_w, router_bias, w_exp_gate, w_exp_up, w_exp_down, w_sh_gate, w_sh_up, w_sh_down):
    bsz, n_lat, d = x.shape
    n_ctx = ctx.shape[1]
    depth = w_mod.shape[0]
    assert bsz == SUBLANES and d == D_MODEL

    mod_rows = 2 * SUBLANES
    cvec = jnp.zeros((mod_rows, d), F32).at[:bsz].set(c).at[bsz].set(c_ctx)
    mod = _modulation(cvec, w_mod, b_mod).reshape(depth, mod_rows, 1, 6 * d)
    lat_row = lambda b: b
    ctx_row = lambda b: bsz
    rope_tabs = _rope_tables(n_lat)

    tm_lat, tm_ctx, scan_steps = 512, 256, 64
    kv0 = 3 * D_CONV + D_SSM + D_ATTN
    h, hc = x, ctx
    for l in range(depth):
        ctx_out = l < depth - 1
        mod_l = mod[l]
        w_in2 = jnp.concatenate([w_in[l][:, :kv0], _dup_heads(w_in[l][:, kv0:kv0 + 2 * HEAD_DIM]),
                                 _dup_heads(w_in[l][:, kv0 + 2 * HEAD_DIM:])], axis=1).astype(BF16)
        qg = jnp.tile(q_norm_g[l].astype(F32), LANES // HEAD_DIM)[None, :]
        kg = jnp.tile(k_norm_g[l].astype(F32), LANES // HEAD_DIM)[None, :]
        n1 = norm1_g[l][None, :]
        cv, cb, u_lat, q, k2, v2 = _in_projection(h, mod_l, lat_row, n1, w_in2, qg, kg, rope_tabs, tm_lat)
        cvc, cbc, u_ctx, qc, kc2, vc2 = _in_projection(hc, mod_l, ctx_row, n1, w_in2, qg, kg, None, tm_ctx)

        bmat, lamv, cmat = _ssm_params(ssm_lam_re[l], ssm_lam_im[l], ssm_log_dt[l], ssm_b_re[l],
                                       ssm_b_im[l], ssm_c_re[l], ssm_c_im[l])
        yf, yr, yfc, yrc = _ssm_scan(u_lat, u_ctx, bmat, lamv, cmat, scan_steps)

        sink = attn_sink[l].astype(F32)
        att = _attention(sink, q, k2, v2, kc2, vc2, window=True)

        rw = router_w[l].astype(F32)
        rw_hi = rw.astype(BF16)
        rw_lo = (rw - rw_hi.astype(F32)).astype(BF16)
        rw2 = jnp.concatenate([rw_hi, rw_lo, jnp.zeros((d, LANES - 2 * N_EXPERTS), BF16)], axis=1)
        rb = jnp.zeros((1, LANES), F32).at[0, :N_EXPERTS].set(router_bias[l].astype(F32))
        post = dict(conv_w=conv_w[l], ssm_d=ssm_d[l][None, :], w_glu=w_glu[l].astype(BF16),
                    b_glu=b_glu[l][None, :], w_out=w_out[l].astype(BF16), norm_g=norm2_g[l][None, :],
                    router_w=rw2, router_b=rb)
        experts = (w_exp_gate[l].astype(BF16).reshape(d, -1), w_exp_up[l].astype(BF16).reshape(d, -1),
                   w_exp_down[l].astype(BF16).reshape(-1, d), w_sh_gate[l].astype(BF16),
                   w_sh_up[l].astype(BF16), w_sh_down[l].astype(BF16))
        hmid, t, gates = _mixer_output(h, mod_l, lat_row, cv, cb, u_lat, yf, yr, att, tm=tm_lat, **post)
        h_new = _moe(t, gates, hmid, mod_l, lat_row, *experts, tm=tm_lat)
        if ctx_out:
            attc = _attention(sink, qc, None, None, kc2, vc2, window=False)
            hmid_c, tc, gates_c = _mixer_output(hc, mod_l, ctx_row, cvc, cbc, u_ctx, yfc, yrc, attc,
                                                tm=tm_ctx, **post)
            flat = lambda a: a.reshape(1, bsz * n_ctx, a.shape[-1])
            hc = _moe(flat(tc), flat(gates_c), flat(hmid_c), mod_l, ctx_row, *experts,
                      tm=tm_lat).reshape(hc.shape)
        h = h_new
    return h
```

```python
import functools
import math

import jax
import jax.numpy as jnp
from jax import lax
from jax.experimental import pallas as pl
from jax.experimental.pallas import tpu as pltpu

F32 = jnp.float32
BF16 = jnp.bfloat16
HIGHEST = lax.Precision.HIGHEST

D_MODEL = 1024
D_CONV = 256
D_SSM = 256
SSM_GROUP = 16
SSM_GROUPS = 16
SSM_STATE = 64
N_STATE = SSM_GROUPS * SSM_STATE
HEAD_DIM = 64
D_ATTN = 512
N_Q_HEADS = 8
N_KV_HEADS = 2
AXIS_PAIRS = 16
ROPE_BASE = 10000.0
GRID_W = 64
BLOCK = 128
N_EXPERTS = 32
TOP_K = 4
D_EXPERT = 128
ROUTED_SCALE = 2.5
NORM_EPS = 1e-6
NEG_INF = -1e30
D_PROJ = 3 * D_CONV + D_SSM + D_ATTN + 4 * N_KV_HEADS * HEAD_DIM
LANES = 128
SUBLANES = 8
MXU_TILE = 256
VMEM_LIMIT = 56 * 1024 * 1024


def _cparams(n_axes):
    return pltpu.CompilerParams(dimension_semantics=("arbitrary",) * n_axes,
                                vmem_limit_bytes=VMEM_LIMIT)


def _const_spec(shape):
    nd = len(shape)
    return pl.BlockSpec(shape, lambda *_: (0,) * nd, pipeline_mode=pl.Buffered(1))


def _silu(x):
    return x * jax.nn.sigmoid(x)


def _rms_modulate(x, gain, shift, scale):
    ms = jnp.mean(x * x, axis=-1, keepdims=True)
    return (x * lax.rsqrt(ms + NORM_EPS) * gain) * (1.0 + scale) + shift


def _mod_kernel(c_ref, w_ref, b_ref, o_ref):
    cv = c_ref[...]
    o_ref[...] = jnp.dot(_silu(cv), w_ref[...], precision=HIGHEST,
                         preferred_element_type=F32) + b_ref[...]


def _modulation(cvec, w_mod, b_mod):
    depth, d, n = w_mod.shape
    rows = cvec.shape[0]
    tn = 1536
    return pl.pallas_call(
        _mod_kernel,
        out_shape=jax.ShapeDtypeStruct((depth, rows, n), F32),
        grid=(depth, n // tn),
        in_specs=[pl.BlockSpec((rows, d), lambda l, j: (0, 0)),
                  pl.BlockSpec((None, d, tn), lambda l, j: (l, 0, j)),
                  pl.BlockSpec((None, 1, tn), lambda l, j: (l, 0, j))],
        out_specs=pl.BlockSpec((None, rows, tn), lambda l, j: (l, 0, j)),
        compiler_params=_cparams(2),
        name="modulation",
    )(cvec, w_mod, b_mod.reshape(depth, 1, n))


HEAD_PAIR = 4 * HEAD_DIM
ROW_PARTS = 1


def _inproj_kernel(*refs, rope):
    if rope:
        (x_ref, mod_ref, g_ref, w_ref, qg_ref, kg_ref, ones_ref, perm_ref, cos_ref, sin_ref,
         cv_ref, cb_ref, u_ref, q_ref, k_ref, v_ref) = refs
    else:
        (x_ref, mod_ref, g_ref, w_ref, qg_ref, kg_ref, ones_ref,
         cv_ref, cb_ref, u_ref, q_ref, k_ref, v_ref) = refs
    tm = x_ref.shape[0]
    mod = mod_ref[...]
    q0 = 3 * D_CONV + D_SSM
    k0 = q0 + D_ATTN
    v0 = k0 + HEAD_PAIR
    parts = [slice(r * tm // ROW_PARTS, (r + 1) * tm // ROW_PARTS) for r in range(ROW_PARTS)]
    mbs = [_rms_modulate(x_ref[rows, :], g_ref[...], mod[:, 0:D_MODEL], mod[:, D_MODEL:2 * D_MODEL]).astype(BF16)
           for rows in parts]

    def proj(mb, c0, width):
        return jnp.dot(mb, w_ref[:, c0:c0 + width], preferred_element_type=F32)

    for rows, mb in zip(parts, mbs):
        cv_ref[rows, :] = proj(mb, 2 * D_CONV, D_CONV) * proj(mb, 0, D_CONV)
        cb_ref[rows, :] = proj(mb, D_CONV, D_CONV)
        u_ref[rows, :] = proj(mb, 3 * D_CONV, D_SSM)
        v_ref[rows, :] = proj(mb, v0, HEAD_PAIR).astype(BF16)

    heads = [(q0 + jj * HEAD_PAIR, qg_ref, q_ref, jj * HEAD_PAIR, HEAD_DIM ** -0.5)
             for jj in range(D_ATTN // HEAD_PAIR)] + [(k0, kg_ref, k_ref, 0, 1.0)]
    blocks = [(rows, mb) + hd for rows, mb in zip(parts, mbs) for hd in heads]
    xs = [proj(mb, c0, HEAD_PAIR) for _, mb, c0, *_ in blocks]
    ssqs = [jnp.dot((x * x).astype(BF16), ones_ref[...], preferred_element_type=F32) for x in xs]
    xns = [x * lax.rsqrt(ssq * (1.0 / HEAD_DIM) + NORM_EPS) * blk[3][...]
           for x, ssq, blk in zip(xs, ssqs, blocks)]
    if rope:
        partners = [jnp.dot(xn.astype(BF16), perm_ref[...], preferred_element_type=F32) for xn in xns]
        xns = [xn * cos_ref[blk[0], :] + pt * sin_ref[blk[0], :] for xn, pt, blk in zip(xns, partners, blocks)]
    for xn, (rows, _, _, _, out_ref, o0, scale) in zip(xns, blocks):
        if scale != 1.0:
            xn = xn * scale
        out_ref[rows, o0:o0 + HEAD_PAIR] = xn.astype(BF16)


def _in_projection(h, mod_l, mod_row, norm_g, w_in2, qg, kg, rope_tabs, tm):
    bsz, t, d = h.shape
    rope = rope_tabs is not None
    row_map = lambda b, i: (b, i, 0)
    in_specs = [pl.BlockSpec((None, tm, d), row_map),
                pl.BlockSpec((None, 1, 6 * d), lambda b, i: (mod_row(b), 0, 0)),
                _const_spec((1, d)),
                _const_spec((d, D_PROJ)),
                _const_spec((1, HEAD_PAIR)),
                _const_spec((1, HEAD_PAIR)),
                _const_spec((HEAD_PAIR, HEAD_PAIR))]
    lane = jnp.arange(HEAD_PAIR)
    head_ones = (lane[:, None] // HEAD_DIM == lane[None, :] // HEAD_DIM).astype(BF16)
    args = [h, mod_l, norm_g, w_in2, qg, kg, head_ones]
    if rope:
        swap = (lane[:, None] == (lane[None, :] ^ AXIS_PAIRS)).astype(BF16)
        in_specs += [_const_spec((HEAD_PAIR, HEAD_PAIR))]
        in_specs += [pl.BlockSpec((tm, HEAD_PAIR), lambda b, i: (i, 0))] * 2
        args += [swap] + list(rope_tabs)
    out_shape = (jax.ShapeDtypeStruct((bsz, t, D_CONV), F32),
                 jax.ShapeDtypeStruct((bsz, t, D_CONV), F32),
                 jax.ShapeDtypeStruct((bsz, t, D_SSM), F32),
                 jax.ShapeDtypeStruct((bsz, t, D_ATTN), BF16),
                 jax.ShapeDtypeStruct((bsz, t, 2 * LANES), BF16),
                 jax.ShapeDtypeStruct((bsz, t, 2 * LANES), BF16))
    out_specs = (pl.BlockSpec((None, tm, D_CONV), row_map),
                 pl.BlockSpec((None, tm, D_CONV), row_map),
                 pl.BlockSpec((None, tm, D_SSM), row_map),
                 pl.BlockSpec((None, tm, D_ATTN), row_map),
                 pl.BlockSpec((None, tm, 2 * LANES), row_map),
                 pl.BlockSpec((None, tm, 2 * LANES), row_map))
    return pl.pallas_call(
        functools.partial(_inproj_kernel, rope=rope),
        out_shape=out_shape, grid=(bsz, t // tm), in_specs=in_specs, out_specs=out_specs,
        compiler_params=_cparams(2),
        name="in_projection_lat" if rope else "in_projection_ctx",
    )(*args)


def _scan_kernel(ulf_ref, ucf_ref, ulr_ref, ucr_ref, bmat_ref, lam_ref, cmat_ref,
                 yfl_ref, yrl_ref, yfc_ref, yrc_ref,
                 uf_ref, ur_ref, yf_ref, yr_ref, sf0_ref, sf1_ref, sr0_ref, sr1_ref, hf_ref, hr_ref,
                 *, steps, ctx_chunks, n_chunks):
    j = pl.program_id(0)

    @pl.when(j == 0)
    def _():
        for ref in (sf0_ref, sf1_ref, sr0_ref, sr1_ref):
            ref[...] = jnp.zeros_like(ref)
        hf_ref[...] = jnp.zeros_like(hf_ref)
        hr_ref[...] = jnp.zeros_like(hr_ref)

    drive_ctx = jnp.minimum(j, n_chunks - 1) < ctx_chunks

    def lam_rows(r):
        return jnp.broadcast_to(lam_ref[r:r + 1, :], (SUBLANES, N_STATE))

    def phases(sf_cur, sr_cur, sf_prev, sr_prev):
        for t in range(steps):
            rows = slice(t * SUBLANES, (t + 1) * SUBLANES)
            uf_ref[rows, :] = jnp.where(drive_ctx, ucf_ref[:, t, :], ulf_ref[:, t, :])
            ur_ref[rows, :] = jnp.where(drive_ctx, ucr_ref[:, t, :], ulr_ref[:, t, :])
        ub_f, ub_r = uf_ref[...].astype(BF16), ur_ref[...].astype(BF16)
        n_blocks = 2 * N_STATE // MXU_TILE
        every = steps // n_blocks

        def matmul_block(k):
            cols = slice(k * MXU_TILE, (k + 1) * MXU_TILE)
            for d, (s_cur, y_ref, ub) in enumerate(((sf_cur, yf_ref, ub_f), (sr_cur, yr_ref, ub_r))):
                part = jnp.dot(s_cur[:, cols].astype(BF16), cmat_ref[d, cols, :], preferred_element_type=F32)
                if k == 0:
                    y_ref[...] = part
                else:
                    y_ref[...] += part
                s_cur[:, cols] = jnp.dot(ub, bmat_ref[d, :, cols], preferred_element_type=F32)

        lf_re, lf_im, lr_re, lr_im = lam_rows(0), lam_rows(1), lam_rows(2), lam_rows(3)

        def step(s_ref, t, l_re, l_im, h_re, h_im):
            rows = slice(t * SUBLANES, (t + 1) * SUBLANES)
            n_re = l_re * h_re - l_im * h_im + s_ref[rows, 0:N_STATE]
            n_im = l_re * h_im + l_im * h_re + s_ref[rows, N_STATE:2 * N_STATE]
            s_ref[rows, 0:N_STATE] = n_re
            s_ref[rows, N_STATE:2 * N_STATE] = n_im
            return n_re, n_im

        f_re, f_im = hf_ref[:, 0:N_STATE], hf_ref[:, N_STATE:2 * N_STATE]
        r_re, r_im = hr_ref[:, 0:N_STATE], hr_ref[:, N_STATE:2 * N_STATE]
        for t in range(steps):
            if t % every == 0:
                matmul_block(t // every)
            f_re, f_im = step(sf_prev, t, lf_re, lf_im, f_re, f_im)
        for t in range(steps):
            r_re, r_im = step(sr_prev, steps - 1 - t, lr_re, lr_im, r_re, r_im)
        hf_ref[:, 0:N_STATE] = f_re
        hf_ref[:, N_STATE:2 * N_STATE] = f_im
        hr_ref[:, 0:N_STATE] = r_re
        hr_ref[:, N_STATE:2 * N_STATE] = r_im

    @pl.when(lax.rem(j, 2) == 0)
    def _():
        phases(sf0_ref, sr0_ref, sf1_ref, sr1_ref)

    @pl.when(lax.rem(j, 2) == 1)
    def _():
        phases(sf1_ref, sr1_ref, sf0_ref, sr0_ref)

    def write_out(yf_out, yr_out):
        for t in range(steps):
            rows = slice(t * SUBLANES, (t + 1) * SUBLANES)
            yf_out[:, t, :] = yf_ref[rows, :]
            yr_out[:, t, :] = yr_ref[rows, :]

    @pl.when(jnp.logical_and(j >= 2, j - 2 < ctx_chunks))
    def _():
        write_out(yfc_ref, yrc_ref)

    @pl.when(j - 2 >= ctx_chunks)
    def _():
        write_out(yfl_ref, yrl_ref)


def _ssm_scan(u_lat, u_ctx, bmat, lamv, cmat, steps):
    bsz = u_lat.shape[0]
    rows = steps * bsz
    nl, nc = u_lat.shape[1] // steps, u_ctx.shape[1] // steps
    n = nl + nc
    blk = lambda f: pl.BlockSpec((bsz, steps, D_SSM), lambda j: (0, f(j), 0))
    lat_f = lambda p: jnp.maximum(p - nc, 0)
    ctx_f = lambda p: jnp.minimum(p, nc - 1)
    lat_r = lambda p: jnp.minimum(n - 1 - p, nl - 1)
    ctx_r = lambda p: jnp.maximum(nc - 1 - p, 0)
    drive = lambda f: blk(lambda j: f(jnp.minimum(j, n - 1)))
    read = lambda f: blk(lambda j: f(jnp.maximum(j - 2, 0)))
    state = pltpu.VMEM((rows, 2 * N_STATE), F32)
    chunk = pltpu.VMEM((rows, D_SSM), F32)
    carry = pltpu.VMEM((SUBLANES, 2 * N_STATE), F32)
    return pl.pallas_call(
        functools.partial(_scan_kernel, steps=steps, ctx_chunks=nc, n_chunks=n),
        out_shape=(jax.ShapeDtypeStruct(u_lat.shape, F32),) * 2 + (jax.ShapeDtypeStruct(u_ctx.shape, F32),) * 2,
        grid=(n + 2,),
        in_specs=[drive(lat_f), drive(ctx_f), drive(lat_r), drive(ctx_r),
                  _const_spec((2, D_SSM, 2 * N_STATE)),
                  _const_spec((4, N_STATE)),
                  _const_spec((2, 2 * N_STATE, D_SSM))],
        out_specs=(read(lat_f), read(lat_r), read(ctx_f), read(ctx_r)),
        scratch_shapes=[chunk, chunk, chunk, chunk, state, state, state, state, carry, carry],
        compiler_params=_cparams(1),
        name="s5_scan",
    )(u_lat, u_ctx, u_lat, u_ctx, bmat, lamv, cmat)


def _attn_kernel(sink_ref, q_ref, *refs, window):
    if window:
        kp_ref, ko_ref, kn_ref, vp_ref, vo_ref, vn_ref, kc_ref, vc_ref, o_ref = refs
    else:
        kc_ref, vc_ref, o_ref = refs
    n = pl.program_id(1)
    last = pl.num_programs(1) - 1
    rows = 2 * BLOCK
    lane = lax.broadcasted_iota(jnp.int32, (1, LANES), 1)
    lo = lane < HEAD_DIM
    qi = lax.broadcasted_iota(jnp.int32, (rows, BLOCK), 0) & (BLOCK - 1)
    kj = lax.broadcasted_iota(jnp.int32, (rows, BLOCK), 1)
    upper_head = lax.broadcasted_iota(jnp.int32, (rows, 1), 0) >= BLOCK
    zero = jnp.zeros((), BF16)
    nt = (((1,), (1,)), ((), ()))
    heads_per_group = N_Q_HEADS // N_KV_HEADS
    units = range(D_ATTN // LANES)
    cols = [slice((2 * j // heads_per_group) * LANES, (2 * j // heads_per_group + 1) * LANES) for j in units]
    qs = []
    for j in units:
        qp = q_ref[:, j * LANES:(j + 1) * LANES]
        qs.append(jnp.concatenate([jnp.where(lo, qp, zero), jnp.where(lo, zero, qp)], axis=0))
    s_all = []
    for j in units:
        scores = []
        if window:
            s = lax.dot_general(qs[j], kp_ref[:, cols[j]], nt, preferred_element_type=F32)
            scores.append(jnp.where((kj >= qi) & (n > 0), s, NEG_INF))
            scores.append(lax.dot_general(qs[j], ko_ref[:, cols[j]], nt, preferred_element_type=F32))
            s = lax.dot_general(qs[j], kn_ref[:, cols[j]], nt, preferred_element_type=F32)
            scores.append(jnp.where((kj <= qi) & (n < last), s, NEG_INF))
        scores.append(lax.dot_general(qs[j], kc_ref[:, cols[j]], nt, preferred_element_type=F32))
        s_all.append(jnp.concatenate(scores, axis=1))
    sinks = [jnp.where(upper_head, sink_ref[2 * j + 1], sink_ref[2 * j]) for j in units]
    ms = [jnp.maximum(sinks[j], jnp.max(s_all[j], axis=-1, keepdims=True)) for j in units]
    es = [jnp.exp(s_all[j] - ms[j]) for j in units]
    denoms = [jnp.exp(sinks[j] - ms[j]) + jnp.sum(es[j], axis=-1, keepdims=True) for j in units]
    accs = []
    for j in units:
        values = [vp_ref[:, cols[j]], vo_ref[:, cols[j]], vn_ref[:, cols[j]]] if window else []
        values.append(vc_ref[:, cols[j]])
        accs.append(jnp.dot(es[j].astype(BF16), jnp.concatenate(values, axis=0), preferred_element_type=F32))
    for j in units:
        out = accs[j] / denoms[j]
        o_ref[:, j * LANES:(j + 1) * LANES] = jnp.where(lo, out[0:BLOCK], out[BLOCK:2 * BLOCK]).astype(BF16)


def _attention(sink, q, k2, v2, kc2, vc2, window):
    bsz, t, _ = q.shape
    n_ctx = kc2.shape[1]
    nb = t // BLOCK
    kw = 2 * LANES
    blk = lambda f: pl.BlockSpec((None, BLOCK, kw), f)
    in_specs = [pl.BlockSpec(memory_space=pltpu.SMEM),
                pl.BlockSpec((None, BLOCK, D_ATTN), lambda b, n: (b, n, 0))]
    args = [sink, q]
    if window:
        prev = lambda b, n: (b, jnp.maximum(n - 1, 0), 0)
        own = lambda b, n: (b, n, 0)
        nxt = lambda b, n: (b, jnp.minimum(n + 1, nb - 1), 0)
        in_specs += [blk(prev), blk(own), blk(nxt)] * 2
        args += [k2, k2, k2, v2, v2, v2]
    ctx_spec = pl.BlockSpec((None, n_ctx, kw), lambda b, n: (b, 0, 0))
    in_specs += [ctx_spec, ctx_spec]
    args += [kc2, vc2]
    return pl.pallas_call(
        functools.partial(_attn_kernel, window=window),
        out_shape=jax.ShapeDtypeStruct((bsz, t, D_ATTN), BF16),
        grid=(bsz, nb), in_specs=in_specs,
        out_specs=pl.BlockSpec((None, BLOCK, D_ATTN), lambda b, n: (b, n, 0)),
        compiler_params=_cparams(2),
        name="window_attention" if window else "context_attention",
    )(*args)


def _mixout_kernel(h_ref, mod_ref, cv_ref, cvp_ref, cvn_ref, cb_ref, u_ref, yf_ref, yr_ref, att_ref,
                   convw_ref, d_ref, wglu_ref, bglu_ref, wout_ref, g2_ref, rw_ref, rb_ref,
                   hmid_ref, t_ref, gates_ref):
    i = pl.program_id(1)
    last = pl.num_programs(1) - 1
    tm = cv_ref.shape[0]
    cv = cv_ref[...]
    row = lax.broadcasted_iota(jnp.int32, (tm, 1), 0)
    before = jnp.where(i > 0, cvp_ref[SUBLANES - 1:SUBLANES, :], 0.0)
    after = jnp.where(i < last, cvn_ref[0:1, :], 0.0)
    down = jnp.where(row == 0, before, pltpu.roll(cv, 1, 0))
    up = jnp.where(row == tm - 1, after, pltpu.roll(cv, tm - 1, 0))
    w = convw_ref[...]
    y_conv = cb_ref[...] * (w[0:1, :] * down + w[1:2, :] * cv + w[2:3, :] * up)
    y = d_ref[...] * u_ref[...] + yf_ref[...] + yr_ref[...]
    gl = jax.nn.gelu(y)
    z = jnp.dot(gl.astype(BF16), wglu_ref[...], preferred_element_type=F32) + bglu_ref[...]
    y_ssm = gl * jax.nn.sigmoid(z)
    mix = (jnp.dot(y_conv.astype(BF16), wout_ref[0:D_CONV, :], preferred_element_type=F32)
           + jnp.dot(y_ssm.astype(BF16), wout_ref[D_CONV:D_CONV + D_SSM, :], preferred_element_type=F32)
           + jnp.dot(att_ref[...], wout_ref[D_CONV + D_SSM:, :], preferred_element_type=F32))
    mod = mod_ref[...]
    h_mid = h_ref[...] + mod[:, 2 * D_MODEL:3 * D_MODEL] * mix
    hmid_ref[...] = h_mid
    t = _rms_modulate(h_mid, g2_ref[...], mod[:, 3 * D_MODEL:4 * D_MODEL], mod[:, 4 * D_MODEL:5 * D_MODEL])
    t_ref[...] = t.astype(BF16)
    t_hi = t.astype(BF16)
    t_lo = (t - t_hi.astype(F32)).astype(BF16)
    nt = (((1,), (1,)), ((), ()))
    a = lax.dot_general(rw_ref[...], t_hi, nt, preferred_element_type=F32)
    b = lax.dot_general(rw_ref[...], t_lo, nt, preferred_element_type=F32)
    logits = a[0:N_EXPERTS] + a[N_EXPERTS:2 * N_EXPERTS] + b[0:N_EXPERTS]
    scores = jax.nn.sigmoid(logits)
    eidx = lax.broadcasted_iota(jnp.int32, (N_EXPERTS, 1), 0).astype(F32)
    biased = scores + rb_ref[...]
    sel = jnp.zeros(scores.shape, F32)
    for _ in range(TOP_K):
        best = jnp.max(biased, axis=0, keepdims=True)
        first = jnp.min(jnp.where(biased == best, eidx, float(N_EXPERTS)), axis=0, keepdims=True)
        pick = eidx == first
        sel = jnp.where(pick, scores, sel)
        biased = jnp.where(pick, -jnp.inf, biased)
    gates = sel / jnp.sum(sel, axis=0, keepdims=True) * ROUTED_SCALE
    padded = jnp.concatenate([gates, jnp.zeros((LANES - N_EXPERTS, gates.shape[1]), F32)], axis=0)
    gates_ref[...] = padded.T


def _mixer_output(h, mod_l, mod_row, cv, cb, u, yf, yr, att,
                  conv_w, ssm_d, w_glu, b_glu, w_out, norm_g, router_w, router_b, tm):
    bsz, t, d = h.shape
    halo = tm // SUBLANES
    n_halo = t // SUBLANES
    row_map = lambda b, i: (b, i, 0)
    u_spec = pl.BlockSpec((None, tm, D_SSM), row_map)
    in_specs = [pl.BlockSpec((None, tm, d), row_map),
                pl.BlockSpec((None, 1, 6 * d), lambda b, i: (mod_row(b), 0, 0)),
                pl.BlockSpec((None, tm, D_CONV), row_map),
                pl.BlockSpec((None, SUBLANES, D_CONV), lambda b, i: (b, jnp.maximum(i * halo - 1, 0), 0)),
                pl.BlockSpec((None, SUBLANES, D_CONV), lambda b, i: (b, jnp.minimum((i + 1) * halo, n_halo - 1), 0)),
                pl.BlockSpec((None, tm, D_CONV), row_map),
                u_spec, u_spec, u_spec,
                pl.BlockSpec((None, tm, D_ATTN), row_map),
                _const_spec((3, D_CONV)),
                _const_spec((1, D_SSM)),
                _const_spec((D_SSM, D_SSM)),
                _const_spec((1, D_SSM)),
                _const_spec((d, d)),
                _const_spec((1, d)),
                _const_spec((LANES, d)),
                _const_spec((N_EXPERTS, 1))]
    out_shape = (jax.ShapeDtypeStruct((bsz, t, d), F32),
                 jax.ShapeDtypeStruct((bsz, t, d), BF16),
                 jax.ShapeDtypeStruct((bsz, t, LANES), F32))
    out_specs = (pl.BlockSpec((None, tm, d), row_map),
                 pl.BlockSpec((None, tm, d), row_map),
                 pl.BlockSpec((None, tm, LANES), row_map))
    return pl.pallas_call(
        _mixout_kernel, out_shape=out_shape, grid=(bsz, t // tm),
        in_specs=in_specs, out_specs=out_specs,
        compiler_params=_cparams(2),
        name="mixer_output",
    )(h, mod_l, cv, cv, cv, cb, u, yf, yr, att,
      conv_w, ssm_d, w_glu, b_glu, w_out, norm_g, router_w, router_b)


EXPERTS_PER_CHUNK = 4


def _moe_kernel(t_ref, gates_ref, hmid_ref, mod_ref, wg_ref, wu_ref, wd_ref,
                wsg_ref, wsu_ref, wsd_ref, o_ref, acc_ref):
    t = t_ref[...]
    gates = gates_ref[...]
    hs = _silu(jnp.dot(t, wsg_ref[...], preferred_element_type=F32)) * jnp.dot(
        t, wsu_ref[...], preferred_element_type=F32)
    acc_ref[...] = jnp.dot(hs.astype(BF16), wsd_ref[...], preferred_element_type=F32)
    width = EXPERTS_PER_CHUNK * D_EXPERT
    for c in range(N_EXPERTS // EXPERTS_PER_CHUNK):
        cols = slice(c * width, (c + 1) * width)
        hid = _silu(jnp.dot(t, wg_ref[:, cols], preferred_element_type=F32)) * jnp.dot(
            t, wu_ref[:, cols], preferred_element_type=F32)
        gated = []
        for e in range(EXPERTS_PER_CHUNK):
            k = c * EXPERTS_PER_CHUNK + e
            gated.append((hid[:, e * D_EXPERT:(e + 1) * D_EXPERT] * gates[:, k:k + 1]).astype(BF16))
        acc_ref[...] += jnp.dot(jnp.concatenate(gated, axis=1), wd_ref[cols, :],
                                preferred_element_type=F32)
    mod = mod_ref[...]
    o_ref[...] = hmid_ref[...] + mod[:, 5 * D_MODEL:6 * D_MODEL] * acc_ref[...]


def _moe(t, gates, hmid, mod_l, mod_row, wg, wu, wd, wsg, wsu, wsd, tm):
    bsz, n, d = hmid.shape
    row_map = lambda b, i: (b, i, 0)
    return pl.pallas_call(
        _moe_kernel,
        out_shape=jax.ShapeDtypeStruct((bsz, n, d), F32),
        grid=(bsz, n // tm),
        in_specs=[pl.BlockSpec((None, tm, d), row_map),
                  pl.BlockSpec((None, tm, LANES), row_map),
                  pl.BlockSpec((None, tm, d), row_map),
                  pl.BlockSpec((None, 1, 6 * d), lambda b, i: (mod_row(b), 0, 0)),
                  _const_spec(wg.shape), _const_spec(wu.shape), _const_spec(wd.shape),
                  _const_spec(wsg.shape), _const_spec(wsu.shape), _const_spec(wsd.shape)],
        out_specs=pl.BlockSpec((None, tm, d), row_map),
        scratch_shapes=[pltpu.VMEM((tm, d), F32)],
        compiler_params=_cparams(2),
        name="moe",
    )(t, gates, hmid, mod_l, wg, wu, wd, wsg, wsu, wsd)


def _rope_tables(n_tokens):
    rows = n_tokens // GRID_W
    row = jnp.repeat(jnp.arange(rows, dtype=F32), GRID_W)
    col = jnp.tile(jnp.arange(GRID_W, dtype=F32), rows)
    inv_freq = ROPE_BASE ** (-jnp.arange(AXIS_PAIRS, dtype=F32) / AXIS_PAIRS)
    ar, ac = row[:, None] * inv_freq, col[:, None] * inv_freq
    cos = jnp.concatenate([jnp.cos(ar), jnp.cos(ar), jnp.cos(ac), jnp.cos(ac)], axis=1)
    sin = jnp.concatenate([-jnp.sin(ar), jnp.sin(ar), -jnp.sin(ac), jnp.sin(ac)], axis=1)
    return jnp.tile(cos, (1, HEAD_PAIR // HEAD_DIM)), jnp.tile(sin, (1, HEAD_PAIR // HEAD_DIM))


def _ssm_params(lam_re, lam_im, log_dt, b_re, b_im, c_re, c_im):
    lr, li = lam_re.astype(F32), lam_im.astype(F32)
    dt = jnp.exp(log_dt.astype(F32))[..., None]
    mag = jnp.exp(lr * dt)
    ar, ai = mag * jnp.cos(li * dt), mag * jnp.sin(li * dt)
    den = lr * lr + li * li
    qr = ((ar - 1.0) * lr + ai * li) / den
    qi = (ai * lr - (ar - 1.0) * li) / den
    br, bi = b_re.astype(F32), b_im.astype(F32)
    bbar_re = qr[..., None] * br - qi[..., None] * bi
    bbar_im = qr[..., None] * bi + qi[..., None] * br
    eye = jnp.eye(SSM_GROUPS, dtype=F32)
    drive = lambda m: jnp.einsum('gk,dgph->dghkp', eye, m).reshape(2, D_SSM, N_STATE)
    read = lambda m: jnp.einsum('gk,dghp->dgpkh', eye, m.astype(F32)).reshape(2, N_STATE, D_SSM)
    bmat = jnp.concatenate([drive(bbar_re), drive(bbar_im)], axis=2)
    cmat = jnp.concatenate([read(c_re), -read(c_im)], axis=1)
    lam_flat = lambda m: m.reshape(2, N_STATE)
    ar, ai = lam_flat(ar), lam_flat(ai)
    lamv = jnp.stack([ar[0], ai[0], ar[1], ai[1]])
    return bmat.astype(BF16), lamv, cmat.astype(BF16)


def _dup_heads(w):
    d = w.shape[0]
    w = w.reshape(d, N_KV_HEADS, 1, HEAD_DIM)
    return jnp.broadcast_to(w, (d, N_KV_HEADS, 2, HEAD_DIM)).reshape(d, 2 * N_KV_HEADS * HEAD_DIM)


def kernel(x, c, ctx, c_ctx, w_mod, b_mod, norm1_g, norm2_g, w_in, conv_w, ssm_lam_re, ssm_lam_im, ssm_log_dt, ssm_b_re, ssm_b_im, ssm_c_re, ssm_c_im, ssm_d, w_glu, b_glu, q_norm_g, k_norm_g, attn_sink, w_out, router_w, router_bias, w_exp_gate, w_exp_up, w_exp_down, w_sh_gate, w_sh_up, w_sh_down):
    bsz, n_lat, d = x.shape
    n_ctx = ctx.shape[1]
    depth = w_mod.shape[0]
    assert bsz == SUBLANES and d == D_MODEL

    mod_rows = 2 * SUBLANES
    cvec = jnp.zeros((mod_rows, d), F32).at[:bsz].set(c).at[bsz].set(c_ctx)
    mod = _modulation(cvec, w_mod, b_mod).reshape(depth, mod_rows, 1, 6 * d)
    lat_row = lambda b: b
    ctx_row = lambda b: bsz
    rope_tabs = _rope_tables(n_lat)

    tm_lat, tm_ctx, scan_steps = 512, 256, 64
    kv0 = 3 * D_CONV + D_SSM + D_ATTN
    h, hc = x, ctx
    for l in range(depth):
        ctx_out = l < depth - 1
        mod_l = mod[l]
        w_in2 = jnp.concatenate([w_in[l][:, :kv0], _dup_heads(w_in[l][:, kv0:kv0 + 2 * HEAD_DIM]),
                                 _dup_heads(w_in[l][:, kv0 + 2 * HEAD_DIM:])], axis=1).astype(BF16)
        qg = jnp.tile(q_norm_g[l].astype(F32), HEAD_PAIR // HEAD_DIM)[None, :]
        kg = jnp.tile(k_norm_g[l].astype(F32), HEAD_PAIR // HEAD_DIM)[None, :]
        n1 = norm1_g[l][None, :]
        cv, cb, u_lat, q, k2, v2 = _in_projection(h, mod_l, lat_row, n1, w_in2, qg, kg, rope_tabs, tm_lat)
        cvc, cbc, u_ctx, qc, kc2, vc2 = _in_projection(hc, mod_l, ctx_row, n1, w_in2, qg, kg, None, tm_ctx)

        bmat, lamv, cmat = _ssm_params(ssm_lam_re[l], ssm_lam_im[l], ssm_log_dt[l], ssm_b_re[l],
                                       ssm_b_im[l], ssm_c_re[l], ssm_c_im[l])
        yf, yr, yfc, yrc = _ssm_scan(u_lat, u_ctx, bmat, lamv, cmat, scan_steps)

        sink = attn_sink[l].astype(F32)
        att = _attention(sink, q, k2, v2, kc2, vc2, window=True)

        rw = router_w[l].astype(F32)
        rw_hi = rw.astype(BF16)
        rw_lo = (rw - rw_hi.astype(F32)).astype(BF16)
        rw2 = jnp.concatenate([rw_hi.T, rw_lo.T, jnp.zeros((LANES - 2 * N_EXPERTS, d), BF16)], axis=0)
        rb = router_bias[l].astype(F32)[:, None]
        post = dict(conv_w=conv_w[l], ssm_d=ssm_d[l][None, :], w_glu=w_glu[l].astype(BF16),
                    b_glu=b_glu[l][None, :], w_out=w_out[l].astype(BF16), norm_g=norm2_g[l][None, :],
                    router_w=rw2, router_b=rb)
        experts = (w_exp_gate[l].astype(BF16).reshape(d, -1), w_exp_up[l].astype(BF16).reshape(d, -1),
                   w_exp_down[l].astype(BF16).reshape(-1, d), w_sh_gate[l].astype(BF16),
                   w_sh_up[l].astype(BF16), w_sh_down[l].astype(BF16))
        hmid, t, gates = _mixer_output(h, mod_l, lat_row, cv, cb, u_lat, yf, yr, att, tm=tm_lat, **post)
        h_new = _moe(t, gates, hmid, mod_l, lat_row, *experts, tm=tm_lat)
        if ctx_out:
            attc = _attention(sink, qc, None, None, kc2, vc2, window=False)
            hmid_c, tc, gates_c = _mixer_output(hc, mod_l, ctx_row, cvc, cbc, u_ctx, yfc, yrc, attc,
                                                tm=tm_ctx, **post)
            flat = lambda a: a.reshape(1, bsz * n_ctx, a.shape[-1])
            hc = _moe(flat(tc), flat(gates_c), flat(hmid_c), mod_l, ctx_row, *experts,
                      tm=tm_lat).reshape(hc.shape)
        h = h_new
    return h
```

```python
import functools
import math

import jax
import jax.numpy as jnp
from jax import lax
from jax.experimental import pallas as pl
from jax.experimental.pallas import tpu as pltpu

F32 = jnp.float32
BF16 = jnp.bfloat16
HIGHEST = lax.Precision.HIGHEST

D_MODEL = 1024
D_CONV = 256
D_SSM = 256
SSM_GROUP = 16
SSM_GROUPS = 16
SSM_STATE = 64
N_STATE = SSM_GROUPS * SSM_STATE
HEAD_DIM = 64
D_ATTN = 512
N_Q_HEADS = 8
N_KV_HEADS = 2
AXIS_PAIRS = 16
ROPE_BASE = 10000.0
GRID_W = 64
BLOCK = 128
N_EXPERTS = 32
TOP_K = 4
D_EXPERT = 128
ROUTED_SCALE = 2.5
NORM_EPS = 1e-6
NEG_INF = -1e30
D_PROJ = 3 * D_CONV + D_SSM + D_ATTN + 4 * N_KV_HEADS * HEAD_DIM
LANES = 128
SUBLANES = 8
MXU_TILE = 256
VMEM_LIMIT = 56 * 1024 * 1024


def _cparams(n_axes):
    return pltpu.CompilerParams(dimension_semantics=("arbitrary",) * n_axes,
                                vmem_limit_bytes=VMEM_LIMIT)


def _const_spec(shape):
    nd = len(shape)
    return pl.BlockSpec(shape, lambda *_: (0,) * nd, pipeline_mode=pl.Buffered(1))


def _silu(x):
    return x * jax.nn.sigmoid(x)


def _rms_modulate(x, gain, shift, scale):
    ms = jnp.mean(x * x, axis=-1, keepdims=True)
    return (x * lax.rsqrt(ms + NORM_EPS) * gain) * (1.0 + scale) + shift


def _mod_kernel(c_ref, w_ref, b_ref, o_ref):
    cv = c_ref[...]
    o_ref[...] = jnp.dot(_silu(cv), w_ref[...], precision=HIGHEST,
                         preferred_element_type=F32) + b_ref[...]


def _modulation(cvec, w_mod, b_mod):
    depth, d, n = w_mod.shape
    rows = cvec.shape[0]
    tn = 1536
    return pl.pallas_call(
        _mod_kernel,
        out_shape=jax.ShapeDtypeStruct((depth, rows, n), F32),
        grid=(depth, n // tn),
        in_specs=[pl.BlockSpec((rows, d), lambda l, j: (0, 0)),
                  pl.BlockSpec((None, d, tn), lambda l, j: (l, 0, j)),
                  pl.BlockSpec((None, 1, tn), lambda l, j: (l, 0, j))],
        out_specs=pl.BlockSpec((None, rows, tn), lambda l, j: (l, 0, j)),
        compiler_params=_cparams(2),
        name="modulation",
    )(cvec, w_mod, b_mod.reshape(depth, 1, n))


HEAD_PAIR = 4 * HEAD_DIM
ROW_PARTS = 1


def _inproj_kernel(*refs, rope):
    if rope:
        (x_ref, mod_ref, g_ref, w_ref, qg_ref, kg_ref, ones_ref, perm_ref, cos_ref, sin_ref,
         cv_ref, cb_ref, u_ref, q_ref, k_ref, v_ref) = refs
    else:
        (x_ref, mod_ref, g_ref, w_ref, qg_ref, kg_ref, ones_ref, _,
         cv_ref, cb_ref, u_ref, q_ref, k_ref, v_ref) = refs
    tm = x_ref.shape[0]
    mod = mod_ref[...]
    q0 = 3 * D_CONV + D_SSM
    k0 = q0 + D_ATTN
    v0 = k0 + HEAD_PAIR
    parts = [slice(r * tm // ROW_PARTS, (r + 1) * tm // ROW_PARTS) for r in range(ROW_PARTS)]
    mbs = [_rms_modulate(x_ref[rows, :], g_ref[...], mod[:, 0:D_MODEL], mod[:, D_MODEL:2 * D_MODEL]).astype(BF16)
           for rows in parts]

    def proj(mb, c0, width):
        return jnp.dot(mb, w_ref[:, c0:c0 + width], preferred_element_type=F32)

    for rows, mb in zip(parts, mbs):
        cv_ref[rows, :] = proj(mb, 2 * D_CONV, D_CONV) * proj(mb, 0, D_CONV)
        cb_ref[rows, :] = proj(mb, D_CONV, D_CONV)
        u_ref[rows, :] = proj(mb, 3 * D_CONV, D_SSM)
        v_ref[rows, :] = proj(mb, v0, HEAD_PAIR).astype(BF16)

    heads = [(q0 + jj * HEAD_PAIR, qg_ref, q_ref, jj * HEAD_PAIR, HEAD_DIM ** -0.5)
             for jj in range(D_ATTN // HEAD_PAIR)] + [(k0, kg_ref, k_ref, 0, 1.0)]
    blocks = [(rows, mb) + hd for rows, mb in zip(parts, mbs) for hd in heads]
    xs = [proj(mb, c0, HEAD_PAIR) for _, mb, c0, *_ in blocks]
    ssqs = [jnp.dot((x * x).astype(BF16), ones_ref[...], preferred_element_type=F32) for x in xs]
    xns = [x * lax.rsqrt(ssq * (1.0 / HEAD_DIM) + NORM_EPS) * blk[3][...]
           for x, ssq, blk in zip(xs, ssqs, blocks)]
    if rope:
        partners = [jnp.dot(xn.astype(BF16), perm_ref[...], preferred_element_type=F32) for xn in xns]
        xns = [xn * cos_ref[blk[0], :] + pt * sin_ref[blk[0], :] for xn, pt, blk in zip(xns, partners, blocks)]
    for xn, (rows, _, _, _, out_ref, o0, scale) in zip(xns, blocks):
        if scale != 1.0:
            xn = xn * scale
        out_ref[rows, o0:o0 + HEAD_PAIR] = xn.astype(BF16)


def _in_projection(h, mod_l, mod_row, norm_g, w_in2, qg, kg, rope_tabs, tm, u_rows, u_all=None):
    bsz, t, d = h.shape
    rope = rope_tabs is not None
    row_map = lambda b, i: (b, i, 0)
    in_specs = [pl.BlockSpec((None, tm, d), row_map),
                pl.BlockSpec((None, 1, 6 * d), lambda b, i: (mod_row(b), 0, 0)),
                _const_spec((1, d)),
                _const_spec((d, D_PROJ)),
                _const_spec((1, HEAD_PAIR)),
                _const_spec((1, HEAD_PAIR)),
                _const_spec((HEAD_PAIR, HEAD_PAIR))]
    lane = jnp.arange(HEAD_PAIR)
    head_ones = (lane[:, None] // HEAD_DIM == lane[None, :] // HEAD_DIM).astype(BF16)
    args = [h, mod_l, norm_g, w_in2, qg, kg, head_ones]
    if rope:
        swap = (lane[:, None] == (lane[None, :] ^ AXIS_PAIRS)).astype(BF16)
        in_specs += [_const_spec((HEAD_PAIR, HEAD_PAIR))]
        in_specs += [pl.BlockSpec((tm, HEAD_PAIR), lambda b, i: (i, 0))] * 2
        args += [swap] + list(rope_tabs)
        u_off, aliases = 0, {}
    else:
        in_specs += [pl.BlockSpec(memory_space=pl.ANY)]
        args += [u_all]
        u_off, aliases = (u_rows - t) // tm, {len(args) - 1: 2}
    out_shape = (jax.ShapeDtypeStruct((bsz, t, D_CONV), F32),
                 jax.ShapeDtypeStruct((bsz, t, D_CONV), F32),
                 jax.ShapeDtypeStruct((bsz, u_rows, D_SSM), F32),
                 jax.ShapeDtypeStruct((bsz, t, D_ATTN), BF16),
                 jax.ShapeDtypeStruct((bsz, t, 2 * LANES), BF16),
                 jax.ShapeDtypeStruct((bsz, t, 2 * LANES), BF16))
    out_specs = (pl.BlockSpec((None, tm, D_CONV), row_map),
                 pl.BlockSpec((None, tm, D_CONV), row_map),
                 pl.BlockSpec((None, tm, D_SSM), lambda b, i: (b, u_off + i, 0)),
                 pl.BlockSpec((None, tm, D_ATTN), row_map),
                 pl.BlockSpec((None, tm, 2 * LANES), row_map),
                 pl.BlockSpec((None, tm, 2 * LANES), row_map))
    return pl.pallas_call(
        functools.partial(_inproj_kernel, rope=rope),
        out_shape=out_shape, grid=(bsz, t // tm), in_specs=in_specs, out_specs=out_specs,
        input_output_aliases=aliases,
        compiler_params=_cparams(2),
        name="in_projection_lat" if rope else "in_projection_ctx",
    )(*args)


def _scan_kernel(uf_in_ref, ur_in_ref, bmat_ref, lam_ref, cmat_ref, yf_out_ref, yr_out_ref,
                 uf_ref, ur_ref, yf_ref, yr_ref, sf0_ref, sf1_ref, sr0_ref, sr1_ref, hf_ref, hr_ref,
                 *, steps):
    j = pl.program_id(0)

    @pl.when(j == 0)
    def _():
        for ref in (sf0_ref, sf1_ref, sr0_ref, sr1_ref):
            ref[...] = jnp.zeros_like(ref)
        hf_ref[...] = jnp.zeros_like(hf_ref)
        hr_ref[...] = jnp.zeros_like(hr_ref)

    def lam_rows(r):
        return jnp.broadcast_to(lam_ref[r:r + 1, :], (SUBLANES, N_STATE))

    def phases(sf_cur, sr_cur, sf_prev, sr_prev):
        for t in range(steps):
            rows = slice(t * SUBLANES, (t + 1) * SUBLANES)
            uf_ref[rows, :] = uf_in_ref[:, t, :]
            ur_ref[rows, :] = ur_in_ref[:, t, :]
        yf_ref[...] = jnp.dot(sf_cur[...].astype(BF16), cmat_ref[0], preferred_element_type=F32)
        yr_ref[...] = jnp.dot(sr_cur[...].astype(BF16), cmat_ref[1], preferred_element_type=F32)
        sf_cur[...] = jnp.dot(uf_ref[...].astype(BF16), bmat_ref[0], preferred_element_type=F32)
        sr_cur[...] = jnp.dot(ur_ref[...].astype(BF16), bmat_ref[1], preferred_element_type=F32)

        lf_re, lf_im, lr_re, lr_im = lam_rows(0), lam_rows(1), lam_rows(2), lam_rows(3)

        def step(s_ref, t, l_re, l_im, h_re, h_im):
            rows = slice(t * SUBLANES, (t + 1) * SUBLANES)
            n_re = l_re * h_re - l_im * h_im + s_ref[rows, 0:N_STATE]
            n_im = l_re * h_im + l_im * h_re + s_ref[rows, N_STATE:2 * N_STATE]
            s_ref[rows, 0:N_STATE] = n_re
            s_ref[rows, N_STATE:2 * N_STATE] = n_im
            return n_re, n_im

        f_re, f_im = hf_ref[:, 0:N_STATE], hf_ref[:, N_STATE:2 * N_STATE]
        r_re, r_im = hr_ref[:, 0:N_STATE], hr_ref[:, N_STATE:2 * N_STATE]
        for t in range(steps):
            f_re, f_im = step(sf_prev, t, lf_re, lf_im, f_re, f_im)
        for t in range(steps):
            r_re, r_im = step(sr_prev, steps - 1 - t, lr_re, lr_im, r_re, r_im)
        hf_ref[:, 0:N_STATE] = f_re
        hf_ref[:, N_STATE:2 * N_STATE] = f_im
        hr_ref[:, 0:N_STATE] = r_re
        hr_ref[:, N_STATE:2 * N_STATE] = r_im

    @pl.when(lax.rem(j, 2) == 0)
    def _():
        phases(sf0_ref, sr0_ref, sf1_ref, sr1_ref)

    @pl.when(lax.rem(j, 2) == 1)
    def _():
        phases(sf1_ref, sr1_ref, sf0_ref, sr0_ref)

    @pl.when(j >= 2)
    def _():
        for t in range(steps):
            rows = slice(t * SUBLANES, (t + 1) * SUBLANES)
            yf_out_ref[:, t, :] = yf_ref[rows, :]
            yr_out_ref[:, t, :] = yr_ref[rows, :]


def _ssm_scan(u_all, n_lat, bmat, lamv, cmat, steps):
    bsz, total, _ = u_all.shape
    rows = steps * bsz
    n, nl = total // steps, n_lat // steps
    nc = n - nl
    blk = lambda f: pl.BlockSpec((bsz, steps, D_SSM), lambda j: (0, f(j), 0))
    fwd = lambda p: jnp.where(p < nc, nl + p, p - nc)
    rev = lambda p: n - 1 - p
    drive = lambda f: blk(lambda j: f(jnp.minimum(j, n - 1)))
    read = lambda f: blk(lambda j: f(jnp.maximum(j - 2, 0)))
    state = pltpu.VMEM((rows, 2 * N_STATE), F32)
    chunk = pltpu.VMEM((rows, D_SSM), F32)
    carry = pltpu.VMEM((SUBLANES, 2 * N_STATE), F32)
    return pl.pallas_call(
        functools.partial(_scan_kernel, steps=steps),
        out_shape=(jax.ShapeDtypeStruct(u_all.shape, F32),) * 2,
        grid=(n + 2,),
        in_specs=[drive(fwd), drive(rev),
                  _const_spec((2, D_SSM, 2 * N_STATE)),
                  _const_spec((4, N_STATE)),
                  _const_spec((2, 2 * N_STATE, D_SSM))],
        out_specs=(read(fwd), read(rev)),
        scratch_shapes=[chunk, chunk, chunk, chunk, state, state, state, state, carry, carry],
        compiler_params=_cparams(1),
        name="s5_scan",
    )(u_all, u_all, bmat, lamv, cmat)


def _attn_kernel(sink_ref, q_ref, *refs, window):
    if window:
        kp_ref, ko_ref, kn_ref, vp_ref, vo_ref, vn_ref, kc_ref, vc_ref, o_ref = refs
    else:
        kc_ref, vc_ref, o_ref = refs
    n = pl.program_id(1)
    last = pl.num_programs(1) - 1
    rows = 2 * BLOCK
    lane = lax.broadcasted_iota(jnp.int32, (1, LANES), 1)
    lo = lane < HEAD_DIM
    qi = lax.broadcasted_iota(jnp.int32, (rows, BLOCK), 0) & (BLOCK - 1)
    kj = lax.broadcasted_iota(jnp.int32, (rows, BLOCK), 1)
    upper_head = lax.broadcasted_iota(jnp.int32, (rows, 1), 0) >= BLOCK
    zero = jnp.zeros((), BF16)
    nt = (((1,), (1,)), ((), ()))
    heads_per_group = N_Q_HEADS // N_KV_HEADS
    units = range(D_ATTN // LANES)
    cols = [slice((2 * j // heads_per_group) * LANES, (2 * j // heads_per_group + 1) * LANES) for j in units]
    qs = []
    for j in units:
        qp = q_ref[:, j * LANES:(j + 1) * LANES]
        qs.append(jnp.concatenate([jnp.where(lo, qp, zero), jnp.where(lo, zero, qp)], axis=0))
    s_all = []
    for j in units:
        scores = []
        if window:
            s = lax.dot_general(qs[j], kp_ref[:, cols[j]], nt, preferred_element_type=F32)
            scores.append(jnp.where((kj >= qi) & (n > 0), s, NEG_INF))
            scores.append(lax.dot_general(qs[j], ko_ref[:, cols[j]], nt, preferred_element_type=F32))
            s = lax.dot_general(qs[j], kn_ref[:, cols[j]], nt, preferred_element_type=F32)
            scores.append(jnp.where((kj <= qi) & (n < last), s, NEG_INF))
        scores.append(lax.dot_general(qs[j], kc_ref[:, cols[j]], nt, preferred_element_type=F32))
        s_all.append(jnp.concatenate(scores, axis=1))
    sinks = [jnp.where(upper_head, sink_ref[2 * j + 1], sink_ref[2 * j]) for j in units]
    ms = [jnp.maximum(sinks[j], jnp.max(s_all[j], axis=-1, keepdims=True)) for j in units]
    es = [jnp.exp(s_all[j] - ms[j]) for j in units]
    denoms = [jnp.exp(sinks[j] - ms[j]) + jnp.sum(es[j], axis=-1, keepdims=True) for j in units]
    accs = []
    for j in units:
        values = [vp_ref[:, cols[j]], vo_ref[:, cols[j]], vn_ref[:, cols[j]]] if window else []
        values.append(vc_ref[:, cols[j]])
        accs.append(jnp.dot(es[j].astype(BF16), jnp.concatenate(values, axis=0), preferred_element_type=F32))
    for j in units:
        out = accs[j] / denoms[j]
        o_ref[:, j * LANES:(j + 1) * LANES] = jnp.where(lo, out[0:BLOCK], out[BLOCK:2 * BLOCK]).astype(BF16)


def _attention(sink, q, k2, v2, kc2, vc2, window):
    bsz, t, _ = q.shape
    n_ctx = kc2.shape[1]
    nb = t // BLOCK
    kw = 2 * LANES
    blk = lambda f: pl.BlockSpec((None, BLOCK, kw), f)
    in_specs = [pl.BlockSpec(memory_space=pltpu.SMEM),
                pl.BlockSpec((None, BLOCK, D_ATTN), lambda b, n: (b, n, 0))]
    args = [sink, q]
    if window:
        prev = lambda b, n: (b, jnp.maximum(n - 1, 0), 0)
        own = lambda b, n: (b, n, 0)
        nxt = lambda b, n: (b, jnp.minimum(n + 1, nb - 1), 0)
        in_specs += [blk(prev), blk(own), blk(nxt)] * 2
        args += [k2, k2, k2, v2, v2, v2]
    ctx_spec = pl.BlockSpec((None, n_ctx, kw), lambda b, n: (b, 0, 0))
    in_specs += [ctx_spec, ctx_spec]
    args += [kc2, vc2]
    return pl.pallas_call(
        functools.partial(_attn_kernel, window=window),
        out_shape=jax.ShapeDtypeStruct((bsz, t, D_ATTN), BF16),
        grid=(bsz, nb), in_specs=in_specs,
        out_specs=pl.BlockSpec((None, BLOCK, D_ATTN), lambda b, n: (b, n, 0)),
        compiler_params=_cparams(2),
        name="window_attention" if window else "context_attention",
    )(*args)


def _mixout_kernel(h_ref, mod_ref, cv_ref, cvp_ref, cvn_ref, cb_ref, u_ref, yf_ref, yr_ref, att_ref,
                   convw_ref, d_ref, wglu_ref, bglu_ref, wout_ref, g2_ref, rw_ref, rb_ref,
                   hmid_ref, t_ref, gates_ref):
    i = pl.program_id(1)
    last = pl.num_programs(1) - 1
    tm = cv_ref.shape[0]
    cv = cv_ref[...]
    row = lax.broadcasted_iota(jnp.int32, (tm, 1), 0)
    before = jnp.where(i > 0, cvp_ref[SUBLANES - 1:SUBLANES, :], 0.0)
    after = jnp.where(i < last, cvn_ref[0:1, :], 0.0)
    down = jnp.where(row == 0, before, pltpu.roll(cv, 1, 0))
    up = jnp.where(row == tm - 1, after, pltpu.roll(cv, tm - 1, 0))
    w = convw_ref[...]
    y_conv = cb_ref[...] * (w[0:1, :] * down + w[1:2, :] * cv + w[2:3, :] * up)
    y = d_ref[...] * u_ref[...] + yf_ref[...] + yr_ref[...]
    gl = jax.nn.gelu(y)
    z = jnp.dot(gl.astype(BF16), wglu_ref[...], preferred_element_type=F32) + bglu_ref[...]
    y_ssm = gl * jax.nn.sigmoid(z)
    mix = (jnp.dot(y_conv.astype(BF16), wout_ref[0:D_CONV, :], preferred_element_type=F32)
           + jnp.dot(y_ssm.astype(BF16), wout_ref[D_CONV:D_CONV + D_SSM, :], preferred_element_type=F32)
           + jnp.dot(att_ref[...], wout_ref[D_CONV + D_SSM:, :], preferred_element_type=F32))
    mod = mod_ref[...]
    h_mid = h_ref[...] + mod[:, 2 * D_MODEL:3 * D_MODEL] * mix
    hmid_ref[...] = h_mid
    t = _rms_modulate(h_mid, g2_ref[...], mod[:, 3 * D_MODEL:4 * D_MODEL], mod[:, 4 * D_MODEL:5 * D_MODEL])
    t_ref[...] = t.astype(BF16)
    t_hi = t.astype(BF16)
    t_lo = (t - t_hi.astype(F32)).astype(BF16)
    nt = (((1,), (1,)), ((), ()))
    a = lax.dot_general(rw_ref[...], t_hi, nt, preferred_element_type=F32)
    b = lax.dot_general(rw_ref[...], t_lo, nt, preferred_element_type=F32)
    logits = a[0:N_EXPERTS] + a[N_EXPERTS:2 * N_EXPERTS] + b[0:N_EXPERTS]
    scores = jax.nn.sigmoid(logits)
    eidx = lax.broadcasted_iota(jnp.int32, (N_EXPERTS, 1), 0).astype(F32)
    biased = scores + rb_ref[...]
    sel = jnp.zeros(scores.shape, F32)
    for _ in range(TOP_K):
        best = jnp.max(biased, axis=0, keepdims=True)
        first = jnp.min(jnp.where(biased == best, eidx, float(N_EXPERTS)), axis=0, keepdims=True)
        pick = eidx == first
        sel = jnp.where(pick, scores, sel)
        biased = jnp.where(pick, -jnp.inf, biased)
    gates = sel / jnp.sum(sel, axis=0, keepdims=True) * ROUTED_SCALE
    padded = jnp.concatenate([gates, jnp.zeros((LANES - N_EXPERTS, gates.shape[1]), F32)], axis=0)
    gates_ref[...] = padded.T


def _mixer_output(h, mod_l, mod_row, cv, cb, u, yf, yr, u_row0, att,
                  conv_w, ssm_d, w_glu, b_glu, w_out, norm_g, router_w, router_b, tm):
    bsz, t, d = h.shape
    halo = tm // SUBLANES
    n_halo = t // SUBLANES
    row_map = lambda b, i: (b, i, 0)
    u_spec = pl.BlockSpec((None, tm, D_SSM), lambda b, i: (b, u_row0 // tm + i, 0))
    in_specs = [pl.BlockSpec((None, tm, d), row_map),
                pl.BlockSpec((None, 1, 6 * d), lambda b, i: (mod_row(b), 0, 0)),
                pl.BlockSpec((None, tm, D_CONV), row_map),
                pl.BlockSpec((None, SUBLANES, D_CONV), lambda b, i: (b, jnp.maximum(i * halo - 1, 0), 0)),
                pl.BlockSpec((None, SUBLANES, D_CONV), lambda b, i: (b, jnp.minimum((i + 1) * halo, n_halo - 1), 0)),
                pl.BlockSpec((None, tm, D_CONV), row_map),
                u_spec, u_spec, u_spec,
                pl.BlockSpec((None, tm, D_ATTN), row_map),
                _const_spec((3, D_CONV)),
                _const_spec((1, D_SSM)),
                _const_spec((D_SSM, D_SSM)),
                _const_spec((1, D_SSM)),
                _const_spec((d, d)),
                _const_spec((1, d)),
                _const_spec((LANES, d)),
                _const_spec((N_EXPERTS, 1))]
    out_shape = (jax.ShapeDtypeStruct((bsz, t, d), F32),
                 jax.ShapeDtypeStruct((bsz, t, d), BF16),
                 jax.ShapeDtypeStruct((bsz, t, LANES), F32))
    out_specs = (pl.BlockSpec((None, tm, d), row_map),
                 pl.BlockSpec((None, tm, d), row_map),
                 pl.BlockSpec((None, tm, LANES), row_map))
    return pl.pallas_call(
        _mixout_kernel, out_shape=out_shape, grid=(bsz, t // tm),
        in_specs=in_specs, out_specs=out_specs,
        compiler_params=_cparams(2),
        name="mixer_output",
    )(h, mod_l, cv, cv, cv, cb, u, yf, yr, att,
      conv_w, ssm_d, w_glu, b_glu, w_out, norm_g, router_w, router_b)


EXPERTS_PER_CHUNK = 4


def _moe_kernel(t_ref, gates_ref, hmid_ref, mod_ref, wg_ref, wu_ref, wd_ref,
                wsg_ref, wsu_ref, wsd_ref, o_ref, acc_ref):
    t = t_ref[...]
    gates = gates_ref[...]
    hs = _silu(jnp.dot(t, wsg_ref[...], preferred_element_type=F32)) * jnp.dot(
        t, wsu_ref[...], preferred_element_type=F32)
    acc_ref[...] = jnp.dot(hs.astype(BF16), wsd_ref[...], preferred_element_type=F32)
    width = EXPERTS_PER_CHUNK * D_EXPERT
    for c in range(N_EXPERTS // EXPERTS_PER_CHUNK):
        cols = slice(c * width, (c + 1) * width)
        hid = _silu(jnp.dot(t, wg_ref[:, cols], preferred_element_type=F32)) * jnp.dot(
            t, wu_ref[:, cols], preferred_element_type=F32)
        gated = []
        for e in range(EXPERTS_PER_CHUNK):
            k = c * EXPERTS_PER_CHUNK + e
            gated.append((hid[:, e * D_EXPERT:(e + 1) * D_EXPERT] * gates[:, k:k + 1]).astype(BF16))
        acc_ref[...] += jnp.dot(jnp.concatenate(gated, axis=1), wd_ref[cols, :],
                                preferred_element_type=F32)
    mod = mod_ref[...]
    o_ref[...] = hmid_ref[...] + mod[:, 5 * D_MODEL:6 * D_MODEL] * acc_ref[...]


def _moe(t, gates, hmid, mod_l, mod_row, wg, wu, wd, wsg, wsu, wsd, tm):
    bsz, n, d = hmid.shape
    row_map = lambda b, i: (b, i, 0)
    return pl.pallas_call(
        _moe_kernel,
        out_shape=jax.ShapeDtypeStruct((bsz, n, d), F32),
        grid=(bsz, n // tm),
        in_specs=[pl.BlockSpec((None, tm, d), row_map),
                  pl.BlockSpec((None, tm, LANES), row_map),
                  pl.BlockSpec((None, tm, d), row_map),
                  pl.BlockSpec((None, 1, 6 * d), lambda b, i: (mod_row(b), 0, 0)),
                  _const_spec(wg.shape), _const_spec(wu.shape), _const_spec(wd.shape),
                  _const_spec(wsg.shape), _const_spec(wsu.shape), _const_spec(wsd.shape)],
        out_specs=pl.BlockSpec((None, tm, d), row_map),
        scratch_shapes=[pltpu.VMEM((tm, d), F32)],
        compiler_params=_cparams(2),
        name="moe",
    )(t, gates, hmid, mod_l, wg, wu, wd, wsg, wsu, wsd)


def _rope_tables(n_tokens):
    rows = n_tokens // GRID_W
    row = jnp.repeat(jnp.arange(rows, dtype=F32), GRID_W)
    col = jnp.tile(jnp.arange(GRID_W, dtype=F32), rows)
    inv_freq = ROPE_BASE ** (-jnp.arange(AXIS_PAIRS, dtype=F32) / AXIS_PAIRS)
    ar, ac = row[:, None] * inv_freq, col[:, None] * inv_freq
    cos = jnp.concatenate([jnp.cos(ar), jnp.cos(ar), jnp.cos(ac), jnp.cos(ac)], axis=1)
    sin = jnp.concatenate([-jnp.sin(ar), jnp.sin(ar), -jnp.sin(ac), jnp.sin(ac)], axis=1)
    return jnp.tile(cos, (1, HEAD_PAIR // HEAD_DIM)), jnp.tile(sin, (1, HEAD_PAIR // HEAD_DIM))


def _ssm_params(lam_re, lam_im, log_dt, b_re, b_im, c_re, c_im):
    lr, li = lam_re.astype(F32), lam_im.astype(F32)
    dt = jnp.exp(log_dt.astype(F32))[..., None]
    mag = jnp.exp(lr * dt)
    ar, ai = mag * jnp.cos(li * dt), mag * jnp.sin(li * dt)
    den = lr * lr + li * li
    qr = ((ar - 1.0) * lr + ai * li) / den
    qi = (ai * lr - (ar - 1.0) * li) / den
    br, bi = b_re.astype(F32), b_im.astype(F32)
    bbar_re = qr[..., None] * br - qi[..., None] * bi
    bbar_im = qr[..., None] * bi + qi[..., None] * br
    eye = jnp.eye(SSM_GROUPS, dtype=F32)
    drive = lambda m: jnp.einsum('gk,dgph->dghkp', eye, m).reshape(2, D_SSM, N_STATE)
    read = lambda m: jnp.einsum('gk,dghp->dgpkh', eye, m.astype(F32)).reshape(2, N_STATE, D_SSM)
    bmat = jnp.concatenate([drive(bbar_re), drive(bbar_im)], axis=2)
    cmat = jnp.concatenate([read(c_re), -read(c_im)], axis=1)
    lam_flat = lambda m: m.reshape(2, N_STATE)
    ar, ai = lam_flat(ar), lam_flat(ai)
    lamv = jnp.stack([ar[0], ai[0], ar[1], ai[1]])
    return bmat.astype(BF16), lamv, cmat.astype(BF16)


def _dup_heads(w):
    d = w.shape[0]
    w = w.reshape(d, N_KV_HEADS, 1, HEAD_DIM)
    return jnp.broadcast_to(w, (d, N_KV_HEADS, 2, HEAD_DIM)).reshape(d, 2 * N_KV_HEADS * HEAD_DIM)


def kernel(x, c, ctx, c_ctx, w_mod, b_mod, norm1_g, norm2_g, w_in, conv_w, ssm_lam_re, ssm_lam_im, ssm_log_dt, ssm_b_re, ssm_b_im, ssm_c_re, ssm_c_im, ssm_d, w_glu, b_glu, q_norm_g, k_norm_g, attn_sink, w_out, router_w, router_bias, w_exp_gate, w_exp_up, w_exp_down, w_sh_gate, w_sh_up, w_sh_down):
    bsz, n_lat, d = x.shape
    n_ctx = ctx.shape[1]
    depth = w_mod.shape[0]
    assert bsz == SUBLANES and d == D_MODEL

    mod_rows = 2 * SUBLANES
    cvec = jnp.zeros((mod_rows, d), F32).at[:bsz].set(c).at[bsz].set(c_ctx)
    mod = _modulation(cvec, w_mod, b_mod).reshape(depth, mod_rows, 1, 6 * d)
    lat_row = lambda b: b
    ctx_row = lambda b: bsz
    rope_tabs = _rope_tables(n_lat)

    tm_lat, tm_ctx, scan_steps = 512, 256, 64
    kv0 = 3 * D_CONV + D_SSM + D_ATTN
    h, hc = x, ctx
    for l in range(depth):
        ctx_out = l < depth - 1
        mod_l = mod[l]
        w_in2 = jnp.concatenate([w_in[l][:, :kv0], _dup_heads(w_in[l][:, kv0:kv0 + 2 * HEAD_DIM]),
                                 _dup_heads(w_in[l][:, kv0 + 2 * HEAD_DIM:])], axis=1).astype(BF16)
        qg = jnp.tile(q_norm_g[l].astype(F32), HEAD_PAIR // HEAD_DIM)[None, :]
        kg = jnp.tile(k_norm_g[l].astype(F32), HEAD_PAIR // HEAD_DIM)[None, :]
        n1 = norm1_g[l][None, :]
        u_rows = n_lat + n_ctx
        cv, cb, u_all, q, k2, v2 = _in_projection(h, mod_l, lat_row, n1, w_in2, qg, kg, rope_tabs, tm_lat, u_rows)
        cvc, cbc, u_all, qc, kc2, vc2 = _in_projection(hc, mod_l, ctx_row, n1, w_in2, qg, kg, None, tm_ctx,
                                                       u_rows, u_all)

        bmat, lamv, cmat = _ssm_params(ssm_lam_re[l], ssm_lam_im[l], ssm_log_dt[l], ssm_b_re[l],
                                       ssm_b_im[l], ssm_c_re[l], ssm_c_im[l])
        yf, yr = _ssm_scan(u_all, n_lat, bmat, lamv, cmat, scan_steps)

        sink = attn_sink[l].astype(F32)
        att = _attention(sink, q, k2, v2, kc2, vc2, window=True)

        rw = router_w[l].astype(F32)
        rw_hi = rw.astype(BF16)
        rw_lo = (rw - rw_hi.astype(F32)).astype(BF16)
        rw2 = jnp.concatenate([rw_hi.T, rw_lo.T, jnp.zeros((LANES - 2 * N_EXPERTS, d), BF16)], axis=0)
        rb = router_bias[l].astype(F32)[:, None]
        post = dict(conv_w=conv_w[l], ssm_d=ssm_d[l][None, :], w_glu=w_glu[l].astype(BF16),
                    b_glu=b_glu[l][None, :], w_out=w_out[l].astype(BF16), norm_g=norm2_g[l][None, :],
                    router_w=rw2, router_b=rb)
        experts = (w_exp_gate[l].astype(BF16).reshape(d, -1), w_exp_up[l].astype(BF16).reshape(d, -1),
                   w_exp_down[l].astype(BF16).reshape(-1, d), w_sh_gate[l].astype(BF16),
                   w_sh_up[l].astype(BF16), w_sh_down[l].astype(BF16))
        hmid, t, gates = _mixer_output(h, mod_l, lat_row, cv, cb, u_all, yf, yr, 0, att, tm=tm_lat, **post)
        h_new = _moe(t, gates, hmid, mod_l, lat_row, *experts, tm=tm_lat)
        if ctx_out:
            attc = _attention(sink, qc, None, None, kc2, vc2, window=False)
            hmid_c, tc, gates_c = _mixer_output(hc, mod_l, ctx_row, cvc, cbc, u_all, yf, yr, n_lat, attc,
                                                tm=tm_ctx, **post)
            flat = lambda a: a.reshape(1, bsz * n_ctx, a.shape[-1])
            hc = _moe(flat(tc), flat(gates_c), flat(hmid_c), mod_l, ctx_row, *experts,
                      tm=tm_lat).reshape(hc.shape)
        h = h_new
    return h
```

```python
import functools
import math

import jax
import jax.numpy as jnp
import numpy as np
from jax import lax
from jax.experimental import pallas as pl
from jax.experimental.pallas import tpu as pltpu

F32 = jnp.float32
BF16 = jnp.bfloat16
HIGHEST = lax.Precision.HIGHEST

D_MODEL = 1024
D_CONV = 256
D_SSM = 256
SSM_GROUP = 16
SSM_GROUPS = 16
SSM_STATE = 64
N_STATE = SSM_GROUPS * SSM_STATE
HEAD_DIM = 64
D_ATTN = 512
N_Q_HEADS = 8
N_KV_HEADS = 2
AXIS_PAIRS = 16
ROPE_BASE = 10000.0
GRID_W = 64
BLOCK = 128
N_EXPERTS = 32
TOP_K = 4
D_EXPERT = 128
ROUTED_SCALE = 2.5
NORM_EPS = 1e-6
NEG_INF = -1e30
D_PROJ = 3 * D_CONV + D_SSM + D_ATTN + 2 * N_KV_HEADS * HEAD_DIM
LANES = 128
SUBLANES = 8
MXU_TILE = 256
VMEM_LIMIT = 56 * 1024 * 1024


def _cparams(n_axes):
    return pltpu.CompilerParams(dimension_semantics=("arbitrary",) * n_axes,
                                vmem_limit_bytes=VMEM_LIMIT)


def _const_spec(shape):
    nd = len(shape)
    return pl.BlockSpec(shape, lambda *_: (0,) * nd, pipeline_mode=pl.Buffered(1))


def _silu(x):
    return x * jax.nn.sigmoid(x)


def _rms_modulate(x, gain, shift, scale):
    ms = jnp.mean(x * x, axis=-1, keepdims=True)
    return (x * lax.rsqrt(ms + NORM_EPS) * gain) * (1.0 + scale) + shift


def _mod_kernel(c_ref, w_ref, b_ref, o_ref):
    cv = c_ref[...]
    o_ref[...] = jnp.dot(_silu(cv), w_ref[...], precision=HIGHEST,
                         preferred_element_type=F32) + b_ref[...]


def _modulation(cvec, w_mod, b_mod):
    depth, d, n = w_mod.shape
    rows = cvec.shape[0]
    tn = 1536
    return pl.pallas_call(
        _mod_kernel,
        out_shape=jax.ShapeDtypeStruct((depth, rows, n), F32),
        grid=(depth, n // tn),
        in_specs=[pl.BlockSpec((rows, d), lambda l, j: (0, 0)),
                  pl.BlockSpec((None, d, tn), lambda l, j: (l, 0, j)),
                  pl.BlockSpec((None, 1, tn), lambda l, j: (l, 0, j))],
        out_specs=pl.BlockSpec((None, rows, tn), lambda l, j: (l, 0, j)),
        compiler_params=_cparams(2),
        name="modulation",
    )(cvec, w_mod, b_mod.reshape(depth, 1, n))


HEAD_PAIR = 4 * HEAD_DIM
LOG2E = math.log2(math.e)
Q_SCALE = HEAD_DIM ** -0.5 * LOG2E


def _inproj_kernel(*refs, rope):
    if rope:
        (x_ref, mod_ref, g_ref, w_ref, qg_ref, kg_ref, ones_ref, perm_ref, cos_ref, sin_ref,
         cv_ref, cb_ref, u_ref, q_ref, kv_ref) = refs
    else:
        (x_ref, mod_ref, g_ref, w_ref, qg_ref, kg_ref, ones_ref, _,
         cv_ref, cb_ref, u_ref, q_ref, kv_ref) = refs
    mod = mod_ref[...]
    mb = _rms_modulate(x_ref[...], g_ref[...], mod[:, 0:D_MODEL], mod[:, D_MODEL:2 * D_MODEL]).astype(BF16)

    def proj(c0, width):
        return jnp.dot(mb, w_ref[:, c0:c0 + width], preferred_element_type=F32)

    cv_ref[...] = proj(2 * D_CONV, D_CONV) * proj(0, D_CONV)
    cb_ref[...] = proj(D_CONV, D_CONV)
    u_ref[...] = proj(3 * D_CONV, D_SSM)
    q0 = 3 * D_CONV + D_SSM
    kv0 = q0 + D_ATTN
    blocks = [(q0 + jj * HEAD_PAIR, qg_ref, q_ref, jj * HEAD_PAIR, Q_SCALE, False)
              for jj in range(D_ATTN // HEAD_PAIR)] + [(kv0, kg_ref, kv_ref, 0, 1.0, True)]
    is_k = lax.broadcasted_iota(jnp.int32, (1, HEAD_PAIR), 1) < N_KV_HEADS * HEAD_DIM
    xs = [proj(blk[0], HEAD_PAIR) for blk in blocks]
    ssqs = [jnp.dot((x * x).astype(BF16), ones_ref[...], preferred_element_type=F32) for x in xs]
    norms = [lax.rsqrt(ssq * (1.0 / HEAD_DIM) + NORM_EPS) for ssq in ssqs]
    norms = [jnp.where(is_k, nm, 1.0) if blk[5] else nm for nm, blk in zip(norms, blocks)]
    xns = [x * nm * blk[1][...] for x, nm, blk in zip(xs, norms, blocks)]
    if rope:
        partners = [jnp.dot(xn.astype(BF16), perm_ref[...], preferred_element_type=F32) for xn in xns]
        cos, sin = cos_ref[...], sin_ref[...]
        xns = [xn * (jnp.where(is_k, cos, 1.0) if blk[5] else cos) + pt * (jnp.where(is_k, sin, 0.0) if blk[5] else sin)
               for xn, pt, blk in zip(xns, partners, blocks)]
    for xn, (_, _, out_ref, o0, scale, _) in zip(xns, blocks):
        if scale != 1.0:
            xn = xn * scale
        out_ref[:, o0:o0 + HEAD_PAIR] = xn.astype(BF16)


def _in_projection(h, mod_l, mod_row, norm_g, w_in2, qg, kg, rope_tabs, tm, u_rows, u_all=None):
    bsz, t, d = h.shape
    rope = rope_tabs is not None
    row_map = lambda b, i: (b, i, 0)
    in_specs = [pl.BlockSpec((None, tm, d), row_map),
                pl.BlockSpec((None, 1, 6 * d), lambda b, i: (mod_row(b), 0, 0)),
                _const_spec((1, d)),
                _const_spec((d, D_PROJ)),
                _const_spec((1, HEAD_PAIR)),
                _const_spec((1, HEAD_PAIR)),
                _const_spec((HEAD_PAIR, HEAD_PAIR))]
    lane = jnp.arange(HEAD_PAIR)
    head_ones = (lane[:, None] // HEAD_DIM == lane[None, :] // HEAD_DIM).astype(BF16)
    args = [h, mod_l, norm_g, w_in2, qg, kg, head_ones]
    if rope:
        swap = (lane[:, None] == (lane[None, :] ^ AXIS_PAIRS)).astype(BF16)
        in_specs += [_const_spec((HEAD_PAIR, HEAD_PAIR))]
        in_specs += [pl.BlockSpec((tm, HEAD_PAIR), lambda b, i: (i, 0))] * 2
        args += [swap] + list(rope_tabs)
        u_off, aliases = 0, {}
    else:
        in_specs += [pl.BlockSpec(memory_space=pl.ANY)]
        args += [u_all]
        u_off, aliases = (u_rows - t) // tm, {len(args) - 1: 2}
    out_shape = (jax.ShapeDtypeStruct((bsz, t, D_CONV), F32),
                 jax.ShapeDtypeStruct((bsz, t, D_CONV), F32),
                 jax.ShapeDtypeStruct((bsz, u_rows, D_SSM), F32),
                 jax.ShapeDtypeStruct((bsz, t, D_ATTN), BF16),
                 jax.ShapeDtypeStruct((bsz, t, HEAD_PAIR), BF16))
    out_specs = (pl.BlockSpec((None, tm, D_CONV), row_map),
                 pl.BlockSpec((None, tm, D_CONV), row_map),
                 pl.BlockSpec((None, tm, D_SSM), lambda b, i: (b, u_off + i, 0)),
                 pl.BlockSpec((None, tm, D_ATTN), row_map),
                 pl.BlockSpec((None, tm, HEAD_PAIR), row_map))
    return pl.pallas_call(
        functools.partial(_inproj_kernel, rope=rope),
        out_shape=out_shape, grid=(bsz, t // tm), in_specs=in_specs, out_specs=out_specs,
        input_output_aliases=aliases,
        compiler_params=_cparams(2),
        name="in_projection_lat" if rope else "in_projection_ctx",
    )(*args)


def _scan_kernel(uf_in_ref, ur_in_ref, bmat_ref, lam_ref, cmat_ref, yf_out_ref, yr_out_ref,
                 uf_ref, ur_ref, yf_ref, yr_ref, sf0_ref, sf1_ref, sr0_ref, sr1_ref, hf_ref, hr_ref,
                 *, steps):
    j = pl.program_id(0)

    @pl.when(j == 0)
    def _():
        for ref in (sf0_ref, sf1_ref, sr0_ref, sr1_ref):
            ref[...] = jnp.zeros_like(ref)
        hf_ref[...] = jnp.zeros_like(hf_ref)
        hr_ref[...] = jnp.zeros_like(hr_ref)

    def lam_rows(r):
        return jnp.broadcast_to(lam_ref[r:r + 1, :], (SUBLANES, N_STATE))

    def phases(sf_cur, sr_cur, sf_prev, sr_prev):
        for t in range(steps):
            rows = slice(t * SUBLANES, (t + 1) * SUBLANES)
            uf_ref[rows, :] = uf_in_ref[:, t, :]
            ur_ref[rows, :] = ur_in_ref[:, t, :]
        yf_ref[...] = jnp.dot(sf_cur[...].astype(BF16), cmat_ref[0], preferred_element_type=F32)
        yr_ref[...] = jnp.dot(sr_cur[...].astype(BF16), cmat_ref[1], preferred_element_type=F32)
        sf_cur[...] = jnp.dot(uf_ref[...].astype(BF16), bmat_ref[0], preferred_element_type=F32)
        sr_cur[...] = jnp.dot(ur_ref[...].astype(BF16), bmat_ref[1], preferred_element_type=F32)

        lf_re, lf_im, lr_re, lr_im = lam_rows(0), lam_rows(1), lam_rows(2), lam_rows(3)

        def step(s_ref, t, l_re, l_im, h_re, h_im):
            rows = slice(t * SUBLANES, (t + 1) * SUBLANES)
            n_re = l_re * h_re - l_im * h_im + s_ref[rows, 0:N_STATE]
            n_im = l_re * h_im + l_im * h_re + s_ref[rows, N_STATE:2 * N_STATE]
            s_ref[rows, 0:N_STATE] = n_re
            s_ref[rows, N_STATE:2 * N_STATE] = n_im
            return n_re, n_im

        f_re, f_im = hf_ref[:, 0:N_STATE], hf_ref[:, N_STATE:2 * N_STATE]
        r_re, r_im = hr_ref[:, 0:N_STATE], hr_ref[:, N_STATE:2 * N_STATE]
        for t in range(steps):
            f_re, f_im = step(sf_prev, t, lf_re, lf_im, f_re, f_im)
        for t in range(steps):
            r_re, r_im = step(sr_prev, steps - 1 - t, lr_re, lr_im, r_re, r_im)
        hf_ref[:, 0:N_STATE] = f_re
        hf_ref[:, N_STATE:2 * N_STATE] = f_im
        hr_ref[:, 0:N_STATE] = r_re
        hr_ref[:, N_STATE:2 * N_STATE] = r_im

    @pl.when(lax.rem(j, 2) == 0)
    def _():
        phases(sf0_ref, sr0_ref, sf1_ref, sr1_ref)

    @pl.when(lax.rem(j, 2) == 1)
    def _():
        phases(sf1_ref, sr1_ref, sf0_ref, sr0_ref)

    @pl.when(j >= 2)
    def _():
        for t in range(steps):
            rows = slice(t * SUBLANES, (t + 1) * SUBLANES)
            yf_out_ref[:, t, :] = yf_ref[rows, :]
            yr_out_ref[:, t, :] = yr_ref[rows, :]


def _ssm_scan(u_all, n_lat, bmat, lamv, cmat, steps):
    bsz, total, _ = u_all.shape
    rows = steps * bsz
    n, nl = total // steps, n_lat // steps
    nc = n - nl
    blk = lambda f: pl.BlockSpec((bsz, steps, D_SSM), lambda j: (0, f(j), 0))
    fwd = lambda p: jnp.where(p < nc, nl + p, p - nc)
    rev = lambda p: n - 1 - p
    drive = lambda f: blk(lambda j: f(jnp.minimum(j, n - 1)))
    read = lambda f: blk(lambda j: f(jnp.maximum(j - 2, 0)))
    state = pltpu.VMEM((rows, 2 * N_STATE), F32)
    chunk = pltpu.VMEM((rows, D_SSM), F32)
    carry = pltpu.VMEM((SUBLANES, 2 * N_STATE), F32)
    return pl.pallas_call(
        functools.partial(_scan_kernel, steps=steps),
        out_shape=(jax.ShapeDtypeStruct(u_all.shape, F32),) * 2,
        grid=(n + 2,),
        in_specs=[drive(fwd), drive(rev),
                  _const_spec((2, D_SSM, 2 * N_STATE)),
                  _const_spec((4, N_STATE)),
                  _const_spec((2, 2 * N_STATE, D_SSM))],
        out_specs=(read(fwd), read(rev)),
        scratch_shapes=[chunk, chunk, chunk, chunk, state, state, state, state, carry, carry],
        compiler_params=_cparams(1),
        name="s5_scan",
    )(u_all, u_all, bmat, lamv, cmat)


def _attn_kernel(sink_ref, q_ref, *refs, window):
    if window:
        kvp_ref, kvo_ref, kvn_ref, kvc_ref, o_ref = refs
    else:
        kvc_ref, o_ref = refs
    n = pl.program_id(1)
    last = pl.num_programs(1) - 1
    rows = 2 * BLOCK
    lane = lax.broadcasted_iota(jnp.int32, (1, LANES), 1)
    lo = lane < HEAD_DIM
    qi = lax.broadcasted_iota(jnp.int32, (rows, BLOCK), 0) & (BLOCK - 1)
    kj = lax.broadcasted_iota(jnp.int32, (rows, BLOCK), 1)
    upper_head = lax.broadcasted_iota(jnp.int32, (rows, 1), 0) >= BLOCK
    zero = jnp.zeros((), BF16)
    nt = (((1,), (1,)), ((), ()))
    k_cols, v_cols = slice(0, LANES), slice(LANES, 2 * LANES)
    units = range(D_ATTN // LANES)
    qs = []
    for j in units:
        qp = q_ref[:, j * LANES:(j + 1) * LANES]
        qs.append(jnp.concatenate([jnp.where(lo, qp, zero), jnp.where(lo, zero, qp)], axis=0))
    s_all = []
    for j in units:
        scores = []
        if window:
            s = lax.dot_general(qs[j], kvp_ref[:, k_cols], nt, preferred_element_type=F32)
            scores.append(jnp.where((kj >= qi) & (n > 0), s, NEG_INF))
            scores.append(lax.dot_general(qs[j], kvo_ref[:, k_cols], nt, preferred_element_type=F32))
            s = lax.dot_general(qs[j], kvn_ref[:, k_cols], nt, preferred_element_type=F32)
            scores.append(jnp.where((kj <= qi) & (n < last), s, NEG_INF))
        scores.append(lax.dot_general(qs[j], kvc_ref[:, k_cols], nt, preferred_element_type=F32))
        s_all.append(jnp.concatenate(scores, axis=1))
    sinks = [jnp.where(upper_head, sink_ref[N_Q_HEADS // 2 + j], sink_ref[j]) * LOG2E for j in units]
    ms = [jnp.maximum(sinks[j], jnp.max(s_all[j], axis=-1, keepdims=True)) for j in units]
    es = [jnp.exp2(s_all[j] - ms[j]) for j in units]
    denoms = [jnp.exp2(sinks[j] - ms[j]) + jnp.sum(es[j], axis=-1, keepdims=True) for j in units]
    values = [kvp_ref[:, v_cols], kvo_ref[:, v_cols], kvn_ref[:, v_cols]] if window else []
    values = jnp.concatenate(values + [kvc_ref[:, v_cols]], axis=0)
    accs = [jnp.dot(es[j].astype(BF16), values, preferred_element_type=F32) for j in units]
    for j in units:
        out = accs[j] / denoms[j]
        o_ref[:, j * LANES:(j + 1) * LANES] = jnp.where(lo, out[0:BLOCK], out[BLOCK:2 * BLOCK]).astype(BF16)


def _attention(sink, q, kv, kv_ctx, window):
    bsz, t, _ = q.shape
    n_ctx = kv_ctx.shape[1]
    nb = t // BLOCK
    blk = lambda f: pl.BlockSpec((None, BLOCK, HEAD_PAIR), f)
    in_specs = [pl.BlockSpec(memory_space=pltpu.SMEM),
                pl.BlockSpec((None, BLOCK, D_ATTN), lambda b, n: (b, n, 0))]
    args = [sink, q]
    if window:
        prev = lambda b, n: (b, jnp.maximum(n - 1, 0), 0)
        own = lambda b, n: (b, n, 0)
        nxt = lambda b, n: (b, jnp.minimum(n + 1, nb - 1), 0)
        in_specs += [blk(prev), blk(own), blk(nxt)]
        args += [kv, kv, kv]
    in_specs += [pl.BlockSpec((None, n_ctx, HEAD_PAIR), lambda b, n: (b, 0, 0))]
    args += [kv_ctx]
    return pl.pallas_call(
        functools.partial(_attn_kernel, window=window),
        out_shape=jax.ShapeDtypeStruct((bsz, t, D_ATTN), BF16),
        grid=(bsz, nb), in_specs=in_specs,
        out_specs=pl.BlockSpec((None, BLOCK, D_ATTN), lambda b, n: (b, n, 0)),
        compiler_params=_cparams(2),
        name="window_attention" if window else "context_attention",
    )(*args)


def _mixout_kernel(h_ref, mod_ref, cv_ref, cvp_ref, cvn_ref, cb_ref, u_ref, yf_ref, yr_ref, att_ref,
                   convw_ref, d_ref, wglu_ref, bglu_ref, wout_ref, g2_ref, rw_ref, rb_ref,
                   hmid_ref, t_ref, gates_ref):
    i = pl.program_id(1)
    last = pl.num_programs(1) - 1
    tm = cv_ref.shape[0]
    cv = cv_ref[...]
    row = lax.broadcasted_iota(jnp.int32, (tm, 1), 0)
    before = jnp.where(i > 0, cvp_ref[SUBLANES - 1:SUBLANES, :], 0.0)
    after = jnp.where(i < last, cvn_ref[0:1, :], 0.0)
    down = jnp.where(row == 0, before, pltpu.roll(cv, 1, 0))
    up = jnp.where(row == tm - 1, after, pltpu.roll(cv, tm - 1, 0))
    w = convw_ref[...]
    y_conv = cb_ref[...] * (w[0:1, :] * down + w[1:2, :] * cv + w[2:3, :] * up)
    y = d_ref[...] * u_ref[...] + yf_ref[...] + yr_ref[...]
    gl = jax.nn.gelu(y)
    z = jnp.dot(gl.astype(BF16), wglu_ref[...], preferred_element_type=F32) + bglu_ref[...]
    y_ssm = gl * jax.nn.sigmoid(z)
    mix = (jnp.dot(y_conv.astype(BF16), wout_ref[0:D_CONV, :], preferred_element_type=F32)
           + jnp.dot(y_ssm.astype(BF16), wout_ref[D_CONV:D_CONV + D_SSM, :], preferred_element_type=F32)
           + jnp.dot(att_ref[...], wout_ref[D_CONV + D_SSM:, :], preferred_element_type=F32))
    mod = mod_ref[...]
    h_mid = h_ref[...] + mod[:, 2 * D_MODEL:3 * D_MODEL] * mix
    hmid_ref[...] = h_mid
    t = _rms_modulate(h_mid, g2_ref[...], mod[:, 3 * D_MODEL:4 * D_MODEL], mod[:, 4 * D_MODEL:5 * D_MODEL])
    t_ref[...] = t.astype(BF16)
    t_hi = t.astype(BF16)
    t_lo = (t - t_hi.astype(F32)).astype(BF16)
    nt = (((1,), (1,)), ((), ()))
    a = lax.dot_general(rw_ref[...], t_hi, nt, preferred_element_type=F32)
    b = lax.dot_general(rw_ref[...], t_lo, nt, preferred_element_type=F32)
    logits = a[0:N_EXPERTS] + a[N_EXPERTS:2 * N_EXPERTS] + b[0:N_EXPERTS]
    scores = jax.nn.sigmoid(logits)
    eidx = lax.broadcasted_iota(jnp.int32, (N_EXPERTS, 1), 0).astype(F32)
    biased = scores + rb_ref[...]
    sel = jnp.zeros(scores.shape, F32)
    for _ in range(TOP_K):
        best = jnp.max(biased, axis=0, keepdims=True)
        first = jnp.min(jnp.where(biased == best, eidx, float(N_EXPERTS)), axis=0, keepdims=True)
        pick = eidx == first
        sel = jnp.where(pick, scores, sel)
        biased = jnp.where(pick, -jnp.inf, biased)
    gates = sel / jnp.sum(sel, axis=0, keepdims=True) * ROUTED_SCALE
    padded = jnp.concatenate([gates, jnp.zeros((LANES - N_EXPERTS, gates.shape[1]), F32)], axis=0)
    gates_ref[...] = padded.T


def _mixer_output(h, mod_l, mod_row, cv, cb, u, yf, yr, u_row0, att,
                  conv_w, ssm_d, w_glu, b_glu, w_out, norm_g, router_w, router_b, tm):
    bsz, t, d = h.shape
    halo = tm // SUBLANES
    n_halo = t // SUBLANES
    row_map = lambda b, i: (b, i, 0)
    u_spec = pl.BlockSpec((None, tm, D_SSM), lambda b, i: (b, u_row0 // tm + i, 0))
    in_specs = [pl.BlockSpec((None, tm, d), row_map),
                pl.BlockSpec((None, 1, 6 * d), lambda b, i: (mod_row(b), 0, 0)),
                pl.BlockSpec((None, tm, D_CONV), row_map),
                pl.BlockSpec((None, SUBLANES, D_CONV), lambda b, i: (b, jnp.maximum(i * halo - 1, 0), 0)),
                pl.BlockSpec((None, SUBLANES, D_CONV), lambda b, i: (b, jnp.minimum((i + 1) * halo, n_halo - 1), 0)),
                pl.BlockSpec((None, tm, D_CONV), row_map),
                u_spec, u_spec, u_spec,
                pl.BlockSpec((None, tm, D_ATTN), row_map),
                _const_spec((3, D_CONV)),
                _const_spec((1, D_SSM)),
                _const_spec((D_SSM, D_SSM)),
                _const_spec((1, D_SSM)),
                _const_spec((d, d)),
                _const_spec((1, d)),
                _const_spec((LANES, d)),
                _const_spec((N_EXPERTS, 1))]
    out_shape = (jax.ShapeDtypeStruct((bsz, t, d), F32),
                 jax.ShapeDtypeStruct((bsz, t, d), BF16),
                 jax.ShapeDtypeStruct((bsz, t, LANES), F32))
    out_specs = (pl.BlockSpec((None, tm, d), row_map),
                 pl.BlockSpec((None, tm, d), row_map),
                 pl.BlockSpec((None, tm, LANES), row_map))
    return pl.pallas_call(
        _mixout_kernel, out_shape=out_shape, grid=(bsz, t // tm),
        in_specs=in_specs, out_specs=out_specs,
        compiler_params=_cparams(2),
        name="mixer_output",
    )(h, mod_l, cv, cv, cv, cb, u, yf, yr, att,
      conv_w, ssm_d, w_glu, b_glu, w_out, norm_g, router_w, router_b)


EXPERTS_PER_CHUNK = 4


def _moe_kernel(t_ref, gates_ref, hmid_ref, mod_ref, wg_ref, wu_ref, wd_ref,
                wsg_ref, wsu_ref, wsd_ref, o_ref, acc_ref):
    t = t_ref[...]
    gates = gates_ref[...]
    hs = _silu(jnp.dot(t, wsg_ref[...], preferred_element_type=F32)) * jnp.dot(
        t, wsu_ref[...], preferred_element_type=F32)
    acc_ref[...] = jnp.dot(hs.astype(BF16), wsd_ref[...], preferred_element_type=F32)
    width = EXPERTS_PER_CHUNK * D_EXPERT
    n_chunks = N_EXPERTS // EXPERTS_PER_CHUNK

    def up(c):
        cols = slice(c * width, (c + 1) * width)
        return (jnp.dot(t, wg_ref[:, cols], preferred_element_type=F32),
                jnp.dot(t, wu_ref[:, cols], preferred_element_type=F32))

    nxt = up(0)
    for c in range(n_chunks):
        hg, hu = nxt
        if c + 1 < n_chunks:
            nxt = up(c + 1)
        hid = _silu(hg) * hu
        gated = []
        for e in range(EXPERTS_PER_CHUNK):
            k = c * EXPERTS_PER_CHUNK + e
            gated.append((hid[:, e * D_EXPERT:(e + 1) * D_EXPERT] * gates[:, k:k + 1]).astype(BF16))
        acc_ref[...] += jnp.dot(jnp.concatenate(gated, axis=1), wd_ref[c * width:(c + 1) * width, :],
                                preferred_element_type=F32)
    mod = mod_ref[...]
    o_ref[...] = hmid_ref[...] + mod[:, 5 * D_MODEL:6 * D_MODEL] * acc_ref[...]


def _moe(t, gates, hmid, mod_l, mod_row, wg, wu, wd, wsg, wsu, wsd, tm):
    bsz, n, d = hmid.shape
    row_map = lambda b, i: (b, i, 0)
    return pl.pallas_call(
        _moe_kernel,
        out_shape=jax.ShapeDtypeStruct((bsz, n, d), F32),
        grid=(bsz, n // tm),
        in_specs=[pl.BlockSpec((None, tm, d), row_map),
                  pl.BlockSpec((None, tm, LANES), row_map),
                  pl.BlockSpec((None, tm, d), row_map),
                  pl.BlockSpec((None, 1, 6 * d), lambda b, i: (mod_row(b), 0, 0)),
                  _const_spec(wg.shape), _const_spec(wu.shape), _const_spec(wd.shape),
                  _const_spec(wsg.shape), _const_spec(wsu.shape), _const_spec(wsd.shape)],
        out_specs=pl.BlockSpec((None, tm, d), row_map),
        scratch_shapes=[pltpu.VMEM((tm, d), F32)],
        compiler_params=_cparams(2),
        name="moe",
    )(t, gates, hmid, mod_l, wg, wu, wd, wsg, wsu, wsd)


def _rope_tables(n_tokens):
    rows = n_tokens // GRID_W
    row = np.repeat(np.arange(rows, dtype=np.float64), GRID_W)
    col = np.tile(np.arange(GRID_W, dtype=np.float64), rows)
    inv_freq = ROPE_BASE ** (-np.arange(AXIS_PAIRS, dtype=np.float64) / AXIS_PAIRS)
    ar, ac = row[:, None] * inv_freq, col[:, None] * inv_freq
    cos = np.concatenate([np.cos(ar), np.cos(ar), np.cos(ac), np.cos(ac)], axis=1).astype(np.float32)
    sin = np.concatenate([-np.sin(ar), np.sin(ar), -np.sin(ac), np.sin(ac)], axis=1).astype(np.float32)
    reps = (1, HEAD_PAIR // HEAD_DIM)
    return jnp.tile(jnp.asarray(cos), reps), jnp.tile(jnp.asarray(sin), reps)


def _ssm_params(lam_re, lam_im, log_dt, b_re, b_im, c_re, c_im):
    lr, li = lam_re.astype(F32), lam_im.astype(F32)
    dt = jnp.exp(log_dt.astype(F32))[..., None]
    mag = jnp.exp(lr * dt)
    ar, ai = mag * jnp.cos(li * dt), mag * jnp.sin(li * dt)
    den = lr * lr + li * li
    qr = ((ar - 1.0) * lr + ai * li) / den
    qi = (ai * lr - (ar - 1.0) * li) / den
    br, bi = b_re.astype(F32), b_im.astype(F32)
    bbar_re = qr[..., None] * br - qi[..., None] * bi
    bbar_im = qr[..., None] * bi + qi[..., None] * br
    eye = jnp.eye(SSM_GROUPS, dtype=F32)
    drive = lambda m: jnp.einsum('gk,dgph->dghkp', eye, m).reshape(2, D_SSM, N_STATE)
    read = lambda m: jnp.einsum('gk,dghp->dgpkh', eye, m.astype(F32)).reshape(2, N_STATE, D_SSM)
    bmat = jnp.concatenate([drive(bbar_re), drive(bbar_im)], axis=2)
    cmat = jnp.concatenate([read(c_re), -read(c_im)], axis=1)
    lam_flat = lambda m: m.reshape(2, N_STATE)
    ar, ai = lam_flat(ar), lam_flat(ai)
    lamv = jnp.stack([ar[0], ai[0], ar[1], ai[1]])
    return bmat.astype(BF16), lamv, cmat.astype(BF16)


def _pair_heads(w, axis):
    shape = w.shape
    split = shape[:axis] + (N_KV_HEADS, N_Q_HEADS // N_KV_HEADS, HEAD_DIM) + shape[axis + 1:]
    return jnp.swapaxes(w.reshape(split), axis, axis + 1).reshape(shape)


def kernel(x, c, ctx, c_ctx, w_mod, b_mod, norm1_g, norm2_g, w_in, conv_w, ssm_lam_re, ssm_lam_im, ssm_log_dt, ssm_b_re, ssm_b_im, ssm_c_re, ssm_c_im, ssm_d, w_glu, b_glu, q_norm_g, k_norm_g, attn_sink, w_out, router_w, router_bias, w_exp_gate, w_exp_up, w_exp_down, w_sh_gate, w_sh_up, w_sh_down):
    bsz, n_lat, d = x.shape
    n_ctx = ctx.shape[1]
    depth = w_mod.shape[0]
    assert bsz == SUBLANES and d == D_MODEL

    mod_rows = 2 * SUBLANES
    cvec = jnp.zeros((mod_rows, d), F32).at[:bsz].set(c).at[bsz].set(c_ctx)
    mod = _modulation(cvec, w_mod, b_mod).reshape(depth, mod_rows, 1, 6 * d)
    lat_row = lambda b: b
    ctx_row = lambda b: bsz
    rope_tabs = _rope_tables(n_lat)

    tm_lat, tm_ctx, scan_steps = 512, 256, 64
    q0, kv0 = 3 * D_CONV + D_SSM, 3 * D_CONV + D_SSM + D_ATTN
    h, hc = x, ctx
    for l in range(depth):
        ctx_out = l < depth - 1
        mod_l = mod[l]
        w_in2 = jnp.concatenate([w_in[l][:, :q0], _pair_heads(w_in[l][:, q0:kv0], 1), w_in[l][:, kv0:]],
                                axis=1).astype(BF16)
        qg = jnp.tile(q_norm_g[l].astype(F32), HEAD_PAIR // HEAD_DIM)[None, :]
        kg = jnp.concatenate([jnp.tile(k_norm_g[l].astype(F32), N_KV_HEADS),
                              jnp.ones((N_KV_HEADS * HEAD_DIM,), F32)])[None, :]
        n1 = norm1_g[l][None, :]
        u_rows = n_lat + n_ctx
        cv, cb, u_all, q, kv = _in_projection(h, mod_l, lat_row, n1, w_in2, qg, kg, rope_tabs, tm_lat, u_rows)
        cvc, cbc, u_all, qc, kvc = _in_projection(hc, mod_l, ctx_row, n1, w_in2, qg, kg, None, tm_ctx,
                                                  u_rows, u_all)

        bmat, lamv, cmat = _ssm_params(ssm_lam_re[l], ssm_lam_im[l], ssm_log_dt[l], ssm_b_re[l],
                                       ssm_b_im[l], ssm_c_re[l], ssm_c_im[l])
        yf, yr = _ssm_scan(u_all, n_lat, bmat, lamv, cmat, scan_steps)

        sink = attn_sink[l].astype(F32)
        att = _attention(sink, q, kv, kvc, window=True)

        w_out_l = jnp.concatenate([w_out[l][:D_CONV + D_SSM], _pair_heads(w_out[l][D_CONV + D_SSM:], 0)], axis=0)
        rw = router_w[l].astype(F32)
        rw_hi = rw.astype(BF16)
        rw_lo = (rw - rw_hi.astype(F32)).astype(BF16)
        rw2 = jnp.concatenate([rw_hi.T, rw_lo.T, jnp.zeros((LANES - 2 * N_EXPERTS, d), BF16)], axis=0)
        rb = router_bias[l].astype(F32)[:, None]
        post = dict(conv_w=conv_w[l], ssm_d=ssm_d[l][None, :], w_glu=w_glu[l].astype(BF16),
                    b_glu=b_glu[l][None, :], w_out=w_out_l.astype(BF16), norm_g=norm2_g[l][None, :],
                    router_w=rw2, router_b=rb)
        experts = (w_exp_gate[l].astype(BF16).reshape(d, -1), w_exp_up[l].astype(BF16).reshape(d, -1),
                   w_exp_down[l].astype(BF16).reshape(-1, d), w_sh_gate[l].astype(BF16),
                   w_sh_up[l].astype(BF16), w_sh_down[l].astype(BF16))
        hmid, t, gates = _mixer_output(h, mod_l, lat_row, cv, cb, u_all, yf, yr, 0, att, tm=tm_lat, **post)
        h_new = _moe(t, gates, hmid, mod_l, lat_row, *experts, tm=tm_lat)
        if ctx_out:
            attc = _attention(sink, qc, None, kvc, window=False)
            hmid_c, tc, gates_c = _mixer_output(hc, mod_l, ctx_row, cvc, cbc, u_all, yf, yr, n_lat, attc,
                                                tm=tm_ctx, **post)
            flat = lambda a: a.reshape(1, bsz * n_ctx, a.shape[-1])
            hc = _moe(flat(tc), flat(gates_c), flat(hmid_c), mod_l, ctx_row, *experts,
                      tm=tm_lat).reshape(hc.shape)
        h = h_new
    return h
```

```python
import functools
import math

import jax
import jax.numpy as jnp
import numpy as np
from jax import lax
from jax.experimental import pallas as pl
from jax.experimental.pallas import tpu as pltpu

F32 = jnp.float32
BF16 = jnp.bfloat16
HIGHEST = lax.Precision.HIGHEST

D_MODEL = 1024
D_CONV = 256
D_SSM = 256
SSM_GROUP = 16
SSM_GROUPS = 16
SSM_STATE = 64
N_STATE = SSM_GROUPS * SSM_STATE
HEAD_DIM = 64
D_ATTN = 512
N_Q_HEADS = 8
N_KV_HEADS = 2
AXIS_PAIRS = 16
ROPE_BASE = 10000.0
GRID_W = 64
BLOCK = 128
N_EXPERTS = 32
TOP_K = 4
D_EXPERT = 128
ROUTED_SCALE = 2.5
NORM_EPS = 1e-6
NEG_INF = -1e30
D_PROJ = 3 * D_CONV + D_SSM + D_ATTN + 2 * N_KV_HEADS * HEAD_DIM
LANES = 128
SUBLANES = 8
MXU_TILE = 256
VMEM_LIMIT = 56 * 1024 * 1024


def _cparams(n_axes):
    return pltpu.CompilerParams(dimension_semantics=("arbitrary",) * n_axes,
                                vmem_limit_bytes=VMEM_LIMIT)


def _const_spec(shape):
    nd = len(shape)
    return pl.BlockSpec(shape, lambda *_: (0,) * nd, pipeline_mode=pl.Buffered(1))


def _silu(x):
    return x * jax.nn.sigmoid(x)


def _rms_modulate(x, gain, shift, scale):
    ms = jnp.mean(x * x, axis=-1, keepdims=True)
    return (x * lax.rsqrt(ms + NORM_EPS) * gain) * (1.0 + scale) + shift


def _mod_kernel(c_ref, w_ref, b_ref, o_ref):
    cv = c_ref[...]
    o_ref[...] = jnp.dot(_silu(cv), w_ref[...], precision=HIGHEST,
                         preferred_element_type=F32) + b_ref[...]


def _modulation(cvec, w_mod, b_mod):
    depth, d, n = w_mod.shape
    rows = cvec.shape[0]
    tn = 1536
    return pl.pallas_call(
        _mod_kernel,
        out_shape=jax.ShapeDtypeStruct((depth, rows, n), F32),
        grid=(depth, n // tn),
        in_specs=[pl.BlockSpec((rows, d), lambda l, j: (0, 0)),
                  pl.BlockSpec((None, d, tn), lambda l, j: (l, 0, j)),
                  pl.BlockSpec((None, 1, tn), lambda l, j: (l, 0, j))],
        out_specs=pl.BlockSpec((None, rows, tn), lambda l, j: (l, 0, j)),
        compiler_params=_cparams(2),
        name="modulation",
    )(cvec, w_mod, b_mod.reshape(depth, 1, n))


HEAD_PAIR = 4 * HEAD_DIM
LOG2E = math.log2(math.e)
Q_SCALE = HEAD_DIM ** -0.5 * LOG2E


def _inproj_kernel(*refs, rope):
    if rope:
        (x_ref, mod_ref, g_ref, w_ref, qg_ref, kg_ref, ones_ref, perm_ref, cos_ref, sin_ref,
         cv_ref, cb_ref, u_ref, q_ref, kv_ref) = refs
    else:
        (x_ref, mod_ref, g_ref, w_ref, qg_ref, kg_ref, ones_ref, _,
         cv_ref, cb_ref, u_ref, q_ref, kv_ref) = refs
    mod = mod_ref[...]
    mb = _rms_modulate(x_ref[...], g_ref[...], mod[:, 0:D_MODEL], mod[:, D_MODEL:2 * D_MODEL]).astype(BF16)

    def proj(c0, width):
        return jnp.dot(mb, w_ref[:, c0:c0 + width], preferred_element_type=F32)

    cv_ref[...] = proj(2 * D_CONV, D_CONV) * proj(0, D_CONV)
    cb_ref[...] = proj(D_CONV, D_CONV)
    u_ref[...] = proj(3 * D_CONV, D_SSM)
    q0 = 3 * D_CONV + D_SSM
    kv0 = q0 + D_ATTN
    blocks = [(q0 + jj * HEAD_PAIR, qg_ref, q_ref, jj * HEAD_PAIR, Q_SCALE, False)
              for jj in range(D_ATTN // HEAD_PAIR)] + [(kv0, kg_ref, kv_ref, 0, 1.0, True)]
    is_k = lax.broadcasted_iota(jnp.int32, (1, HEAD_PAIR), 1) < N_KV_HEADS * HEAD_DIM
    xs = [proj(blk[0], HEAD_PAIR) for blk in blocks]
    ssqs = [jnp.dot((x * x).astype(BF16), ones_ref[...], preferred_element_type=F32) for x in xs]
    norms = [lax.rsqrt(ssq * (1.0 / HEAD_DIM) + NORM_EPS) for ssq in ssqs]
    norms = [jnp.where(is_k, nm, 1.0) if blk[5] else nm for nm, blk in zip(norms, blocks)]
    xns = [x * nm * blk[1][...] for x, nm, blk in zip(xs, norms, blocks)]
    if rope:
        partners = [jnp.dot(xn.astype(BF16), perm_ref[...], preferred_element_type=F32) for xn in xns]
        cos, sin = cos_ref[...], sin_ref[...]
        xns = [xn * (jnp.where(is_k, cos, 1.0) if blk[5] else cos) + pt * (jnp.where(is_k, sin, 0.0) if blk[5] else sin)
               for xn, pt, blk in zip(xns, partners, blocks)]
    for xn, (_, _, out_ref, o0, scale, _) in zip(xns, blocks):
        if scale != 1.0:
            xn = xn * scale
        out_ref[:, o0:o0 + HEAD_PAIR] = xn.astype(BF16)


def _in_projection(h, mod_l, mod_row, norm_g, w_in2, qg, kg, rope_tabs, tm, u_rows, u_all=None):
    bsz, t, d = h.shape
    rope = rope_tabs is not None
    row_map = lambda b, i: (b, i, 0)
    in_specs = [pl.BlockSpec((None, tm, d), row_map),
                pl.BlockSpec((None, 1, 6 * d), lambda b, i: (mod_row(b), 0, 0)),
                _const_spec((1, d)),
                _const_spec((d, D_PROJ)),
                _const_spec((1, HEAD_PAIR)),
                _const_spec((1, HEAD_PAIR)),
                _const_spec((HEAD_PAIR, HEAD_PAIR))]
    lane = jnp.arange(HEAD_PAIR)
    head_ones = (lane[:, None] // HEAD_DIM == lane[None, :] // HEAD_DIM).astype(BF16)
    args = [h, mod_l, norm_g, w_in2, qg, kg, head_ones]
    if rope:
        swap = (lane[:, None] == (lane[None, :] ^ AXIS_PAIRS)).astype(BF16)
        in_specs += [_const_spec((HEAD_PAIR, HEAD_PAIR))]
        in_specs += [pl.BlockSpec((tm, HEAD_PAIR), lambda b, i: (i, 0))] * 2
        args += [swap] + list(rope_tabs)
        u_off, aliases = 0, {}
    else:
        in_specs += [pl.BlockSpec(memory_space=pl.ANY)]
        args += [u_all]
        u_off, aliases = (u_rows - t) // tm, {len(args) - 1: 2}
    out_shape = (jax.ShapeDtypeStruct((bsz, t, D_CONV), F32),
                 jax.ShapeDtypeStruct((bsz, t, D_CONV), F32),
                 jax.ShapeDtypeStruct((bsz, u_rows, D_SSM), F32),
                 jax.ShapeDtypeStruct((bsz, t, D_ATTN), BF16),
                 jax.ShapeDtypeStruct((bsz, t, HEAD_PAIR), BF16))
    out_specs = (pl.BlockSpec((None, tm, D_CONV), row_map),
                 pl.BlockSpec((None, tm, D_CONV), row_map),
                 pl.BlockSpec((None, tm, D_SSM), lambda b, i: (b, u_off + i, 0)),
                 pl.BlockSpec((None, tm, D_ATTN), row_map),
                 pl.BlockSpec((None, tm, HEAD_PAIR), row_map))
    return pl.pallas_call(
        functools.partial(_inproj_kernel, rope=rope),
        out_shape=out_shape, grid=(bsz, t // tm), in_specs=in_specs, out_specs=out_specs,
        input_output_aliases=aliases,
        compiler_params=_cparams(2),
        name="in_projection_lat" if rope else "in_projection_ctx",
    )(*args)


def _scan_kernel(uf_in_ref, ur_in_ref, bmat_ref, lam_ref, cmat_ref, yf_out_ref, yr_out_ref,
                 uf_ref, ur_ref, yf_ref, yr_ref, sf0_ref, sf1_ref, sr0_ref, sr1_ref, hf_ref, hr_ref,
                 *, steps):
    j = pl.program_id(0)

    @pl.when(j == 0)
    def _():
        for ref in (sf0_ref, sf1_ref, sr0_ref, sr1_ref):
            ref[...] = jnp.zeros_like(ref)
        hf_ref[...] = jnp.zeros_like(hf_ref)
        hr_ref[...] = jnp.zeros_like(hr_ref)

    def lam_rows(r):
        return jnp.broadcast_to(lam_ref[r:r + 1, :], (SUBLANES, N_STATE))

    def phases(sf_cur, sr_cur, sf_prev, sr_prev):
        for t in range(steps):
            rows = slice(t * SUBLANES, (t + 1) * SUBLANES)
            uf_ref[rows, :] = uf_in_ref[:, t, :]
            ur_ref[rows, :] = ur_in_ref[:, t, :]
        yf_ref[...] = jnp.dot(sf_cur[...].astype(BF16), cmat_ref[0], preferred_element_type=F32)
        yr_ref[...] = jnp.dot(sr_cur[...].astype(BF16), cmat_ref[1], preferred_element_type=F32)
        sf_cur[...] = jnp.dot(uf_ref[...].astype(BF16), bmat_ref[0], preferred_element_type=F32)
        sr_cur[...] = jnp.dot(ur_ref[...].astype(BF16), bmat_ref[1], preferred_element_type=F32)

        lf_re, lf_im, lr_re, lr_im = lam_rows(0), lam_rows(1), lam_rows(2), lam_rows(3)

        def step(s_ref, t, l_re, l_im, h_re, h_im):
            rows = slice(t * SUBLANES, (t + 1) * SUBLANES)
            n_re = l_re * h_re - l_im * h_im + s_ref[rows, 0:N_STATE]
            n_im = l_re * h_im + l_im * h_re + s_ref[rows, N_STATE:2 * N_STATE]
            s_ref[rows, 0:N_STATE] = n_re
            s_ref[rows, N_STATE:2 * N_STATE] = n_im
            return n_re, n_im

        f_re, f_im = hf_ref[:, 0:N_STATE], hf_ref[:, N_STATE:2 * N_STATE]
        r_re, r_im = hr_ref[:, 0:N_STATE], hr_ref[:, N_STATE:2 * N_STATE]
        for t in range(steps):
            f_re, f_im = step(sf_prev, t, lf_re, lf_im, f_re, f_im)
        for t in range(steps):
            r_re, r_im = step(sr_prev, steps - 1 - t, lr_re, lr_im, r_re, r_im)
        hf_ref[:, 0:N_STATE] = f_re
        hf_ref[:, N_STATE:2 * N_STATE] = f_im
        hr_ref[:, 0:N_STATE] = r_re
        hr_ref[:, N_STATE:2 * N_STATE] = r_im

    @pl.when(lax.rem(j, 2) == 0)
    def _():
        phases(sf0_ref, sr0_ref, sf1_ref, sr1_ref)

    @pl.when(lax.rem(j, 2) == 1)
    def _():
        phases(sf1_ref, sr1_ref, sf0_ref, sr0_ref)

    @pl.when(j >= 2)
    def _():
        for t in range(steps):
            rows = slice(t * SUBLANES, (t + 1) * SUBLANES)
            yf_out_ref[:, t, :] = yf_ref[rows, :]
            yr_out_ref[:, t, :] = yr_ref[rows, :]


def _ssm_scan(u_all, n_lat, bmat, lamv, cmat, steps):
    bsz, total, _ = u_all.shape
    rows = steps * bsz
    n, nl = total // steps, n_lat // steps
    nc = n - nl
    blk = lambda f: pl.BlockSpec((bsz, steps, D_SSM), lambda j: (0, f(j), 0))
    fwd = lambda p: jnp.where(p < nc, nl + p, p - nc)
    rev = lambda p: n - 1 - p
    drive = lambda f: blk(lambda j: f(jnp.minimum(j, n - 1)))
    read = lambda f: blk(lambda j: f(jnp.maximum(j - 2, 0)))
    state = pltpu.VMEM((rows, 2 * N_STATE), F32)
    chunk = pltpu.VMEM((rows, D_SSM), F32)
    carry = pltpu.VMEM((SUBLANES, 2 * N_STATE), F32)
    return pl.pallas_call(
        functools.partial(_scan_kernel, steps=steps),
        out_shape=(jax.ShapeDtypeStruct(u_all.shape, F32),) * 2,
        grid=(n + 2,),
        in_specs=[drive(fwd), drive(rev),
                  _const_spec((2, D_SSM, 2 * N_STATE)),
                  _const_spec((4, N_STATE)),
                  _const_spec((2, 2 * N_STATE, D_SSM))],
        out_specs=(read(fwd), read(rev)),
        scratch_shapes=[chunk, chunk, chunk, chunk, state, state, state, state, carry, carry],
        compiler_params=_cparams(1),
        name="s5_scan",
    )(u_all, u_all, bmat, lamv, cmat)


def _attn_kernel(sink_ref, q_ref, *refs, window):
    if window:
        kvp_ref, kvo_ref, kvn_ref, kvc_ref, o_ref = refs
    else:
        kvc_ref, o_ref = refs
    n = pl.program_id(1)
    last = pl.num_programs(1) - 1
    rows = 2 * BLOCK
    lane = lax.broadcasted_iota(jnp.int32, (1, LANES), 1)
    lo = lane < HEAD_DIM
    qi = lax.broadcasted_iota(jnp.int32, (rows, BLOCK), 0) & (BLOCK - 1)
    kj = lax.broadcasted_iota(jnp.int32, (rows, BLOCK), 1)
    upper_head = lax.broadcasted_iota(jnp.int32, (rows, 1), 0) >= BLOCK
    zero = jnp.zeros((), BF16)
    nt = (((1,), (1,)), ((), ()))
    k_cols, v_cols = slice(0, LANES), slice(LANES, 2 * LANES)
    units = range(D_ATTN // LANES)
    qs = []
    for j in units:
        qp = q_ref[:, j * LANES:(j + 1) * LANES]
        qs.append(jnp.concatenate([jnp.where(lo, qp, zero), jnp.where(lo, zero, qp)], axis=0))
    s_all = []
    for j in units:
        scores = []
        if window:
            s = lax.dot_general(qs[j], kvp_ref[:, k_cols], nt, preferred_element_type=F32)
            scores.append(jnp.where((kj >= qi) & (n > 0), s, NEG_INF))
            scores.append(lax.dot_general(qs[j], kvo_ref[:, k_cols], nt, preferred_element_type=F32))
            s = lax.dot_general(qs[j], kvn_ref[:, k_cols], nt, preferred_element_type=F32)
            scores.append(jnp.where((kj <= qi) & (n < last), s, NEG_INF))
        scores.append(lax.dot_general(qs[j], kvc_ref[:, k_cols], nt, preferred_element_type=F32))
        s_all.append(jnp.concatenate(scores, axis=1))
    sinks = [jnp.where(upper_head, sink_ref[N_Q_HEADS // 2 + j], sink_ref[j]) * LOG2E for j in units]
    ms = [jnp.maximum(sinks[j], jnp.max(s_all[j], axis=-1, keepdims=True)) for j in units]
    es = [jnp.exp2(s_all[j] - ms[j]) for j in units]
    denoms = [jnp.exp2(sinks[j] - ms[j]) + jnp.sum(es[j], axis=-1, keepdims=True) for j in units]
    values = [kvp_ref[:, v_cols], kvo_ref[:, v_cols], kvn_ref[:, v_cols]] if window else []
    values = jnp.concatenate(values + [kvc_ref[:, v_cols]], axis=0)
    accs = [jnp.dot(es[j].astype(BF16), values, preferred_element_type=F32) for j in units]
    for j in units:
        out = accs[j] / denoms[j]
        o_ref[:, j * LANES:(j + 1) * LANES] = jnp.where(lo, out[0:BLOCK], out[BLOCK:2 * BLOCK]).astype(BF16)


def _attention(sink, q, kv, kv_ctx, window):
    bsz, t, _ = q.shape
    n_ctx = kv_ctx.shape[1]
    nb = t // BLOCK
    blk = lambda f: pl.BlockSpec((None, BLOCK, HEAD_PAIR), f)
    in_specs = [pl.BlockSpec(memory_space=pltpu.SMEM),
                pl.BlockSpec((None, BLOCK, D_ATTN), lambda b, n: (b, n, 0))]
    args = [sink, q]
    if window:
        prev = lambda b, n: (b, jnp.maximum(n - 1, 0), 0)
        own = lambda b, n: (b, n, 0)
        nxt = lambda b, n: (b, jnp.minimum(n + 1, nb - 1), 0)
        in_specs += [blk(prev), blk(own), blk(nxt)]
        args += [kv, kv, kv]
    in_specs += [pl.BlockSpec((None, n_ctx, HEAD_PAIR), lambda b, n: (b, 0, 0))]
    args += [kv_ctx]
    return pl.pallas_call(
        functools.partial(_attn_kernel, window=window),
        out_shape=jax.ShapeDtypeStruct((bsz, t, D_ATTN), BF16),
        grid=(bsz, nb), in_specs=in_specs,
        out_specs=pl.BlockSpec((None, BLOCK, D_ATTN), lambda b, n: (b, n, 0)),
        compiler_params=_cparams(2),
        name="window_attention" if window else "context_attention",
    )(*args)


def _mixout_kernel(h_ref, mod_ref, cv_ref, cvp_ref, cvn_ref, cb_ref, u_ref, yf_ref, yr_ref, att_ref,
                   convw_ref, d_ref, wglu_ref, bglu_ref, wout_ref, g2_ref, rw_ref, rb_ref,
                   hmid_ref, t_ref, gates_ref):
    i = pl.program_id(1)
    last = pl.num_programs(1) - 1
    tm = cv_ref.shape[0]
    cv = cv_ref[...]
    row = lax.broadcasted_iota(jnp.int32, (tm, 1), 0)
    before = jnp.where(i > 0, cvp_ref[SUBLANES - 1:SUBLANES, :], 0.0)
    after = jnp.where(i < last, cvn_ref[0:1, :], 0.0)
    down = jnp.where(row == 0, before, pltpu.roll(cv, 1, 0))
    up = jnp.where(row == tm - 1, after, pltpu.roll(cv, tm - 1, 0))
    w = convw_ref[...]
    y_conv = cb_ref[...] * (w[0:1, :] * down + w[1:2, :] * cv + w[2:3, :] * up)
    y = d_ref[...] * u_ref[...] + yf_ref[...] + yr_ref[...]
    gl = jax.nn.gelu(y)
    z = jnp.dot(gl.astype(BF16), wglu_ref[...], preferred_element_type=F32) + bglu_ref[...]
    y_ssm = gl * jax.nn.sigmoid(z)
    mix = (jnp.dot(y_conv.astype(BF16), wout_ref[0:D_CONV, :], preferred_element_type=F32)
           + jnp.dot(y_ssm.astype(BF16), wout_ref[D_CONV:D_CONV + D_SSM, :], preferred_element_type=F32)
           + jnp.dot(att_ref[...], wout_ref[D_CONV + D_SSM:, :], preferred_element_type=F32))
    mod = mod_ref[...]
    h_mid = h_ref[...] + mod[:, 2 * D_MODEL:3 * D_MODEL] * mix
    hmid_ref[...] = h_mid
    t = _rms_modulate(h_mid, g2_ref[...], mod[:, 3 * D_MODEL:4 * D_MODEL], mod[:, 4 * D_MODEL:5 * D_MODEL])
    t_ref[...] = t.astype(BF16)
    t_hi = t.astype(BF16)
    t_lo = (t - t_hi.astype(F32)).astype(BF16)
    nt = (((1,), (1,)), ((), ()))
    a = lax.dot_general(rw_ref[...], t_hi, nt, preferred_element_type=F32)
    b = lax.dot_general(rw_ref[...], t_lo, nt, preferred_element_type=F32)
    logits = a[0:N_EXPERTS] + a[N_EXPERTS:2 * N_EXPERTS] + b[0:N_EXPERTS]
    scores = jax.nn.sigmoid(logits)
    eidx = lax.broadcasted_iota(jnp.int32, (N_EXPERTS, 1), 0).astype(F32)
    biased = scores + rb_ref[...]
    sel = jnp.zeros(scores.shape, F32)
    for _ in range(TOP_K):
        best = jnp.max(biased, axis=0, keepdims=True)
        first = jnp.min(jnp.where(biased == best, eidx, float(N_EXPERTS)), axis=0, keepdims=True)
        pick = eidx == first
        sel = jnp.where(pick, scores, sel)
        biased = jnp.where(pick, -jnp.inf, biased)
    gates = sel / jnp.sum(sel, axis=0, keepdims=True) * ROUTED_SCALE
    padded = jnp.concatenate([gates, jnp.zeros((LANES - N_EXPERTS, gates.shape[1]), F32)], axis=0)
    gates_ref[...] = padded.T


def _mixer_output(h, mod_l, mod_row, cv, cb, u, yf, yr, u_row0, att,
                  conv_w, ssm_d, w_glu, b_glu, w_out, norm_g, router_w, router_b, tm):
    bsz, t, d = h.shape
    halo = tm // SUBLANES
    n_halo = t // SUBLANES
    row_map = lambda b, i: (b, i, 0)
    u_spec = pl.BlockSpec((None, tm, D_SSM), lambda b, i: (b, u_row0 // tm + i, 0))
    in_specs = [pl.BlockSpec((None, tm, d), row_map),
                pl.BlockSpec((None, 1, 6 * d), lambda b, i: (mod_row(b), 0, 0)),
                pl.BlockSpec((None, tm, D_CONV), row_map),
                pl.BlockSpec((None, SUBLANES, D_CONV), lambda b, i: (b, jnp.maximum(i * halo - 1, 0), 0)),
                pl.BlockSpec((None, SUBLANES, D_CONV), lambda b, i: (b, jnp.minimum((i + 1) * halo, n_halo - 1), 0)),
                pl.BlockSpec((None, tm, D_CONV), row_map),
                u_spec, u_spec, u_spec,
                pl.BlockSpec((None, tm, D_ATTN), row_map),
                _const_spec((3, D_CONV)),
                _const_spec((1, D_SSM)),
                _const_spec((D_SSM, D_SSM)),
                _const_spec((1, D_SSM)),
                _const_spec((d, d)),
                _const_spec((1, d)),
                _const_spec((LANES, d)),
                _const_spec((N_EXPERTS, 1))]
    out_shape = (jax.ShapeDtypeStruct((bsz, t, d), F32),
                 jax.ShapeDtypeStruct((bsz, t, d), BF16),
                 jax.ShapeDtypeStruct((bsz, t, LANES), F32))
    out_specs = (pl.BlockSpec((None, tm, d), row_map),
                 pl.BlockSpec((None, tm, d), row_map),
                 pl.BlockSpec((None, tm, LANES), row_map))
    return pl.pallas_call(
        _mixout_kernel, out_shape=out_shape, grid=(bsz, t // tm),
        in_specs=in_specs, out_specs=out_specs,
        compiler_params=_cparams(2),
        name="mixer_output",
    )(h, mod_l, cv, cv, cv, cb, u, yf, yr, att,
      conv_w, ssm_d, w_glu, b_glu, w_out, norm_g, router_w, router_b)


EXPERTS_PER_CHUNK = 4


def _moe_kernel(t_ref, gates_ref, hmid_ref, mod_ref, wg_ref, wu_ref, wd_ref,
                wsg_ref, wsu_ref, wsd_ref, o_ref, acc_ref):
    t = t_ref[...]
    gates = gates_ref[...]
    hs = _silu(jnp.dot(t, wsg_ref[...], preferred_element_type=F32)) * jnp.dot(
        t, wsu_ref[...], preferred_element_type=F32)
    acc_ref[...] = jnp.dot(hs.astype(BF16), wsd_ref[...], preferred_element_type=F32)
    width = EXPERTS_PER_CHUNK * D_EXPERT
    n_chunks = N_EXPERTS // EXPERTS_PER_CHUNK

    def up(c):
        cols = slice(c * width, (c + 1) * width)
        return (jnp.dot(t, wg_ref[:, cols], preferred_element_type=F32),
                jnp.dot(t, wu_ref[:, cols], preferred_element_type=F32))

    nxt = up(0)
    for c in range(n_chunks):
        hg, hu = nxt
        if c + 1 < n_chunks:
            nxt = up(c + 1)
        hid = _silu(hg) * hu
        gated = []
        for e in range(EXPERTS_PER_CHUNK):
            k = c * EXPERTS_PER_CHUNK + e
            gated.append((hid[:, e * D_EXPERT:(e + 1) * D_EXPERT] * gates[:, k:k + 1]).astype(BF16))
        acc_ref[...] += jnp.dot(jnp.concatenate(gated, axis=1), wd_ref[c * width:(c + 1) * width, :],
                                preferred_element_type=F32)
    mod = mod_ref[...]
    o_ref[...] = hmid_ref[...] + mod[:, 5 * D_MODEL:6 * D_MODEL] * acc_ref[...]


def _moe(t, gates, hmid, mod_l, mod_row, wg, wu, wd, wsg, wsu, wsd, tm):
    bsz, n, d = hmid.shape
    row_map = lambda b, i: (b, i, 0)
    return pl.pallas_call(
        _moe_kernel,
        out_shape=jax.ShapeDtypeStruct((bsz, n, d), F32),
        grid=(bsz, n // tm),
        in_specs=[pl.BlockSpec((None, tm, d), row_map),
                  pl.BlockSpec((None, tm, LANES), row_map),
                  pl.BlockSpec((None, tm, d), row_map),
                  pl.BlockSpec((None, 1, 6 * d), lambda b, i: (mod_row(b), 0, 0)),
                  _const_spec(wg.shape), _const_spec(wu.shape), _const_spec(wd.shape),
                  _const_spec(wsg.shape), _const_spec(wsu.shape), _const_spec(wsd.shape)],
        out_specs=pl.BlockSpec((None, tm, d), row_map),
        scratch_shapes=[pltpu.VMEM((tm, d), F32)],
        compiler_params=_cparams(2),
        name="moe",
    )(t, gates, hmid, mod_l, wg, wu, wd, wsg, wsu, wsd)


def _rope_tables(n_tokens):
    rows = n_tokens // GRID_W
    row = np.repeat(np.arange(rows, dtype=np.float64), GRID_W)
    col = np.tile(np.arange(GRID_W, dtype=np.float64), rows)
    inv_freq = ROPE_BASE ** (-np.arange(AXIS_PAIRS, dtype=np.float64) / AXIS_PAIRS)
    ar, ac = row[:, None] * inv_freq, col[:, None] * inv_freq
    cos = np.concatenate([np.cos(ar), np.cos(ar), np.cos(ac), np.cos(ac)], axis=1).astype(np.float32)
    sin = np.concatenate([-np.sin(ar), np.sin(ar), -np.sin(ac), np.sin(ac)], axis=1).astype(np.float32)
    reps = (1, HEAD_PAIR // HEAD_DIM)
    return jnp.tile(jnp.asarray(cos), reps), jnp.tile(jnp.asarray(sin), reps)


def _ssm_params(lam_re, lam_im, log_dt, b_re, b_im, c_re, c_im):
    lead = lam_re.shape[:2]
    lr, li = lam_re.astype(F32), lam_im.astype(F32)
    dt = jnp.exp(log_dt.astype(F32))[..., None]
    mag = jnp.exp(lr * dt)
    ar, ai = mag * jnp.cos(li * dt), mag * jnp.sin(li * dt)
    den = lr * lr + li * li
    qr = ((ar - 1.0) * lr + ai * li) / den
    qi = (ai * lr - (ar - 1.0) * li) / den
    br, bi = b_re.astype(F32), b_im.astype(F32)
    bbar_re = qr[..., None] * br - qi[..., None] * bi
    bbar_im = qr[..., None] * bi + qi[..., None] * br
    eye = jnp.eye(SSM_GROUPS, dtype=F32)
    drive = lambda m: jnp.einsum('gk,ldgph->ldghkp', eye, m).reshape(lead + (D_SSM, N_STATE))
    read = lambda m: jnp.einsum('gk,ldghp->ldgpkh', eye, m.astype(F32)).reshape(lead + (N_STATE, D_SSM))
    bmat = jnp.concatenate([drive(bbar_re), drive(bbar_im)], axis=-1)
    cmat = jnp.concatenate([read(c_re), -read(c_im)], axis=-2)
    lamv = jnp.stack([ar, ai], axis=2).reshape(lead[0], 4, N_STATE)
    return bmat.astype(BF16), lamv, cmat.astype(BF16)


def _pair_heads(w, axis):
    shape = w.shape
    split = shape[:axis] + (N_KV_HEADS, N_Q_HEADS // N_KV_HEADS, HEAD_DIM) + shape[axis + 1:]
    return jnp.swapaxes(w.reshape(split), axis, axis + 1).reshape(shape)


def kernel(x, c, ctx, c_ctx, w_mod, b_mod, norm1_g, norm2_g, w_in, conv_w, ssm_lam_re, ssm_lam_im, ssm_log_dt, ssm_b_re, ssm_b_im, ssm_c_re, ssm_c_im, ssm_d, w_glu, b_glu, q_norm_g, k_norm_g, attn_sink, w_out, router_w, router_bias, w_exp_gate, w_exp_up, w_exp_down, w_sh_gate, w_sh_up, w_sh_down):
    bsz, n_lat, d = x.shape
    n_ctx = ctx.shape[1]
    depth = w_mod.shape[0]
    assert bsz == SUBLANES and d == D_MODEL

    mod_rows = 2 * SUBLANES
    cvec = jnp.zeros((mod_rows, d), F32).at[:bsz].set(c).at[bsz].set(c_ctx)
    mod = _modulation(cvec, w_mod, b_mod).reshape(depth, mod_rows, 1, 6 * d)
    lat_row = lambda b: b
    ctx_row = lambda b: bsz
    rope_tabs = _rope_tables(n_lat)

    tm_lat, tm_ctx, scan_steps = 512, 256, 64
    u_rows = n_lat + n_ctx

    q0, kv0 = 3 * D_CONV + D_SSM, 3 * D_CONV + D_SSM + D_ATTN
    w_in2 = jnp.concatenate([w_in[:, :, :q0], _pair_heads(w_in[:, :, q0:kv0], 2), w_in[:, :, kv0:]],
                            axis=2).astype(BF16)
    qg = jnp.tile(q_norm_g.astype(F32), (1, HEAD_PAIR // HEAD_DIM))[:, None, :]
    kg = jnp.concatenate([jnp.tile(k_norm_g.astype(F32), (1, N_KV_HEADS)),
                          jnp.ones((depth, N_KV_HEADS * HEAD_DIM), F32)], axis=1)[:, None, :]
    bmat, lamv, cmat = _ssm_params(ssm_lam_re, ssm_lam_im, ssm_log_dt, ssm_b_re, ssm_b_im, ssm_c_re, ssm_c_im)
    w_out2 = jnp.concatenate([w_out[:, :D_CONV + D_SSM], _pair_heads(w_out[:, D_CONV + D_SSM:], 1)],
                             axis=1).astype(BF16)
    rw = router_w.astype(F32)
    rw_hi = rw.astype(BF16)
    rw_lo = (rw - rw_hi.astype(F32)).astype(BF16)
    rw2 = jnp.concatenate([jnp.swapaxes(rw_hi, 1, 2), jnp.swapaxes(rw_lo, 1, 2),
                           jnp.zeros((depth, LANES - 2 * N_EXPERTS, d), BF16)], axis=1)
    w_glu2 = w_glu.astype(BF16)
    experts_all = (w_exp_gate.astype(BF16).reshape(depth, d, -1), w_exp_up.astype(BF16).reshape(depth, d, -1),
                   w_exp_down.astype(BF16).reshape(depth, -1, d), w_sh_gate.astype(BF16),
                   w_sh_up.astype(BF16), w_sh_down.astype(BF16))

    h, hc = x, ctx
    for l in range(depth):
        ctx_out = l < depth - 1
        mod_l = mod[l]
        n1 = norm1_g[l][None, :]
        cv, cb, u_all, q, kv = _in_projection(h, mod_l, lat_row, n1, w_in2[l], qg[l], kg[l], rope_tabs,
                                              tm_lat, u_rows)
        cvc, cbc, u_all, qc, kvc = _in_projection(hc, mod_l, ctx_row, n1, w_in2[l], qg[l], kg[l], None,
                                                  tm_ctx, u_rows, u_all)
        yf, yr = _ssm_scan(u_all, n_lat, bmat[l], lamv[l], cmat[l], scan_steps)

        sink = attn_sink[l].astype(F32)
        att = _attention(sink, q, kv, kvc, window=True)

        post = dict(conv_w=conv_w[l], ssm_d=ssm_d[l][None, :], w_glu=w_glu2[l], b_glu=b_glu[l][None, :],
                    w_out=w_out2[l], norm_g=norm2_g[l][None, :], router_w=rw2[l],
                    router_b=router_bias[l].astype(F32)[:, None])
        experts = tuple(w[l] for w in experts_all)
        hmid, t, gates = _mixer_output(h, mod_l, lat_row, cv, cb, u_all, yf, yr, 0, att, tm=tm_lat, **post)
        h_new = _moe(t, gates, hmid, mod_l, lat_row, *experts, tm=tm_lat)
        if ctx_out:
            attc = _attention(sink, qc, None, kvc, window=False)
            hmid_c, tc, gates_c = _mixer_output(hc, mod_l, ctx_row, cvc, cbc, u_all, yf, yr, n_lat, attc,
                                                tm=tm_ctx, **post)
            flat = lambda a: a.reshape(1, bsz * n_ctx, a.shape[-1])
            hc = _moe(flat(tc), flat(gates_c), flat(hmid_c), mod_l, ctx_row, *experts,
                      tm=tm_lat).reshape(hc.shape)
        h = h_new
    return h
```

```python
import functools
import math

import jax
import jax.numpy as jnp
import numpy as np
from jax import lax
from jax.experimental import pallas as pl
from jax.experimental.pallas import tpu as pltpu

F32 = jnp.float32
BF16 = jnp.bfloat16
HIGHEST = lax.Precision.HIGHEST

D_MODEL = 1024
D_CONV = 256
D_SSM = 256
SSM_GROUP = 16
SSM_GROUPS = 16
SSM_STATE = 64
N_STATE = SSM_GROUPS * SSM_STATE
HEAD_DIM = 64
D_ATTN = 512
N_Q_HEADS = 8
N_KV_HEADS = 2
AXIS_PAIRS = 16
ROPE_BASE = 10000.0
GRID_W = 64
BLOCK = 128
N_EXPERTS = 32
TOP_K = 4
D_EXPERT = 128
ROUTED_SCALE = 2.5
NORM_EPS = 1e-6
NEG_INF = -1e30
D_PROJ = 3 * D_CONV + D_SSM + D_ATTN + 2 * N_KV_HEADS * HEAD_DIM
LANES = 128
SUBLANES = 8
MXU_TILE = 256
VMEM_LIMIT = 56 * 1024 * 1024


def _cparams(n_axes):
    return pltpu.CompilerParams(dimension_semantics=("arbitrary",) * n_axes,
                                vmem_limit_bytes=VMEM_LIMIT)


def _const_spec(shape):
    nd = len(shape)
    return pl.BlockSpec(shape, lambda *_: (0,) * nd, pipeline_mode=pl.Buffered(1))


def _silu(x):
    return x * jax.nn.sigmoid(x)


def _rms_modulate(x, gain, shift, scale):
    ms = jnp.mean(x * x, axis=-1, keepdims=True)
    return (x * lax.rsqrt(ms + NORM_EPS) * gain) * (1.0 + scale) + shift


def _mod_kernel(c_ref, w_ref, b_ref, o_ref):
    cv = c_ref[...]
    o_ref[...] = jnp.dot(_silu(cv), w_ref[...], precision=HIGHEST,
                         preferred_element_type=F32) + b_ref[...]


def _modulation(cvec, w_mod, b_mod):
    depth, d, n = w_mod.shape
    rows = cvec.shape[0]
    tn = 1536
    return pl.pallas_call(
        _mod_kernel,
        out_shape=jax.ShapeDtypeStruct((depth, rows, n), F32),
        grid=(depth, n // tn),
        in_specs=[pl.BlockSpec((rows, d), lambda l, j: (0, 0)),
                  pl.BlockSpec((None, d, tn), lambda l, j: (l, 0, j)),
                  pl.BlockSpec((None, 1, tn), lambda l, j: (l, 0, j))],
        out_specs=pl.BlockSpec((None, rows, tn), lambda l, j: (l, 0, j)),
        compiler_params=_cparams(2),
        name="modulation",
    )(cvec, w_mod, b_mod.reshape(depth, 1, n))


HEAD_PAIR = 4 * HEAD_DIM
LOG2E = math.log2(math.e)
Q_SCALE = HEAD_DIM ** -0.5 * LOG2E


def _inproj_kernel(*refs, rope):
    if rope:
        (x_ref, mod_ref, g_ref, w_ref, qg_ref, kg_ref, ones_ref, perm_ref, cos_ref, sin_ref,
         cv_ref, cb_ref, u_ref, q_ref, kv_ref) = refs
    else:
        (x_ref, mod_ref, g_ref, w_ref, qg_ref, kg_ref, ones_ref,
         cv_ref, cb_ref, u_ref, q_ref, kv_ref) = refs
    mod = mod_ref[...]
    mb = _rms_modulate(x_ref[...], g_ref[...], mod[:, 0:D_MODEL], mod[:, D_MODEL:2 * D_MODEL]).astype(BF16)

    def proj(c0, width):
        return jnp.dot(mb, w_ref[:, c0:c0 + width], preferred_element_type=F32)

    cv_ref[...] = proj(2 * D_CONV, D_CONV) * proj(0, D_CONV)
    cb_ref[...] = proj(D_CONV, D_CONV)
    u_ref[...] = proj(3 * D_CONV, D_SSM)
    q0 = 3 * D_CONV + D_SSM
    kv0 = q0 + D_ATTN
    blocks = [(q0 + jj * HEAD_PAIR, qg_ref, q_ref, jj * HEAD_PAIR, Q_SCALE, False)
              for jj in range(D_ATTN // HEAD_PAIR)] + [(kv0, kg_ref, kv_ref, 0, 1.0, True)]
    is_k = lax.broadcasted_iota(jnp.int32, (1, HEAD_PAIR), 1) < N_KV_HEADS * HEAD_DIM
    xs = [proj(blk[0], HEAD_PAIR) for blk in blocks]
    ssqs = [jnp.dot((x * x).astype(BF16), ones_ref[...], preferred_element_type=F32) for x in xs]
    norms = [lax.rsqrt(ssq * (1.0 / HEAD_DIM) + NORM_EPS) for ssq in ssqs]
    norms = [jnp.where(is_k, nm, 1.0) if blk[5] else nm for nm, blk in zip(norms, blocks)]
    xns = [x * nm * blk[1][...] for x, nm, blk in zip(xs, norms, blocks)]
    if rope:
        partners = [jnp.dot(xn.astype(BF16), perm_ref[...], preferred_element_type=F32) for xn in xns]
        cos, sin = cos_ref[...], sin_ref[...]
        xns = [xn * (jnp.where(is_k, cos, 1.0) if blk[5] else cos) + pt * (jnp.where(is_k, sin, 0.0) if blk[5] else sin)
               for xn, pt, blk in zip(xns, partners, blocks)]
    for xn, (_, _, out_ref, o0, scale, _) in zip(xns, blocks):
        if scale != 1.0:
            xn = xn * scale
        out_ref[:, o0:o0 + HEAD_PAIR] = xn.astype(BF16)


def _in_projection(h, mod_l, mod_row, norm_g, w_in2, qg, kg, rope_tabs, tm):
    bsz, t, d = h.shape
    rope = rope_tabs is not None
    row_map = lambda b, i: (b, i, 0)
    in_specs = [pl.BlockSpec((None, tm, d), row_map),
                pl.BlockSpec((None, 1, 6 * d), lambda b, i: (mod_row(b), 0, 0)),
                _const_spec((1, d)),
                _const_spec((d, D_PROJ)),
                _const_spec((1, HEAD_PAIR)),
                _const_spec((1, HEAD_PAIR)),
                _const_spec((HEAD_PAIR, HEAD_PAIR))]
    lane = jnp.arange(HEAD_PAIR)
    head_ones = (lane[:, None] // HEAD_DIM == lane[None, :] // HEAD_DIM).astype(BF16)
    args = [h, mod_l, norm_g, w_in2, qg, kg, head_ones]
    if rope:
        swap = (lane[:, None] == (lane[None, :] ^ AXIS_PAIRS)).astype(BF16)
        in_specs += [_const_spec((HEAD_PAIR, HEAD_PAIR))]
        in_specs += [pl.BlockSpec((tm, HEAD_PAIR), lambda b, i: (i, 0))] * 2
        args += [swap] + list(rope_tabs)
    out_shape = (jax.ShapeDtypeStruct((bsz, t, D_CONV), F32),
                 jax.ShapeDtypeStruct((bsz, t, D_CONV), F32),
                 jax.ShapeDtypeStruct((bsz, t, D_SSM), F32),
                 jax.ShapeDtypeStruct((bsz, t, D_ATTN), BF16),
                 jax.ShapeDtypeStruct((bsz, t, HEAD_PAIR), BF16))
    out_specs = (pl.BlockSpec((None, tm, D_CONV), row_map),
                 pl.BlockSpec((None, tm, D_CONV), row_map),
                 pl.BlockSpec((None, tm, D_SSM), row_map),
                 pl.BlockSpec((None, tm, D_ATTN), row_map),
                 pl.BlockSpec((None, tm, HEAD_PAIR), row_map))
    return pl.pallas_call(
        functools.partial(_inproj_kernel, rope=rope),
        out_shape=out_shape, grid=(bsz, t // tm), in_specs=in_specs, out_specs=out_specs,
        compiler_params=_cparams(2),
        name="in_projection_lat" if rope else "in_projection_ctx",
    )(*args)


def _scan_kernel(ulf_ref, ucf_ref, ulr_ref, ucr_ref, bmat_ref, lam_ref, cmat_ref,
                 yfl_ref, yrl_ref, yfc_ref, yrc_ref,
                 uf_ref, ur_ref, yf_ref, yr_ref, sf0_ref, sf1_ref, sr0_ref, sr1_ref, hf_ref, hr_ref,
                 *, steps, ctx_chunks, n_chunks):
    j = pl.program_id(0)

    def gather(f_in, r_in):
        for t in range(steps):
            rows = slice(t * SUBLANES, (t + 1) * SUBLANES)
            uf_ref[rows, :] = f_in[:, t, :]
            ur_ref[rows, :] = r_in[:, t, :]

    drive_ctx = jnp.minimum(j, n_chunks - 1) < ctx_chunks

    @pl.when(drive_ctx)
    def _():
        gather(ucf_ref, ucr_ref)

    @pl.when(jnp.logical_not(drive_ctx))
    def _():
        gather(ulf_ref, ulr_ref)

    @pl.when(j == 0)
    def _():
        for ref in (sf0_ref, sf1_ref, sr0_ref, sr1_ref):
            ref[...] = jnp.zeros_like(ref)
        hf_ref[...] = jnp.zeros_like(hf_ref)
        hr_ref[...] = jnp.zeros_like(hr_ref)

    def lam_rows(r):
        return jnp.broadcast_to(lam_ref[r:r + 1, :], (SUBLANES, N_STATE))

    def phases(sf_cur, sr_cur, sf_prev, sr_prev):
        yf_ref[...] = jnp.dot(sf_cur[...].astype(BF16), cmat_ref[0], preferred_element_type=F32)
        yr_ref[...] = jnp.dot(sr_cur[...].astype(BF16), cmat_ref[1], preferred_element_type=F32)
        sf_cur[...] = jnp.dot(uf_ref[...].astype(BF16), bmat_ref[0], preferred_element_type=F32)
        sr_cur[...] = jnp.dot(ur_ref[...].astype(BF16), bmat_ref[1], preferred_element_type=F32)

        lf_re, lf_im, lr_re, lr_im = lam_rows(0), lam_rows(1), lam_rows(2), lam_rows(3)

        def step(s_ref, t, l_re, l_im, h_re, h_im):
            rows = slice(t * SUBLANES, (t + 1) * SUBLANES)
            n_re = l_re * h_re - l_im * h_im + s_ref[rows, 0:N_STATE]
            n_im = l_re * h_im + l_im * h_re + s_ref[rows, N_STATE:2 * N_STATE]
            s_ref[rows, 0:N_STATE] = n_re
            s_ref[rows, N_STATE:2 * N_STATE] = n_im
            return n_re, n_im

        f_re, f_im = hf_ref[:, 0:N_STATE], hf_ref[:, N_STATE:2 * N_STATE]
        r_re, r_im = hr_ref[:, 0:N_STATE], hr_ref[:, N_STATE:2 * N_STATE]
        for t in range(steps):
            f_re, f_im = step(sf_prev, t, lf_re, lf_im, f_re, f_im)
        for t in range(steps):
            r_re, r_im = step(sr_prev, steps - 1 - t, lr_re, lr_im, r_re, r_im)
        hf_ref[:, 0:N_STATE] = f_re
        hf_ref[:, N_STATE:2 * N_STATE] = f_im
        hr_ref[:, 0:N_STATE] = r_re
        hr_ref[:, N_STATE:2 * N_STATE] = r_im

    @pl.when(lax.rem(j, 2) == 0)
    def _():
        phases(sf0_ref, sr0_ref, sf1_ref, sr1_ref)

    @pl.when(lax.rem(j, 2) == 1)
    def _():
        phases(sf1_ref, sr1_ref, sf0_ref, sr0_ref)

    def write_out(yf_out, yr_out):
        for t in range(steps):
            rows = slice(t * SUBLANES, (t + 1) * SUBLANES)
            yf_out[:, t, :] = yf_ref[rows, :]
            yr_out[:, t, :] = yr_ref[rows, :]

    @pl.when(jnp.logical_and(j >= 2, j - 2 < ctx_chunks))
    def _():
        write_out(yfc_ref, yrc_ref)

    @pl.when(j - 2 >= ctx_chunks)
    def _():
        write_out(yfl_ref, yrl_ref)


def _ssm_scan(u_lat, u_ctx, bmat, lamv, cmat, steps):
    bsz = u_lat.shape[0]
    rows = steps * bsz
    nl, nc = u_lat.shape[1] // steps, u_ctx.shape[1] // steps
    n = nl + nc
    blk = lambda f: pl.BlockSpec((bsz, steps, D_SSM), lambda j: (0, f(j), 0))
    lat_f = lambda p: jnp.maximum(p - nc, 0)
    ctx_f = lambda p: jnp.minimum(p, nc - 1)
    lat_r = lambda p: jnp.minimum(n - 1 - p, nl - 1)
    ctx_r = lambda p: jnp.maximum(nc - 1 - p, 0)
    drive = lambda f: blk(lambda j: f(jnp.minimum(j, n - 1)))
    read = lambda f: blk(lambda j: f(jnp.maximum(j - 2, 0)))
    state = pltpu.VMEM((rows, 2 * N_STATE), F32)
    chunk = pltpu.VMEM((rows, D_SSM), F32)
    carry = pltpu.VMEM((SUBLANES, 2 * N_STATE), F32)
    return pl.pallas_call(
        functools.partial(_scan_kernel, steps=steps, ctx_chunks=nc, n_chunks=n),
        out_shape=(jax.ShapeDtypeStruct(u_lat.shape, F32),) * 2 + (jax.ShapeDtypeStruct(u_ctx.shape, F32),) * 2,
        grid=(n + 2,),
        in_specs=[drive(lat_f), drive(ctx_f), drive(lat_r), drive(ctx_r),
                  _const_spec((2, D_SSM, 2 * N_STATE)),
                  _const_spec((4, N_STATE)),
                  _const_spec((2, 2 * N_STATE, D_SSM))],
        out_specs=(read(lat_f), read(lat_r), read(ctx_f), read(ctx_r)),
        scratch_shapes=[chunk, chunk, chunk, chunk, state, state, state, state, carry, carry],
        compiler_params=_cparams(1),
        name="s5_scan",
    )(u_lat, u_ctx, u_lat, u_ctx, bmat, lamv, cmat)


def _attn_kernel(sink_ref, q_ref, *refs, window):
    if window:
        kvp_ref, kvo_ref, kvn_ref, kvc_ref, o_ref = refs
    else:
        kvc_ref, o_ref = refs
    n = pl.program_id(1)
    last = pl.num_programs(1) - 1
    rows = 2 * BLOCK
    lane = lax.broadcasted_iota(jnp.int32, (1, LANES), 1)
    lo = lane < HEAD_DIM
    qi = lax.broadcasted_iota(jnp.int32, (rows, BLOCK), 0) & (BLOCK - 1)
    kj = lax.broadcasted_iota(jnp.int32, (rows, BLOCK), 1)
    upper_head = lax.broadcasted_iota(jnp.int32, (rows, 1), 0) >= BLOCK
    zero = jnp.zeros((), BF16)
    nt = (((1,), (1,)), ((), ()))
    k_cols, v_cols = slice(0, LANES), slice(LANES, 2 * LANES)
    units = range(D_ATTN // LANES)
    qs = []
    for j in units:
        qp = q_ref[:, j * LANES:(j + 1) * LANES]
        qs.append(jnp.concatenate([jnp.where(lo, qp, zero), jnp.where(lo, zero, qp)], axis=0))
    s_all = []
    for j in units:
        scores = []
        if window:
            s = lax.dot_general(qs[j], kvp_ref[:, k_cols], nt, preferred_element_type=F32)
            scores.append(jnp.where((kj >= qi) & (n > 0), s, NEG_INF))
            scores.append(lax.dot_general(qs[j], kvo_ref[:, k_cols], nt, preferred_element_type=F32))
            s = lax.dot_general(qs[j], kvn_ref[:, k_cols], nt, preferred_element_type=F32)
            scores.append(jnp.where((kj <= qi) & (n < last), s, NEG_INF))
        scores.append(lax.dot_general(qs[j], kvc_ref[:, k_cols], nt, preferred_element_type=F32))
        s_all.append(jnp.concatenate(scores, axis=1))
    sinks = [jnp.where(upper_head, sink_ref[N_Q_HEADS // 2 + j], sink_ref[j]) * LOG2E for j in units]
    ms = [jnp.maximum(sinks[j], jnp.max(s_all[j], axis=-1, keepdims=True)) for j in units]
    es = [jnp.exp2(s_all[j] - ms[j]) for j in units]
    denoms = [jnp.exp2(sinks[j] - ms[j]) + jnp.sum(es[j], axis=-1, keepdims=True) for j in units]
    values = [kvp_ref[:, v_cols], kvo_ref[:, v_cols], kvn_ref[:, v_cols]] if window else []
    values = jnp.concatenate(values + [kvc_ref[:, v_cols]], axis=0)
    accs = [jnp.dot(es[j].astype(BF16), values, preferred_element_type=F32) for j in units]
    for j in units:
        out = accs[j] / denoms[j]
        o_ref[:, j * LANES:(j + 1) * LANES] = jnp.where(lo, out[0:BLOCK], out[BLOCK:2 * BLOCK]).astype(BF16)


def _attention(sink, q, kv, kv_ctx, window):
    bsz, t, _ = q.shape
    n_ctx = kv_ctx.shape[1]
    nb = t // BLOCK
    blk = lambda f: pl.BlockSpec((None, BLOCK, HEAD_PAIR), f)
    in_specs = [pl.BlockSpec(memory_space=pltpu.SMEM),
                pl.BlockSpec((None, BLOCK, D_ATTN), lambda b, n: (b, n, 0))]
    args = [sink, q]
    if window:
        prev = lambda b, n: (b, jnp.maximum(n - 1, 0), 0)
        own = lambda b, n: (b, n, 0)
        nxt = lambda b, n: (b, jnp.minimum(n + 1, nb - 1), 0)
        in_specs += [blk(prev), blk(own), blk(nxt)]
        args += [kv, kv, kv]
    in_specs += [pl.BlockSpec((None, n_ctx, HEAD_PAIR), lambda b, n: (b, 0, 0))]
    args += [kv_ctx]
    return pl.pallas_call(
        functools.partial(_attn_kernel, window=window),
        out_shape=jax.ShapeDtypeStruct((bsz, t, D_ATTN), BF16),
        grid=(bsz, nb), in_specs=in_specs,
        out_specs=pl.BlockSpec((None, BLOCK, D_ATTN), lambda b, n: (b, n, 0)),
        compiler_params=_cparams(2),
        name="window_attention" if window else "context_attention",
    )(*args)


def _mixout_kernel(h_ref, mod_ref, cv_ref, cvp_ref, cvn_ref, cb_ref, u_ref, yf_ref, yr_ref, att_ref,
                   convw_ref, d_ref, wglu_ref, bglu_ref, wout_ref, g2_ref, rw_ref, rb_ref,
                   hmid_ref, t_ref, gates_ref):
    i = pl.program_id(1)
    last = pl.num_programs(1) - 1
    tm = cv_ref.shape[0]
    cv = cv_ref[...]
    row = lax.broadcasted_iota(jnp.int32, (tm, 1), 0)
    before = jnp.where(i > 0, cvp_ref[SUBLANES - 1:SUBLANES, :], 0.0)
    after = jnp.where(i < last, cvn_ref[0:1, :], 0.0)
    down = jnp.where(row == 0, before, pltpu.roll(cv, 1, 0))
    up = jnp.where(row == tm - 1, after, pltpu.roll(cv, tm - 1, 0))
    w = convw_ref[...]
    y_conv = cb_ref[...] * (w[0:1, :] * down + w[1:2, :] * cv + w[2:3, :] * up)
    y = d_ref[...] * u_ref[...] + yf_ref[...] + yr_ref[...]
    gl = jax.nn.gelu(y)
    z = jnp.dot(gl.astype(BF16), wglu_ref[...], preferred_element_type=F32) + bglu_ref[...]
    y_ssm = gl * jax.nn.sigmoid(z)
    mix = (jnp.dot(y_conv.astype(BF16), wout_ref[0:D_CONV, :], preferred_element_type=F32)
           + jnp.dot(y_ssm.astype(BF16), wout_ref[D_CONV:D_CONV + D_SSM, :], preferred_element_type=F32)
           + jnp.dot(att_ref[...], wout_ref[D_CONV + D_SSM:, :], preferred_element_type=F32))
    mod = mod_ref[...]
    h_mid = h_ref[...] + mod[:, 2 * D_MODEL:3 * D_MODEL] * mix
    hmid_ref[...] = h_mid
    t = _rms_modulate(h_mid, g2_ref[...], mod[:, 3 * D_MODEL:4 * D_MODEL], mod[:, 4 * D_MODEL:5 * D_MODEL])
    t_ref[...] = t.astype(BF16)
    t_hi = t.astype(BF16)
    t_lo = (t - t_hi.astype(F32)).astype(BF16)
    nt = (((1,), (1,)), ((), ()))
    a = lax.dot_general(rw_ref[...], t_hi, nt, preferred_element_type=F32)
    b = lax.dot_general(rw_ref[...], t_lo, nt, preferred_element_type=F32)
    logits = a[0:N_EXPERTS] + a[N_EXPERTS:2 * N_EXPERTS] + b[0:N_EXPERTS]
    scores = jax.nn.sigmoid(logits)
    eidx = lax.broadcasted_iota(jnp.int32, (N_EXPERTS, 1), 0).astype(F32)
    biased = scores + rb_ref[...]
    sel = jnp.zeros(scores.shape, F32)
    for _ in range(TOP_K):
        best = jnp.max(biased, axis=0, keepdims=True)
        first = jnp.min(jnp.where(biased == best, eidx, float(N_EXPERTS)), axis=0, keepdims=True)
        pick = eidx == first
        sel = jnp.where(pick, scores, sel)
        biased = jnp.where(pick, -jnp.inf, biased)
    gates = sel / jnp.sum(sel, axis=0, keepdims=True) * ROUTED_SCALE
    padded = jnp.concatenate([gates, jnp.zeros((LANES - N_EXPERTS, gates.shape[1]), F32)], axis=0)
    gates_ref[...] = padded.T


def _mixer_output(h, mod_l, mod_row, cv, cb, u, yf, yr, att,
                  conv_w, ssm_d, w_glu, b_glu, w_out, norm_g, router_w, router_b, tm):
    bsz, t, d = h.shape
    halo = tm // SUBLANES
    n_halo = t // SUBLANES
    row_map = lambda b, i: (b, i, 0)
    u_spec = pl.BlockSpec((None, tm, D_SSM), row_map)
    in_specs = [pl.BlockSpec((None, tm, d), row_map),
                pl.BlockSpec((None, 1, 6 * d), lambda b, i: (mod_row(b), 0, 0)),
                pl.BlockSpec((None, tm, D_CONV), row_map),
                pl.BlockSpec((None, SUBLANES, D_CONV), lambda b, i: (b, jnp.maximum(i * halo - 1, 0), 0)),
                pl.BlockSpec((None, SUBLANES, D_CONV), lambda b, i: (b, jnp.minimum((i + 1) * halo, n_halo - 1), 0)),
                pl.BlockSpec((None, tm, D_CONV), row_map),
                u_spec, u_spec, u_spec,
                pl.BlockSpec((None, tm, D_ATTN), row_map),
                _const_spec((3, D_CONV)),
                _const_spec((1, D_SSM)),
                _const_spec((D_SSM, D_SSM)),
                _const_spec((1, D_SSM)),
                _const_spec((d, d)),
                _const_spec((1, d)),
                _const_spec((LANES, d)),
                _const_spec((N_EXPERTS, 1))]
    out_shape = (jax.ShapeDtypeStruct((bsz, t, d), F32),
                 jax.ShapeDtypeStruct((bsz, t, d), BF16),
                 jax.ShapeDtypeStruct((bsz, t, LANES), F32))
    out_specs = (pl.BlockSpec((None, tm, d), row_map),
                 pl.BlockSpec((None, tm, d), row_map),
                 pl.BlockSpec((None, tm, LANES), row_map))
    return pl.pallas_call(
        _mixout_kernel, out_shape=out_shape, grid=(bsz, t // tm),
        in_specs=in_specs, out_specs=out_specs,
        compiler_params=_cparams(2),
        name="mixer_output",
    )(h, mod_l, cv, cv, cv, cb, u, yf, yr, att,
      conv_w, ssm_d, w_glu, b_glu, w_out, norm_g, router_w, router_b)


EXPERTS_PER_CHUNK = 4


def _moe_kernel(t_ref, gates_ref, hmid_ref, mod_ref, wg_ref, wu_ref, wd_ref,
                wsg_ref, wsu_ref, wsd_ref, o_ref, acc_ref):
    t = t_ref[...]
    gates = gates_ref[...]
    hs = _silu(jnp.dot(t, wsg_ref[...], preferred_element_type=F32)) * jnp.dot(
        t, wsu_ref[...], preferred_element_type=F32)
    acc_ref[...] = jnp.dot(hs.astype(BF16), wsd_ref[...], preferred_element_type=F32)
    width = EXPERTS_PER_CHUNK * D_EXPERT
    n_chunks = N_EXPERTS // EXPERTS_PER_CHUNK

    def up(c):
        cols = slice(c * width, (c + 1) * width)
        return (jnp.dot(t, wg_ref[:, cols], preferred_element_type=F32),
                jnp.dot(t, wu_ref[:, cols], preferred_element_type=F32))

    nxt = up(0)
    for c in range(n_chunks):
        hg, hu = nxt
        if c + 1 < n_chunks:
            nxt = up(c + 1)
        hid = _silu(hg) * hu
        gated = []
        for e in range(EXPERTS_PER_CHUNK):
            k = c * EXPERTS_PER_CHUNK + e
            gated.append((hid[:, e * D_EXPERT:(e + 1) * D_EXPERT] * gates[:, k:k + 1]).astype(BF16))
        acc_ref[...] += jnp.dot(jnp.concatenate(gated, axis=1), wd_ref[c * width:(c + 1) * width, :],
                                preferred_element_type=F32)
    mod = mod_ref[...]
    o_ref[...] = hmid_ref[...] + mod[:, 5 * D_MODEL:6 * D_MODEL] * acc_ref[...]


def _moe(t, gates, hmid, mod_l, mod_row, wg, wu, wd, wsg, wsu, wsd, tm):
    bsz, n, d = hmid.shape
    row_map = lambda b, i: (b, i, 0)
    return pl.pallas_call(
        _moe_kernel,
        out_shape=jax.ShapeDtypeStruct((bsz, n, d), F32),
        grid=(bsz, n // tm),
        in_specs=[pl.BlockSpec((None, tm, d), row_map),
                  pl.BlockSpec((None, tm, LANES), row_map),
                  pl.BlockSpec((None, tm, d), row_map),
                  pl.BlockSpec((None, 1, 6 * d), lambda b, i: (mod_row(b), 0, 0)),
                  _const_spec(wg.shape), _const_spec(wu.shape), _const_spec(wd.shape),
                  _const_spec(wsg.shape), _const_spec(wsu.shape), _const_spec(wsd.shape)],
        out_specs=pl.BlockSpec((None, tm, d), row_map),
        scratch_shapes=[pltpu.VMEM((tm, d), F32)],
        compiler_params=_cparams(2),
        name="moe",
    )(t, gates, hmid, mod_l, wg, wu, wd, wsg, wsu, wsd)


def _rope_tables(n_tokens):
    rows = n_tokens // GRID_W
    row = np.repeat(np.arange(rows, dtype=np.float64), GRID_W)
    col = np.tile(np.arange(GRID_W, dtype=np.float64), rows)
    inv_freq = ROPE_BASE ** (-np.arange(AXIS_PAIRS, dtype=np.float64) / AXIS_PAIRS)
    ar, ac = row[:, None] * inv_freq, col[:, None] * inv_freq
    cos = np.concatenate([np.cos(ar), np.cos(ar), np.cos(ac), np.cos(ac)], axis=1).astype(np.float32)
    sin = np.concatenate([-np.sin(ar), np.sin(ar), -np.sin(ac), np.sin(ac)], axis=1).astype(np.float32)
    reps = (1, HEAD_PAIR // HEAD_DIM)
    return jnp.tile(jnp.asarray(cos), reps), jnp.tile(jnp.asarray(sin), reps)


def _ssm_params(lam_re, lam_im, log_dt, b_re, b_im, c_re, c_im):
    lead = lam_re.shape[:2]
    lr, li = lam_re.astype(F32), lam_im.astype(F32)
    dt = jnp.exp(log_dt.astype(F32))[..., None]
    mag = jnp.exp(lr * dt)
    ar, ai = mag * jnp.cos(li * dt), mag * jnp.sin(li * dt)
    den = lr * lr + li * li
    qr = ((ar - 1.0) * lr + ai * li) / den
    qi = (ai * lr - (ar - 1.0) * li) / den
    br, bi = b_re.astype(F32), b_im.astype(F32)
    bbar_re = qr[..., None] * br - qi[..., None] * bi
    bbar_im = qr[..., None] * bi + qi[..., None] * br
    eye = jnp.eye(SSM_GROUPS, dtype=F32)
    drive = lambda m: jnp.einsum('gk,ldgph->ldghkp', eye, m).reshape(lead + (D_SSM, N_STATE))
    read = lambda m: jnp.einsum('gk,ldghp->ldgpkh', eye, m.astype(F32)).reshape(lead + (N_STATE, D_SSM))
    bmat = jnp.concatenate([drive(bbar_re), drive(bbar_im)], axis=-1)
    cmat = jnp.concatenate([read(c_re), -read(c_im)], axis=-2)
    lamv = jnp.stack([ar, ai], axis=2).reshape(lead[0], 4, N_STATE)
    return bmat.astype(BF16), lamv, cmat.astype(BF16)


def _pair_heads(w, axis):
    shape = w.shape
    split = shape[:axis] + (N_KV_HEADS, N_Q_HEADS // N_KV_HEADS, HEAD_DIM) + shape[axis + 1:]
    return jnp.swapaxes(w.reshape(split), axis, axis + 1).reshape(shape)


def kernel(x, c, ctx, c_ctx, w_mod, b_mod, norm1_g, norm2_g, w_in, conv_w, ssm_lam_re, ssm_lam_im, ssm_log_dt, ssm_b_re, ssm_b_im, ssm_c_re, ssm_c_im, ssm_d, w_glu, b_glu, q_norm_g, k_norm_g, attn_sink, w_out, router_w, router_bias, w_exp_gate, w_exp_up, w_exp_down, w_sh_gate, w_sh_up, w_sh_down):
    bsz, n_lat, d = x.shape
    n_ctx = ctx.shape[1]
    depth = w_mod.shape[0]
    assert bsz == SUBLANES and d == D_MODEL

    mod_rows = 2 * SUBLANES
    cvec = jnp.zeros((mod_rows, d), F32).at[:bsz].set(c).at[bsz].set(c_ctx)
    mod = _modulation(cvec, w_mod, b_mod).reshape(depth, mod_rows, 1, 6 * d)
    lat_row = lambda b: b
    ctx_row = lambda b: bsz
    rope_tabs = _rope_tables(n_lat)

    tm_lat, tm_ctx, scan_steps = 512, 256, 64
    q0, kv0 = 3 * D_CONV + D_SSM, 3 * D_CONV + D_SSM + D_ATTN
    bmat, lamv, cmat = _ssm_params(ssm_lam_re, ssm_lam_im, ssm_log_dt, ssm_b_re, ssm_b_im, ssm_c_re, ssm_c_im)
    h, hc = x, ctx
    for l in range(depth):
        ctx_out = l < depth - 1
        mod_l = mod[l]
        w_in2 = jnp.concatenate([w_in[l][:, :q0], _pair_heads(w_in[l][:, q0:kv0], 1), w_in[l][:, kv0:]],
                                axis=1).astype(BF16)
        qg = jnp.tile(q_norm_g[l].astype(F32), HEAD_PAIR // HEAD_DIM)[None, :]
        kg = jnp.concatenate([jnp.tile(k_norm_g[l].astype(F32), N_KV_HEADS),
                              jnp.ones((N_KV_HEADS * HEAD_DIM,), F32)])[None, :]
        n1 = norm1_g[l][None, :]
        cv, cb, u_lat, q, kv = _in_projection(h, mod_l, lat_row, n1, w_in2, qg, kg, rope_tabs, tm_lat)
        cvc, cbc, u_ctx, qc, kvc = _in_projection(hc, mod_l, ctx_row, n1, w_in2, qg, kg, None, tm_ctx)
        yf, yr, yfc, yrc = _ssm_scan(u_lat, u_ctx, bmat[l], lamv[l], cmat[l], scan_steps)

        sink = attn_sink[l].astype(F32)
        att = _attention(sink, q, kv, kvc, window=True)

        w_out_l = jnp.concatenate([w_out[l][:D_CONV + D_SSM], _pair_heads(w_out[l][D_CONV + D_SSM:], 0)], axis=0)
        rw = router_w[l].astype(F32)
        rw_hi = rw.astype(BF16)
        rw_lo = (rw - rw_hi.astype(F32)).astype(BF16)
        rw2 = jnp.concatenate([rw_hi.T, rw_lo.T, jnp.zeros((LANES - 2 * N_EXPERTS, d), BF16)], axis=0)
        rb = router_bias[l].astype(F32)[:, None]
        post = dict(conv_w=conv_w[l], ssm_d=ssm_d[l][None, :], w_glu=w_glu[l].astype(BF16),
                    b_glu=b_glu[l][None, :], w_out=w_out_l.astype(BF16), norm_g=norm2_g[l][None, :],
                    router_w=rw2, router_b=rb)
        experts = (w_exp_gate[l].astype(BF16).reshape(d, -1), w_exp_up[l].astype(BF16).reshape(d, -1),
                   w_exp_down[l].astype(BF16).reshape(-1, d), w_sh_gate[l].astype(BF16),
                   w_sh_up[l].astype(BF16), w_sh_down[l].astype(BF16))
        hmid, t, gates = _mixer_output(h, mod_l, lat_row, cv, cb, u_lat, yf, yr, att, tm=tm_lat, **post)
        h_new = _moe(t, gates, hmid, mod_l, lat_row, *experts, tm=tm_lat)
        if ctx_out:
            attc = _attention(sink, qc, None, kvc, window=False)
            hmid_c, tc, gates_c = _mixer_output(hc, mod_l, ctx_row, cvc, cbc, u_ctx, yfc, yrc, attc,
                                                tm=tm_ctx, **post)
            flat = lambda a: a.reshape(1, bsz * n_ctx, a.shape[-1])
            hc = _moe(flat(tc), flat(gates_c), flat(hmid_c), mod_l, ctx_row, *experts,
                      tm=tm_lat).reshape(hc.shape)
        h = h_new
    return h
```

```python
import functools
import math

import jax
import jax.numpy as jnp
import numpy as np
from jax import lax
from jax.experimental import pallas as pl
from jax.experimental.pallas import tpu as pltpu

F32 = jnp.float32
BF16 = jnp.bfloat16
HIGHEST = lax.Precision.HIGHEST

D_MODEL = 1024
D_CONV = 256
D_SSM = 256
SSM_GROUP = 16
SSM_GROUPS = 16
SSM_STATE = 64
N_STATE = SSM_GROUPS * SSM_STATE
HEAD_DIM = 64
D_ATTN = 512
N_Q_HEADS = 8
N_KV_HEADS = 2
AXIS_PAIRS = 16
ROPE_BASE = 10000.0
GRID_W = 64
BLOCK = 128
N_EXPERTS = 32
TOP_K = 4
D_EXPERT = 128
ROUTED_SCALE = 2.5
NORM_EPS = 1e-6
NEG_INF = -1e30
D_PROJ = 3 * D_CONV + D_SSM + D_ATTN + 2 * N_KV_HEADS * HEAD_DIM
LANES = 128
SUBLANES = 8
MXU_TILE = 256
SCAN_PIECES = 4
VMEM_LIMIT = 56 * 1024 * 1024


def _cparams(n_axes):
    return pltpu.CompilerParams(dimension_semantics=("arbitrary",) * n_axes,
                                vmem_limit_bytes=VMEM_LIMIT)


def _const_spec(shape):
    nd = len(shape)
    return pl.BlockSpec(shape, lambda *_: (0,) * nd, pipeline_mode=pl.Buffered(1))


def _silu(x):
    return x * jax.nn.sigmoid(x)


def _rms_modulate(x, gain, shift, scale):
    ms = jnp.mean(x * x, axis=-1, keepdims=True)
    return (x * lax.rsqrt(ms + NORM_EPS) * gain) * (1.0 + scale) + shift


def _mod_kernel(c_ref, w_ref, b_ref, o_ref):
    cv = c_ref[...]
    o_ref[...] = jnp.dot(_silu(cv), w_ref[...], precision=HIGHEST,
                         preferred_element_type=F32) + b_ref[...]


def _modulation(cvec, w_mod, b_mod):
    depth, d, n = w_mod.shape
    rows = cvec.shape[0]
    tn = 1536
    return pl.pallas_call(
        _mod_kernel,
        out_shape=jax.ShapeDtypeStruct((depth, rows, n), F32),
        grid=(depth, n // tn),
        in_specs=[pl.BlockSpec((rows, d), lambda l, j: (0, 0)),
                  pl.BlockSpec((None, d, tn), lambda l, j: (l, 0, j)),
                  pl.BlockSpec((None, 1, tn), lambda l, j: (l, 0, j))],
        out_specs=pl.BlockSpec((None, rows, tn), lambda l, j: (l, 0, j)),
        compiler_params=_cparams(2),
        name="modulation",
    )(cvec, w_mod, b_mod.reshape(depth, 1, n))


HEAD_PAIR = 4 * HEAD_DIM
LOG2E = math.log2(math.e)
Q_SCALE = HEAD_DIM ** -0.5 * LOG2E


def _inproj_kernel(*refs, rope):
    if rope:
        (x_ref, mod_ref, g_ref, w_ref, qg_ref, kg_ref, ones_ref, perm_ref, cos_ref, sin_ref,
         cv_ref, cb_ref, u_ref, q_ref, kv_ref) = refs
    else:
        (x_ref, mod_ref, g_ref, w_ref, qg_ref, kg_ref, ones_ref,
         cv_ref, cb_ref, u_ref, q_ref, kv_ref) = refs
    mod = mod_ref[...]
    mb = _rms_modulate(x_ref[...], g_ref[...], mod[:, 0:D_MODEL], mod[:, D_MODEL:2 * D_MODEL]).astype(BF16)

    def proj(c0, width):
        return jnp.dot(mb, w_ref[:, c0:c0 + width], preferred_element_type=F32)

    cv_ref[...] = proj(2 * D_CONV, D_CONV) * proj(0, D_CONV)
    cb_ref[...] = proj(D_CONV, D_CONV)
    u_ref[...] = proj(3 * D_CONV, D_SSM)
    q0 = 3 * D_CONV + D_SSM
    kv0 = q0 + D_ATTN
    blocks = [(q0 + jj * HEAD_PAIR, qg_ref, q_ref, jj * HEAD_PAIR, Q_SCALE, False)
              for jj in range(D_ATTN // HEAD_PAIR)] + [(kv0, kg_ref, kv_ref, 0, 1.0, True)]
    is_k = lax.broadcasted_iota(jnp.int32, (1, HEAD_PAIR), 1) < N_KV_HEADS * HEAD_DIM
    xs = [proj(blk[0], HEAD_PAIR) for blk in blocks]
    ssqs = [jnp.dot((x * x).astype(BF16), ones_ref[...], preferred_element_type=F32) for x in xs]
    norms = [lax.rsqrt(ssq * (1.0 / HEAD_DIM) + NORM_EPS) for ssq in ssqs]
    norms = [jnp.where(is_k, nm, 1.0) if blk[5] else nm for nm, blk in zip(norms, blocks)]
    xns = [x * nm * blk[1][...] for x, nm, blk in zip(xs, norms, blocks)]
    if rope:
        partners = [jnp.dot(xn.astype(BF16), perm_ref[...], preferred_element_type=F32) for xn in xns]
        cos, sin = cos_ref[...], sin_ref[...]
        xns = [xn * (jnp.where(is_k, cos, 1.0) if blk[5] else cos) + pt * (jnp.where(is_k, sin, 0.0) if blk[5] else sin)
               for xn, pt, blk in zip(xns, partners, blocks)]
    for xn, (_, _, out_ref, o0, scale, _) in zip(xns, blocks):
        if scale != 1.0:
            xn = xn * scale
        out_ref[:, o0:o0 + HEAD_PAIR] = xn.astype(BF16)


def _in_projection(h, mod_l, mod_row, norm_g, w_in2, qg, kg, rope_tabs, tm):
    bsz, t, d = h.shape
    rope = rope_tabs is not None
    row_map = lambda b, i: (b, i, 0)
    in_specs = [pl.BlockSpec((None, tm, d), row_map),
                pl.BlockSpec((None, 1, 6 * d), lambda b, i: (mod_row(b), 0, 0)),
                _const_spec((1, d)),
                _const_spec((d, D_PROJ)),
                _const_spec((1, HEAD_PAIR)),
                _const_spec((1, HEAD_PAIR)),
                _const_spec((HEAD_PAIR, HEAD_PAIR))]
    lane = jnp.arange(HEAD_PAIR)
    head_ones = (lane[:, None] // HEAD_DIM == lane[None, :] // HEAD_DIM).astype(BF16)
    args = [h, mod_l, norm_g, w_in2, qg, kg, head_ones]
    if rope:
        swap = (lane[:, None] == (lane[None, :] ^ AXIS_PAIRS)).astype(BF16)
        in_specs += [_const_spec((HEAD_PAIR, HEAD_PAIR))]
        in_specs += [pl.BlockSpec((tm, HEAD_PAIR), lambda b, i: (i, 0))] * 2
        args += [swap] + list(rope_tabs)
    out_shape = (jax.ShapeDtypeStruct((bsz, t, D_CONV), F32),
                 jax.ShapeDtypeStruct((bsz, t, D_CONV), F32),
                 jax.ShapeDtypeStruct((bsz, t, D_SSM), F32),
                 jax.ShapeDtypeStruct((bsz, t, D_ATTN), BF16),
                 jax.ShapeDtypeStruct((bsz, t, HEAD_PAIR), BF16))
    out_specs = (pl.BlockSpec((None, tm, D_CONV), row_map),
                 pl.BlockSpec((None, tm, D_CONV), row_map),
                 pl.BlockSpec((None, tm, D_SSM), row_map),
                 pl.BlockSpec((None, tm, D_ATTN), row_map),
                 pl.BlockSpec((None, tm, HEAD_PAIR), row_map))
    return pl.pallas_call(
        functools.partial(_inproj_kernel, rope=rope),
        out_shape=out_shape, grid=(bsz, t // tm), in_specs=in_specs, out_specs=out_specs,
        compiler_params=_cparams(2),
        name="in_projection_lat" if rope else "in_projection_ctx",
    )(*args)


def _scan_kernel(uf_ref, ur_ref, h0f_ref, h0r_ref, bmat_ref, lam_ref, cmat_ref,
                 yf_ref, yr_ref, hf_ref, hr_ref, *, steps):
    @pl.when(pl.program_id(0) == 0)
    def _():
        hf_ref[...] = h0f_ref[...]
        hr_ref[...] = h0r_ref[...]

    piece = steps // SCAN_PIECES

    def drive(u_ref, d, p):
        u = jnp.concatenate([u_ref[:, t, :] for t in range(p * piece, (p + 1) * piece)], axis=0)
        return jnp.dot(u.astype(BF16), bmat_ref[d], preferred_element_type=F32)

    def recur(bu, d, h, descending):
        l_re = jnp.broadcast_to(lam_ref[2 * d:2 * d + 1, :], (SUBLANES, N_STATE))
        l_im = jnp.broadcast_to(lam_ref[2 * d + 1:2 * d + 2, :], (SUBLANES, N_STATE))
        h_re, h_im = h
        states = [None] * piece
        for t in (range(piece - 1, -1, -1) if descending else range(piece)):
            rows = slice(t * SUBLANES, (t + 1) * SUBLANES)
            h_re, h_im = (l_re * h_re - l_im * h_im + bu[rows, 0:N_STATE],
                          l_re * h_im + l_im * h_re + bu[rows, N_STATE:2 * N_STATE])
            states[t] = (h_re, h_im)
        return (h_re, h_im), states

    def readout(states, d, y_ref, p):
        s_re = jnp.concatenate([s[0] for s in states], axis=0).astype(BF16)
        s_im = jnp.concatenate([s[1] for s in states], axis=0).astype(BF16)
        y = jnp.dot(jnp.concatenate([s_re, s_im], axis=1), cmat_ref[d], preferred_element_type=F32)
        for t in range(piece):
            y_ref[:, p * piece + t, :] = y[t * SUBLANES:(t + 1) * SUBLANES, :]

    hf = (hf_ref[:, 0:N_STATE], hf_ref[:, N_STATE:2 * N_STATE])
    hr = (hr_ref[:, 0:N_STATE], hr_ref[:, N_STATE:2 * N_STATE])
    for p in range(SCAN_PIECES):
        q = SCAN_PIECES - 1 - p
        hf, sf = recur(drive(uf_ref, 0, p), 0, hf, False)
        hr, sr = recur(drive(ur_ref, 1, q), 1, hr, True)
        readout(sf, 0, yf_ref, p)
        readout(sr, 1, yr_ref, q)
    hf_ref[:, 0:N_STATE], hf_ref[:, N_STATE:2 * N_STATE] = hf
    hr_ref[:, 0:N_STATE], hr_ref[:, N_STATE:2 * N_STATE] = hr


def _ssm_scan(u, h0f, h0r, bmat, lamv, cmat, steps):
    bsz, t, _ = u.shape
    n = t // steps
    blk = lambda f: pl.BlockSpec((bsz, steps, D_SSM), lambda j: (0, f(j), 0))
    fwd, rev = blk(lambda j: j), blk(lambda j: n - 1 - j)
    carry = jax.ShapeDtypeStruct((SUBLANES, 2 * N_STATE), F32)
    return pl.pallas_call(
        functools.partial(_scan_kernel, steps=steps),
        out_shape=(jax.ShapeDtypeStruct(u.shape, F32),) * 2 + (carry, carry),
        grid=(n,),
        in_specs=[fwd, rev,
                  _const_spec((SUBLANES, 2 * N_STATE)),
                  _const_spec((SUBLANES, 2 * N_STATE)),
                  _const_spec((2, D_SSM, 2 * N_STATE)),
                  _const_spec((4, N_STATE)),
                  _const_spec((2, 2 * N_STATE, D_SSM))],
        out_specs=(fwd, rev,
                   pl.BlockSpec((SUBLANES, 2 * N_STATE), lambda j: (0, 0)),
                   pl.BlockSpec((SUBLANES, 2 * N_STATE), lambda j: (0, 0))),
        compiler_params=_cparams(1),
        name="s5_scan",
    )(u, u, h0f, h0r, bmat, lamv, cmat)


def _attn_kernel(sink_ref, q_ref, *refs, window):
    if window:
        kvp_ref, kvo_ref, kvn_ref, kvc_ref, o_ref = refs
    else:
        kvc_ref, o_ref = refs
    n = pl.program_id(1)
    last = pl.num_programs(1) - 1
    rows = 2 * BLOCK
    lane = lax.broadcasted_iota(jnp.int32, (1, LANES), 1)
    lo = lane < HEAD_DIM
    qi = lax.broadcasted_iota(jnp.int32, (rows, BLOCK), 0) & (BLOCK - 1)
    kj = lax.broadcasted_iota(jnp.int32, (rows, BLOCK), 1)
    upper_head = lax.broadcasted_iota(jnp.int32, (rows, 1), 0) >= BLOCK
    zero = jnp.zeros((), BF16)
    nt = (((1,), (1,)), ((), ()))
    k_cols, v_cols = slice(0, LANES), slice(LANES, 2 * LANES)
    units = range(D_ATTN // LANES)
    qs = []
    for j in units:
        qp = q_ref[:, j * LANES:(j + 1) * LANES]
        qs.append(jnp.concatenate([jnp.where(lo, qp, zero), jnp.where(lo, zero, qp)], axis=0))
    s_all = []
    for j in units:
        scores = []
        if window:
            s = lax.dot_general(qs[j], kvp_ref[:, k_cols], nt, preferred_element_type=F32)
            scores.append(jnp.where((kj >= qi) & (n > 0), s, NEG_INF))
            scores.append(lax.dot_general(qs[j], kvo_ref[:, k_cols], nt, preferred_element_type=F32))
            s = lax.dot_general(qs[j], kvn_ref[:, k_cols], nt, preferred_element_type=F32)
            scores.append(jnp.where((kj <= qi) & (n < last), s, NEG_INF))
        scores.append(lax.dot_general(qs[j], kvc_ref[:, k_cols], nt, preferred_element_type=F32))
        s_all.append(jnp.concatenate(scores, axis=1))
    sinks = [jnp.where(upper_head, sink_ref[N_Q_HEADS // 2 + j], sink_ref[j]) * LOG2E for j in units]
    ms = [jnp.maximum(sinks[j], jnp.max(s_all[j], axis=-1, keepdims=True)) for j in units]
    es = [jnp.exp2(s_all[j] - ms[j]) for j in units]
    denoms = [jnp.exp2(sinks[j] - ms[j]) + jnp.sum(es[j], axis=-1, keepdims=True) for j in units]
    values = [kvp_ref[:, v_cols], kvo_ref[:, v_cols], kvn_ref[:, v_cols]] if window else []
    values = jnp.concatenate(values + [kvc_ref[:, v_cols]], axis=0)
    accs = [jnp.dot(es[j].astype(BF16), values, preferred_element_type=F32) for j in units]
    for j in units:
        out = accs[j] / denoms[j]
        o_ref[:, j * LANES:(j + 1) * LANES] = jnp.where(lo, out[0:BLOCK], out[BLOCK:2 * BLOCK]).astype(BF16)


def _attention(sink, q, kv, kv_ctx, window):
    bsz, t, _ = q.shape
    n_ctx = kv_ctx.shape[1]
    nb = t // BLOCK
    blk = lambda f: pl.BlockSpec((None, BLOCK, HEAD_PAIR), f)
    in_specs = [pl.BlockSpec(memory_space=pltpu.SMEM),
                pl.BlockSpec((None, BLOCK, D_ATTN), lambda b, n: (b, n, 0))]
    args = [sink, q]
    if window:
        prev = lambda b, n: (b, jnp.maximum(n - 1, 0), 0)
        own = lambda b, n: (b, n, 0)
        nxt = lambda b, n: (b, jnp.minimum(n + 1, nb - 1), 0)
        in_specs += [blk(prev), blk(own), blk(nxt)]
        args += [kv, kv, kv]
    in_specs += [pl.BlockSpec((None, n_ctx, HEAD_PAIR), lambda b, n: (b, 0, 0))]
    args += [kv_ctx]
    return pl.pallas_call(
        functools.partial(_attn_kernel, window=window),
        out_shape=jax.ShapeDtypeStruct((bsz, t, D_ATTN), BF16),
        grid=(bsz, nb), in_specs=in_specs,
        out_specs=pl.BlockSpec((None, BLOCK, D_ATTN), lambda b, n: (b, n, 0)),
        compiler_params=_cparams(2),
        name="window_attention" if window else "context_attention",
    )(*args)


def _mixout_kernel(h_ref, mod_ref, cv_ref, cvp_ref, cvn_ref, cb_ref, u_ref, yf_ref, yr_ref, att_ref,
                   convw_ref, d_ref, wglu_ref, bglu_ref, wout_ref, g2_ref, rw_ref, rb_ref,
                   hmid_ref, t_ref, gates_ref):
    i = pl.program_id(1)
    last = pl.num_programs(1) - 1
    tm = cv_ref.shape[0]
    cv = cv_ref[...]
    row = lax.broadcasted_iota(jnp.int32, (tm, 1), 0)
    before = jnp.where(i > 0, cvp_ref[SUBLANES - 1:SUBLANES, :], 0.0)
    after = jnp.where(i < last, cvn_ref[0:1, :], 0.0)
    down = jnp.where(row == 0, before, pltpu.roll(cv, 1, 0))
    up = jnp.where(row == tm - 1, after, pltpu.roll(cv, tm - 1, 0))
    w = convw_ref[...]
    y_conv = cb_ref[...] * (w[0:1, :] * down + w[1:2, :] * cv + w[2:3, :] * up)
    y = d_ref[...] * u_ref[...] + yf_ref[...] + yr_ref[...]
    gl = jax.nn.gelu(y)
    z = jnp.dot(gl.astype(BF16), wglu_ref[...], preferred_element_type=F32) + bglu_ref[...]
    y_ssm = gl * jax.nn.sigmoid(z)
    mix = (jnp.dot(y_conv.astype(BF16), wout_ref[0:D_CONV, :], preferred_element_type=F32)
           + jnp.dot(y_ssm.astype(BF16), wout_ref[D_CONV:D_CONV + D_SSM, :], preferred_element_type=F32)
           + jnp.dot(att_ref[...], wout_ref[D_CONV + D_SSM:, :], preferred_element_type=F32))
    mod = mod_ref[...]
    h_mid = h_ref[...] + mod[:, 2 * D_MODEL:3 * D_MODEL] * mix
    hmid_ref[...] = h_mid
    t = _rms_modulate(h_mid, g2_ref[...], mod[:, 3 * D_MODEL:4 * D_MODEL], mod[:, 4 * D_MODEL:5 * D_MODEL])
    t_ref[...] = t.astype(BF16)
    t_hi = t.astype(BF16)
    t_lo = (t - t_hi.astype(F32)).astype(BF16)
    nt = (((1,), (1,)), ((), ()))
    a = lax.dot_general(rw_ref[...], t_hi, nt, preferred_element_type=F32)
    b = lax.dot_general(rw_ref[...], t_lo, nt, preferred_element_type=F32)
    logits = a[0:N_EXPERTS] + a[N_EXPERTS:2 * N_EXPERTS] + b[0:N_EXPERTS]
    scores = jax.nn.sigmoid(logits)
    eidx = lax.broadcasted_iota(jnp.int32, (N_EXPERTS, 1), 0).astype(F32)
    biased = scores + rb_ref[...]
    sel = jnp.zeros(scores.shape, F32)
    for _ in range(TOP_K):
        best = jnp.max(biased, axis=0, keepdims=True)
        first = jnp.min(jnp.where(biased == best, eidx, float(N_EXPERTS)), axis=0, keepdims=True)
        pick = eidx == first
        sel = jnp.where(pick, scores, sel)
        biased = jnp.where(pick, -jnp.inf, biased)
    gates = sel / jnp.sum(sel, axis=0, keepdims=True) * ROUTED_SCALE
    padded = jnp.concatenate([gates, jnp.zeros((LANES - N_EXPERTS, gates.shape[1]), F32)], axis=0)
    gates_ref[...] = padded.T


def _mixer_output(h, mod_l, mod_row, cv, cb, u, yf, yr, att,
                  conv_w, ssm_d, w_glu, b_glu, w_out, norm_g, router_w, router_b, tm):
    bsz, t, d = h.shape
    halo = tm // SUBLANES
    n_halo = t // SUBLANES
    row_map = lambda b, i: (b, i, 0)
    u_spec = pl.BlockSpec((None, tm, D_SSM), row_map)
    in_specs = [pl.BlockSpec((None, tm, d), row_map),
                pl.BlockSpec((None, 1, 6 * d), lambda b, i: (mod_row(b), 0, 0)),
                pl.BlockSpec((None, tm, D_CONV), row_map),
                pl.BlockSpec((None, SUBLANES, D_CONV), lambda b, i: (b, jnp.maximum(i * halo - 1, 0), 0)),
                pl.BlockSpec((None, SUBLANES, D_CONV), lambda b, i: (b, jnp.minimum((i + 1) * halo, n_halo - 1), 0)),
                pl.BlockSpec((None, tm, D_CONV), row_map),
                u_spec, u_spec, u_spec,
                pl.BlockSpec((None, tm, D_ATTN), row_map),
                _const_spec((3, D_CONV)),
                _const_spec((1, D_SSM)),
                _const_spec((D_SSM, D_SSM)),
                _const_spec((1, D_SSM)),
                _const_spec((d, d)),
                _const_spec((1, d)),
                _const_spec((LANES, d)),
                _const_spec((N_EXPERTS, 1))]
    out_shape = (jax.ShapeDtypeStruct((bsz, t, d), F32),
                 jax.ShapeDtypeStruct((bsz, t, d), BF16),
                 jax.ShapeDtypeStruct((bsz, t, LANES), F32))
    out_specs = (pl.BlockSpec((None, tm, d), row_map),
                 pl.BlockSpec((None, tm, d), row_map),
                 pl.BlockSpec((None, tm, LANES), row_map))
    return pl.pallas_call(
        _mixout_kernel, out_shape=out_shape, grid=(bsz, t // tm),
        in_specs=in_specs, out_specs=out_specs,
        compiler_params=_cparams(2),
        name="mixer_output",
    )(h, mod_l, cv, cv, cv, cb, u, yf, yr, att,
      conv_w, ssm_d, w_glu, b_glu, w_out, norm_g, router_w, router_b)


EXPERTS_PER_CHUNK = 4


def _moe_kernel(t_ref, gates_ref, hmid_ref, mod_ref, wg_ref, wu_ref, wd_ref,
                wsg_ref, wsu_ref, wsd_ref, o_ref, acc_ref):
    t = t_ref[...]
    gates = gates_ref[...]
    hs = _silu(jnp.dot(t, wsg_ref[...], preferred_element_type=F32)) * jnp.dot(
        t, wsu_ref[...], preferred_element_type=F32)
    acc_ref[...] = jnp.dot(hs.astype(BF16), wsd_ref[...], preferred_element_type=F32)
    width = EXPERTS_PER_CHUNK * D_EXPERT
    n_chunks = N_EXPERTS // EXPERTS_PER_CHUNK

    def up(c):
        cols = slice(c * width, (c + 1) * width)
        return (jnp.dot(t, wg_ref[:, cols], preferred_element_type=F32),
                jnp.dot(t, wu_ref[:, cols], preferred_element_type=F32))

    nxt = up(0)
    for c in range(n_chunks):
        hg, hu = nxt
        if c + 1 < n_chunks:
            nxt = up(c + 1)
        hid = _silu(hg) * hu
        gated = []
        for e in range(EXPERTS_PER_CHUNK):
            k = c * EXPERTS_PER_CHUNK + e
            gated.append((hid[:, e * D_EXPERT:(e + 1) * D_EXPERT] * gates[:, k:k + 1]).astype(BF16))
        acc_ref[...] += jnp.dot(jnp.concatenate(gated, axis=1), wd_ref[c * width:(c + 1) * width, :],
                                preferred_element_type=F32)
    mod = mod_ref[...]
    o_ref[...] = hmid_ref[...] + mod[:, 5 * D_MODEL:6 * D_MODEL] * acc_ref[...]


def _moe(t, gates, hmid, mod_l, mod_row, wg, wu, wd, wsg, wsu, wsd, tm):
    bsz, n, d = hmid.shape
    row_map = lambda b, i: (b, i, 0)
    return pl.pallas_call(
        _moe_kernel,
        out_shape=jax.ShapeDtypeStruct((bsz, n, d), F32),
        grid=(bsz, n // tm),
        in_specs=[pl.BlockSpec((None, tm, d), row_map),
                  pl.BlockSpec((None, tm, LANES), row_map),
                  pl.BlockSpec((None, tm, d), row_map),
                  pl.BlockSpec((None, 1, 6 * d), lambda b, i: (mod_row(b), 0, 0)),
                  _const_spec(wg.shape), _const_spec(wu.shape), _const_spec(wd.shape),
                  _const_spec(wsg.shape), _const_spec(wsu.shape), _const_spec(wsd.shape)],
        out_specs=pl.BlockSpec((None, tm, d), row_map),
        scratch_shapes=[pltpu.VMEM((tm, d), F32)],
        compiler_params=_cparams(2),
        name="moe",
    )(t, gates, hmid, mod_l, wg, wu, wd, wsg, wsu, wsd)


def _rope_tables(n_tokens):
    rows = n_tokens // GRID_W
    row = np.repeat(np.arange(rows, dtype=np.float64), GRID_W)
    col = np.tile(np.arange(GRID_W, dtype=np.float64), rows)
    inv_freq = ROPE_BASE ** (-np.arange(AXIS_PAIRS, dtype=np.float64) / AXIS_PAIRS)
    ar, ac = row[:, None] * inv_freq, col[:, None] * inv_freq
    cos = np.concatenate([np.cos(ar), np.cos(ar), np.cos(ac), np.cos(ac)], axis=1).astype(np.float32)
    sin = np.concatenate([-np.sin(ar), np.sin(ar), -np.sin(ac), np.sin(ac)], axis=1).astype(np.float32)
    reps = (1, HEAD_PAIR // HEAD_DIM)
    return jnp.tile(jnp.asarray(cos), reps), jnp.tile(jnp.asarray(sin), reps)


def _ssm_params(lam_re, lam_im, log_dt, b_re, b_im, c_re, c_im):
    lead = lam_re.shape[:2]
    lr, li = lam_re.astype(F32), lam_im.astype(F32)
    dt = jnp.exp(log_dt.astype(F32))[..., None]
    mag = jnp.exp(lr * dt)
    ar, ai = mag * jnp.cos(li * dt), mag * jnp.sin(li * dt)
    den = lr * lr + li * li
    qr = ((ar - 1.0) * lr + ai * li) / den
    qi = (ai * lr - (ar - 1.0) * li) / den
    br, bi = b_re.astype(F32), b_im.astype(F32)
    bbar_re = qr[..., None] * br - qi[..., None] * bi
    bbar_im = qr[..., None] * bi + qi[..., None] * br
    eye = jnp.eye(SSM_GROUPS, dtype=F32)
    drive = lambda m: jnp.einsum('gk,ldgph->ldghkp', eye, m).reshape(lead + (D_SSM, N_STATE))
    read = lambda m: jnp.einsum('gk,ldghp->ldgpkh', eye, m.astype(F32)).reshape(lead + (N_STATE, D_SSM))
    bmat = jnp.concatenate([drive(bbar_re), drive(bbar_im)], axis=-1)
    cmat = jnp.concatenate([read(c_re), -read(c_im)], axis=-2)
    lamv = jnp.stack([ar, ai], axis=2).reshape(lead[0], 4, N_STATE)
    return bmat.astype(BF16), lamv, cmat.astype(BF16)


def _pair_heads(w, axis):
    shape = w.shape
    split = shape[:axis] + (N_KV_HEADS, N_Q_HEADS // N_KV_HEADS, HEAD_DIM) + shape[axis + 1:]
    return jnp.swapaxes(w.reshape(split), axis, axis + 1).reshape(shape)


def kernel(x, c, ctx, c_ctx, w_mod, b_mod, norm1_g, norm2_g, w_in, conv_w, ssm_lam_re, ssm_lam_im, ssm_log_dt, ssm_b_re, ssm_b_im, ssm_c_re, ssm_c_im, ssm_d, w_glu, b_glu, q_norm_g, k_norm_g, attn_sink, w_out, router_w, router_bias, w_exp_gate, w_exp_up, w_exp_down, w_sh_gate, w_sh_up, w_sh_down):
    bsz, n_lat, d = x.shape
    n_ctx = ctx.shape[1]
    depth = w_mod.shape[0]
    assert bsz == SUBLANES and d == D_MODEL

    mod_rows = 2 * SUBLANES
    cvec = jnp.zeros((mod_rows, d), F32).at[:bsz].set(c).at[bsz].set(c_ctx)
    mod = _modulation(cvec, w_mod, b_mod).reshape(depth, mod_rows, 1, 6 * d)
    lat_row = lambda b: b
    ctx_row = lambda b: bsz
    rope_tabs = _rope_tables(n_lat)

    tm_lat, tm_ctx, scan_steps = 512, 256, 128
    q0, kv0 = 3 * D_CONV + D_SSM, 3 * D_CONV + D_SSM + D_ATTN
    bmat, lamv, cmat = _ssm_params(ssm_lam_re, ssm_lam_im, ssm_log_dt, ssm_b_re, ssm_b_im, ssm_c_re, ssm_c_im)
    h, hc = x, ctx
    for l in range(depth):
        ctx_out = l < depth - 1
        mod_l = mod[l]
        w_in2 = jnp.concatenate([w_in[l][:, :q0], _pair_heads(w_in[l][:, q0:kv0], 1), w_in[l][:, kv0:]],
                                axis=1).astype(BF16)
        qg = jnp.tile(q_norm_g[l].astype(F32), HEAD_PAIR // HEAD_DIM)[None, :]
        kg = jnp.concatenate([jnp.tile(k_norm_g[l].astype(F32), N_KV_HEADS),
                              jnp.ones((N_KV_HEADS * HEAD_DIM,), F32)])[None, :]
        n1 = norm1_g[l][None, :]
        cv, cb, u_lat, q, kv = _in_projection(h, mod_l, lat_row, n1, w_in2, qg, kg, rope_tabs, tm_lat)
        cvc, cbc, u_ctx, qc, kvc = _in_projection(hc, mod_l, ctx_row, n1, w_in2, qg, kg, None, tm_ctx)
        zero_state = jnp.zeros((bsz, 2 * N_STATE), F32)
        yfc, yrc, hf, hr = _ssm_scan(u_ctx, zero_state, zero_state, bmat[l], lamv[l], cmat[l], scan_steps)
        yf, yr, _, _ = _ssm_scan(u_lat, hf, hr, bmat[l], lamv[l], cmat[l], scan_steps)

        sink = attn_sink[l].astype(F32)
        att = _attention(sink, q, kv, kvc, window=True)

        w_out_l = jnp.concatenate([w_out[l][:D_CONV + D_SSM], _pair_heads(w_out[l][D_CONV + D_SSM:], 0)], axis=0)
        rw = router_w[l].astype(F32)
        rw_hi = rw.astype(BF16)
        rw_lo = (rw - rw_hi.astype(F32)).astype(BF16)
        rw2 = jnp.concatenate([rw_hi.T, rw_lo.T, jnp.zeros((LANES - 2 * N_EXPERTS, d), BF16)], axis=0)
        rb = router_bias[l].astype(F32)[:, None]
        post = dict(conv_w=conv_w[l], ssm_d=ssm_d[l][None, :], w_glu=w_glu[l].astype(BF16),
                    b_glu=b_glu[l][None, :], w_out=w_out_l.astype(BF16), norm_g=norm2_g[l][None, :],
                    router_w=rw2, router_b=rb)
        experts = (w_exp_gate[l].astype(BF16).reshape(d, -1), w_exp_up[l].astype(BF16).reshape(d, -1),
                   w_exp_down[l].astype(BF16).reshape(-1, d), w_sh_gate[l].astype(BF16),
                   w_sh_up[l].astype(BF16), w_sh_down[l].astype(BF16))
        hmid, t, gates = _mixer_output(h, mod_l, lat_row, cv, cb, u_lat, yf, yr, att, tm=tm_lat, **post)
        h_new = _moe(t, gates, hmid, mod_l, lat_row, *experts, tm=tm_lat)
        if ctx_out:
            attc = _attention(sink, qc, None, kvc, window=False)
            hmid_c, tc, gates_c = _mixer_output(hc, mod_l, ctx_row, cvc, cbc, u_ctx, yfc, yrc, attc,
                                                tm=tm_ctx, **post)
            flat = lambda a: a.reshape(1, bsz * n_ctx, a.shape[-1])
            hc = _moe(flat(tc), flat(gates_c), flat(hmid_c), mod_l, ctx_row, *experts,
                      tm=tm_lat).reshape(hc.shape)
        h = h_new
    return h
```

```python
import functools
import math

import jax
import jax.numpy as jnp
import numpy as np
from jax import lax
from jax.experimental import pallas as pl
from jax.experimental.pallas import tpu as pltpu

F32 = jnp.float32
BF16 = jnp.bfloat16
HIGHEST = lax.Precision.HIGHEST

D_MODEL = 1024
D_CONV = 256
D_SSM = 256
SSM_GROUP = 16
SSM_GROUPS = 16
SSM_STATE = 64
N_STATE = SSM_GROUPS * SSM_STATE
HEAD_DIM = 64
D_ATTN = 512
N_Q_HEADS = 8
N_KV_HEADS = 2
AXIS_PAIRS = 16
ROPE_BASE = 10000.0
GRID_W = 64
BLOCK = 128
N_EXPERTS = 32
TOP_K = 4
D_EXPERT = 128
ROUTED_SCALE = 2.5
NORM_EPS = 1e-6
NEG_INF = -1e30
D_PROJ = 3 * D_CONV + D_SSM + D_ATTN + 2 * N_KV_HEADS * HEAD_DIM
LANES = 128
SUBLANES = 8
MXU_TILE = 256
SCAN_PIECES = 4
VMEM_LIMIT = 56 * 1024 * 1024


def _cparams(n_axes):
    return pltpu.CompilerParams(dimension_semantics=("arbitrary",) * n_axes,
                                vmem_limit_bytes=VMEM_LIMIT)


def _const_spec(shape):
    nd = len(shape)
    return pl.BlockSpec(shape, lambda *_: (0,) * nd, pipeline_mode=pl.Buffered(1))


def _silu(x):
    return x * jax.nn.sigmoid(x)


def _rms_modulate(x, gain, shift, scale):
    ms = jnp.mean(x * x, axis=-1, keepdims=True)
    return (x * lax.rsqrt(ms + NORM_EPS) * gain) * (1.0 + scale) + shift


def _mod_kernel(c_ref, w_ref, b_ref, o_ref):
    cv = c_ref[...]
    o_ref[...] = jnp.dot(_silu(cv), w_ref[...], precision=HIGHEST,
                         preferred_element_type=F32) + b_ref[...]


def _modulation(cvec, w_mod, b_mod):
    depth, d, n = w_mod.shape
    rows = cvec.shape[0]
    tn = 1536
    return pl.pallas_call(
        _mod_kernel,
        out_shape=jax.ShapeDtypeStruct((depth, rows, n), F32),
        grid=(depth, n // tn),
        in_specs=[pl.BlockSpec((rows, d), lambda l, j: (0, 0)),
                  pl.BlockSpec((None, d, tn), lambda l, j: (l, 0, j)),
                  pl.BlockSpec((None, 1, tn), lambda l, j: (l, 0, j))],
        out_specs=pl.BlockSpec((None, rows, tn), lambda l, j: (l, 0, j)),
        compiler_params=_cparams(2),
        name="modulation",
    )(cvec, w_mod, b_mod.reshape(depth, 1, n))


HEAD_PAIR = 4 * HEAD_DIM
LOG2E = math.log2(math.e)
Q_SCALE = HEAD_DIM ** -0.5 * LOG2E


def _inproj_kernel(*refs, rope):
    if rope:
        (x_ref, mod_ref, g_ref, w_ref, qg_ref, kg_ref, ones_ref, perm_ref, cos_ref, sin_ref,
         cv_ref, cb_ref, u_ref, q_ref, kv_ref) = refs
    else:
        (x_ref, mod_ref, g_ref, w_ref, qg_ref, kg_ref, ones_ref,
         cv_ref, cb_ref, u_ref, q_ref, kv_ref) = refs
    mod = mod_ref[...]
    mb = _rms_modulate(x_ref[...], g_ref[...], mod[:, 0:D_MODEL], mod[:, D_MODEL:2 * D_MODEL]).astype(BF16)

    def proj(c0, width):
        return jnp.dot(mb, w_ref[:, c0:c0 + width], preferred_element_type=F32)

    conv_v = proj(2 * D_CONV, D_CONV) * proj(0, D_CONV)
    conv_b = proj(D_CONV, D_CONV)
    u = proj(3 * D_CONV, D_SSM)
    q0 = 3 * D_CONV + D_SSM
    kv0 = q0 + D_ATTN
    blocks = [(q0 + jj * HEAD_PAIR, qg_ref, q_ref, jj * HEAD_PAIR, Q_SCALE, False)
              for jj in range(D_ATTN // HEAD_PAIR)] + [(kv0, kg_ref, kv_ref, 0, 1.0, True)]
    is_k = lax.broadcasted_iota(jnp.int32, (1, HEAD_PAIR), 1) < N_KV_HEADS * HEAD_DIM
    xs = [proj(blk[0], HEAD_PAIR) for blk in blocks]
    ssqs = [jnp.dot((x * x).astype(BF16), ones_ref[...], preferred_element_type=F32) for x in xs]
    norms = [lax.rsqrt(ssq * (1.0 / HEAD_DIM) + NORM_EPS) for ssq in ssqs]
    norms = [jnp.where(is_k, nm, 1.0) if blk[5] else nm for nm, blk in zip(norms, blocks)]
    xns = [x * nm * blk[1][...] for x, nm, blk in zip(xs, norms, blocks)]
    if rope:
        partners = [jnp.dot(xn.astype(BF16), perm_ref[...], preferred_element_type=F32) for xn in xns]
        cos, sin = cos_ref[...], sin_ref[...]
        xns = [xn * (jnp.where(is_k, cos, 1.0) if blk[5] else cos) + pt * (jnp.where(is_k, sin, 0.0) if blk[5] else sin)
               for xn, pt, blk in zip(xns, partners, blocks)]
    for xn, (_, _, out_ref, o0, scale, _) in zip(xns, blocks):
        if scale != 1.0:
            xn = xn * scale
        out_ref[:, o0:o0 + HEAD_PAIR] = xn.astype(BF16)
    cv_ref[...] = conv_v
    cb_ref[...] = conv_b
    u_ref[...] = u


def _in_projection(h, mod_l, mod_row, norm_g, w_in2, qg, kg, rope_tabs, tm):
    bsz, t, d = h.shape
    rope = rope_tabs is not None
    row_map = lambda b, i: (b, i, 0)
    in_specs = [pl.BlockSpec((None, tm, d), row_map),
                pl.BlockSpec((None, 1, 6 * d), lambda b, i: (mod_row(b), 0, 0)),
                _const_spec((1, d)),
                _const_spec((d, D_PROJ)),
                _const_spec((1, HEAD_PAIR)),
                _const_spec((1, HEAD_PAIR)),
                _const_spec((HEAD_PAIR, HEAD_PAIR))]
    lane = jnp.arange(HEAD_PAIR)
    head_ones = (lane[:, None] // HEAD_DIM == lane[None, :] // HEAD_DIM).astype(BF16)
    args = [h, mod_l, norm_g, w_in2, qg, kg, head_ones]
    if rope:
        swap = (lane[:, None] == (lane[None, :] ^ AXIS_PAIRS)).astype(BF16)
        in_specs += [_const_spec((HEAD_PAIR, HEAD_PAIR))]
        in_specs += [pl.BlockSpec((tm, HEAD_PAIR), lambda b, i: (i, 0))] * 2
        args += [swap] + list(rope_tabs)
    out_shape = (jax.ShapeDtypeStruct((bsz, t, D_CONV), F32),
                 jax.ShapeDtypeStruct((bsz, t, D_CONV), F32),
                 jax.ShapeDtypeStruct((bsz, t, D_SSM), F32),
                 jax.ShapeDtypeStruct((bsz, t, D_ATTN), BF16),
                 jax.ShapeDtypeStruct((bsz, t, HEAD_PAIR), BF16))
    out_specs = (pl.BlockSpec((None, tm, D_CONV), row_map),
                 pl.BlockSpec((None, tm, D_CONV), row_map),
                 pl.BlockSpec((None, tm, D_SSM), row_map),
                 pl.BlockSpec((None, tm, D_ATTN), row_map),
                 pl.BlockSpec((None, tm, HEAD_PAIR), row_map))
    return pl.pallas_call(
        functools.partial(_inproj_kernel, rope=rope),
        out_shape=out_shape, grid=(bsz, t // tm), in_specs=in_specs, out_specs=out_specs,
        compiler_params=_cparams(2),
        name="in_projection_lat" if rope else "in_projection_ctx",
    )(*args)


def _scan_kernel(uf_ref, ur_ref, h0f_ref, h0r_ref, bmat_ref, lam_ref, cmat_ref,
                 yf_ref, yr_ref, hf_ref, hr_ref, *, steps):
    @pl.when(pl.program_id(0) == 0)
    def _():
        hf_ref[...] = h0f_ref[...]
        hr_ref[...] = h0r_ref[...]

    piece = steps // SCAN_PIECES

    def drive(u_ref, d, p):
        u = jnp.concatenate([u_ref[:, t, :] for t in range(p * piece, (p + 1) * piece)], axis=0)
        return jnp.dot(u.astype(BF16), bmat_ref[d], preferred_element_type=F32)

    def recur(bu, d, h, descending):
        l_re = jnp.broadcast_to(lam_ref[2 * d:2 * d + 1, :], (SUBLANES, N_STATE))
        l_im = jnp.broadcast_to(lam_ref[2 * d + 1:2 * d + 2, :], (SUBLANES, N_STATE))
        h_re, h_im = h
        states = [None] * piece
        for t in (range(piece - 1, -1, -1) if descending else range(piece)):
            rows = slice(t * SUBLANES, (t + 1) * SUBLANES)
            h_re, h_im = (l_re * h_re - l_im * h_im + bu[rows, 0:N_STATE],
                          l_re * h_im + l_im * h_re + bu[rows, N_STATE:2 * N_STATE])
            states[t] = (h_re, h_im)
        return (h_re, h_im), states

    def readout(states, d, y_ref, p):
        s_re = jnp.concatenate([s[0] for s in states], axis=0).astype(BF16)
        s_im = jnp.concatenate([s[1] for s in states], axis=0).astype(BF16)
        y = jnp.dot(jnp.concatenate([s_re, s_im], axis=1), cmat_ref[d], preferred_element_type=F32)
        for t in range(piece):
            y_ref[:, p * piece + t, :] = y[t * SUBLANES:(t + 1) * SUBLANES, :]

    hf = (hf_ref[:, 0:N_STATE], hf_ref[:, N_STATE:2 * N_STATE])
    hr = (hr_ref[:, 0:N_STATE], hr_ref[:, N_STATE:2 * N_STATE])
    for p in range(SCAN_PIECES):
        q = SCAN_PIECES - 1 - p
        hf, sf = recur(drive(uf_ref, 0, p), 0, hf, False)
        hr, sr = recur(drive(ur_ref, 1, q), 1, hr, True)
        readout(sf, 0, yf_ref, p)
        readout(sr, 1, yr_ref, q)
    hf_ref[:, 0:N_STATE], hf_ref[:, N_STATE:2 * N_STATE] = hf
    hr_ref[:, 0:N_STATE], hr_ref[:, N_STATE:2 * N_STATE] = hr


def _ssm_scan(u, h0f, h0r, bmat, lamv, cmat, steps):
    bsz, t, _ = u.shape
    n = t // steps
    blk = lambda f: pl.BlockSpec((bsz, steps, D_SSM), lambda j: (0, f(j), 0))
    fwd, rev = blk(lambda j: j), blk(lambda j: n - 1 - j)
    carry = jax.ShapeDtypeStruct((SUBLANES, 2 * N_STATE), F32)
    return pl.pallas_call(
        functools.partial(_scan_kernel, steps=steps),
        out_shape=(jax.ShapeDtypeStruct(u.shape, F32),) * 2 + (carry, carry),
        grid=(n,),
        in_specs=[fwd, rev,
                  _const_spec((SUBLANES, 2 * N_STATE)),
                  _const_spec((SUBLANES, 2 * N_STATE)),
                  _const_spec((2, D_SSM, 2 * N_STATE)),
                  _const_spec((4, N_STATE)),
                  _const_spec((2, 2 * N_STATE, D_SSM))],
        out_specs=(fwd, rev,
                   pl.BlockSpec((SUBLANES, 2 * N_STATE), lambda j: (0, 0)),
                   pl.BlockSpec((SUBLANES, 2 * N_STATE), lambda j: (0, 0))),
        compiler_params=_cparams(1),
        name="s5_scan",
    )(u, u, h0f, h0r, bmat, lamv, cmat)


def _attn_kernel(sink_ref, q_ref, *refs, window):
    if window:
        kvp_ref, kvo_ref, kvn_ref, kvc_ref, o_ref = refs
    else:
        kvc_ref, o_ref = refs
    n = pl.program_id(1)
    last = pl.num_programs(1) - 1
    rows = 2 * BLOCK
    lane = lax.broadcasted_iota(jnp.int32, (1, LANES), 1)
    lo = lane < HEAD_DIM
    qi = lax.broadcasted_iota(jnp.int32, (rows, BLOCK), 0) & (BLOCK - 1)
    kj = lax.broadcasted_iota(jnp.int32, (rows, BLOCK), 1)
    upper_head = lax.broadcasted_iota(jnp.int32, (rows, 1), 0) >= BLOCK
    zero = jnp.zeros((), BF16)
    nt = (((1,), (1,)), ((), ()))
    k_cols, v_cols = slice(0, LANES), slice(LANES, 2 * LANES)
    units = range(D_ATTN // LANES)
    qs = []
    for j in units:
        qp = q_ref[:, j * LANES:(j + 1) * LANES]
        qs.append(jnp.concatenate([jnp.where(lo, qp, zero), jnp.where(lo, zero, qp)], axis=0))
    s_all = []
    for j in units:
        scores = []
        if window:
            s = lax.dot_general(qs[j], kvp_ref[:, k_cols], nt, preferred_element_type=F32)
            scores.append(jnp.where((kj >= qi) & (n > 0), s, NEG_INF))
            scores.append(lax.dot_general(qs[j], kvo_ref[:, k_cols], nt, preferred_element_type=F32))
            s = lax.dot_general(qs[j], kvn_ref[:, k_cols], nt, preferred_element_type=F32)
            scores.append(jnp.where((kj <= qi) & (n < last), s, NEG_INF))
        scores.append(lax.dot_general(qs[j], kvc_ref[:, k_cols], nt, preferred_element_type=F32))
        s_all.append(jnp.concatenate(scores, axis=1))
    sinks = [jnp.where(upper_head, sink_ref[N_Q_HEADS // 2 + j], sink_ref[j]) * LOG2E for j in units]
    ms = [jnp.maximum(sinks[j], jnp.max(s_all[j], axis=-1, keepdims=True)) for j in units]
    es = [jnp.exp2(s_all[j] - ms[j]) for j in units]
    denoms = [jnp.exp2(sinks[j] - ms[j]) + jnp.sum(es[j], axis=-1, keepdims=True) for j in units]
    values = [kvp_ref[:, v_cols], kvo_ref[:, v_cols], kvn_ref[:, v_cols]] if window else []
    values = jnp.concatenate(values + [kvc_ref[:, v_cols]], axis=0)
    accs = [jnp.dot(es[j].astype(BF16), values, preferred_element_type=F32) for j in units]
    for j in units:
        out = accs[j] / denoms[j]
        o_ref[:, j * LANES:(j + 1) * LANES] = jnp.where(lo, out[0:BLOCK], out[BLOCK:2 * BLOCK]).astype(BF16)


def _attention(sink, q, kv, kv_ctx, window):
    bsz, t, _ = q.shape
    n_ctx = kv_ctx.shape[1]
    nb = t // BLOCK
    blk = lambda f: pl.BlockSpec((None, BLOCK, HEAD_PAIR), f)
    in_specs = [pl.BlockSpec(memory_space=pltpu.SMEM),
                pl.BlockSpec((None, BLOCK, D_ATTN), lambda b, n: (b, n, 0))]
    args = [sink, q]
    if window:
        prev = lambda b, n: (b, jnp.maximum(n - 1, 0), 0)
        own = lambda b, n: (b, n, 0)
        nxt = lambda b, n: (b, jnp.minimum(n + 1, nb - 1), 0)
        in_specs += [blk(prev), blk(own), blk(nxt)]
        args += [kv, kv, kv]
    in_specs += [pl.BlockSpec((None, n_ctx, HEAD_PAIR), lambda b, n: (b, 0, 0))]
    args += [kv_ctx]
    return pl.pallas_call(
        functools.partial(_attn_kernel, window=window),
        out_shape=jax.ShapeDtypeStruct((bsz, t, D_ATTN), BF16),
        grid=(bsz, nb), in_specs=in_specs,
        out_specs=pl.BlockSpec((None, BLOCK, D_ATTN), lambda b, n: (b, n, 0)),
        compiler_params=_cparams(2),
        name="window_attention" if window else "context_attention",
    )(*args)


MIX_PARTS = 1


def _mixout_kernel(h_ref, mod_ref, cv_ref, cvp_ref, cvn_ref, cb_ref, u_ref, yf_ref, yr_ref, att_ref,
                   convw_ref, d_ref, wglu_ref, bglu_ref, wout_ref, g2_ref, rw_ref, rb_ref,
                   hmid_ref, t_ref, gates_ref):
    i = pl.program_id(1)
    last = pl.num_programs(1) - 1
    tm = cv_ref.shape[0]
    cv = cv_ref[...]
    row = lax.broadcasted_iota(jnp.int32, (tm, 1), 0)
    before = jnp.where(i > 0, cvp_ref[SUBLANES - 1:SUBLANES, :], 0.0)
    after = jnp.where(i < last, cvn_ref[0:1, :], 0.0)
    down = jnp.where(row == 0, before, pltpu.roll(cv, 1, 0))
    up = jnp.where(row == tm - 1, after, pltpu.roll(cv, tm - 1, 0))
    w = convw_ref[...]
    y_conv = cb_ref[...] * (w[0:1, :] * down + w[1:2, :] * cv + w[2:3, :] * up)
    mod = mod_ref[...]
    nt = (((1,), (1,)), ((), ()))
    eidx = lax.broadcasted_iota(jnp.int32, (N_EXPERTS, 1), 0).astype(F32)

    def part(rows):
        y = d_ref[...] * u_ref[rows, :] + yf_ref[rows, :] + yr_ref[rows, :]
        gl = jax.nn.gelu(y)
        z = jnp.dot(gl.astype(BF16), wglu_ref[...], preferred_element_type=F32) + bglu_ref[...]
        y_ssm = gl * jax.nn.sigmoid(z)
        mix = (jnp.dot(y_conv[rows, :].astype(BF16), wout_ref[0:D_CONV, :], preferred_element_type=F32)
               + jnp.dot(y_ssm.astype(BF16), wout_ref[D_CONV:D_CONV + D_SSM, :], preferred_element_type=F32)
               + jnp.dot(att_ref[rows, :], wout_ref[D_CONV + D_SSM:, :], preferred_element_type=F32))
        h_mid = h_ref[rows, :] + mod[:, 2 * D_MODEL:3 * D_MODEL] * mix
        t = _rms_modulate(h_mid, g2_ref[...], mod[:, 3 * D_MODEL:4 * D_MODEL], mod[:, 4 * D_MODEL:5 * D_MODEL])
        t_hi = t.astype(BF16)
        t_lo = (t - t_hi.astype(F32)).astype(BF16)
        a = lax.dot_general(rw_ref[...], t_hi, nt, preferred_element_type=F32)
        b = lax.dot_general(rw_ref[...], t_lo, nt, preferred_element_type=F32)
        logits = a[0:N_EXPERTS] + a[N_EXPERTS:2 * N_EXPERTS] + b[0:N_EXPERTS]
        scores = jax.nn.sigmoid(logits)
        biased = scores + rb_ref[...]
        sel = jnp.zeros(scores.shape, F32)
        for _ in range(TOP_K):
            best = jnp.max(biased, axis=0, keepdims=True)
            first = jnp.min(jnp.where(biased == best, eidx, float(N_EXPERTS)), axis=0, keepdims=True)
            pick = eidx == first
            sel = jnp.where(pick, scores, sel)
            biased = jnp.where(pick, -jnp.inf, biased)
        gates = sel / jnp.sum(sel, axis=0, keepdims=True) * ROUTED_SCALE
        padded = jnp.concatenate([gates, jnp.zeros((LANES - N_EXPERTS, gates.shape[1]), F32)], axis=0)
        return h_mid, t_hi, padded.T

    parts = [slice(p * tm // MIX_PARTS, (p + 1) * tm // MIX_PARTS) for p in range(MIX_PARTS)]
    results = [part(rows) for rows in parts]
    for rows, (h_mid, t_hi, gates) in zip(parts, results):
        hmid_ref[rows, :] = h_mid
        t_ref[rows, :] = t_hi
        gates_ref[rows, :] = gates


def _mixer_output(h, mod_l, mod_row, cv, cb, u, yf, yr, att,
                  conv_w, ssm_d, w_glu, b_glu, w_out, norm_g, router_w, router_b, tm):
    bsz, t, d = h.shape
    halo = tm // SUBLANES
    n_halo = t // SUBLANES
    row_map = lambda b, i: (b, i, 0)
    u_spec = pl.BlockSpec((None, tm, D_SSM), row_map)
    in_specs = [pl.BlockSpec((None, tm, d), row_map),
                pl.BlockSpec((None, 1, 6 * d), lambda b, i: (mod_row(b), 0, 0)),
                pl.BlockSpec((None, tm, D_CONV), row_map),
                pl.BlockSpec((None, SUBLANES, D_CONV), lambda b, i: (b, jnp.maximum(i * halo - 1, 0), 0)),
                pl.BlockSpec((None, SUBLANES, D_CONV), lambda b, i: (b, jnp.minimum((i + 1) * halo, n_halo - 1), 0)),
                pl.BlockSpec((None, tm, D_CONV), row_map),
                u_spec, u_spec, u_spec,
                pl.BlockSpec((None, tm, D_ATTN), row_map),
                _const_spec((3, D_CONV)),
                _const_spec((1, D_SSM)),
                _const_spec((D_SSM, D_SSM)),
                _const_spec((1, D_SSM)),
                _const_spec((d, d)),
                _const_spec((1, d)),
                _const_spec((LANES, d)),
                _const_spec((N_EXPERTS, 1))]
    out_shape = (jax.ShapeDtypeStruct((bsz, t, d), F32),
                 jax.ShapeDtypeStruct((bsz, t, d), BF16),
                 jax.ShapeDtypeStruct((bsz, t, LANES), F32))
    out_specs = (pl.BlockSpec((None, tm, d), row_map),
                 pl.BlockSpec((None, tm, d), row_map),
                 pl.BlockSpec((None, tm, LANES), row_map))
    return pl.pallas_call(
        _mixout_kernel, out_shape=out_shape, grid=(bsz, t // tm),
        in_specs=in_specs, out_specs=out_specs,
        compiler_params=_cparams(2),
        name="mixer_output",
    )(h, mod_l, cv, cv, cv, cb, u, yf, yr, att,
      conv_w, ssm_d, w_glu, b_glu, w_out, norm_g, router_w, router_b)


EXPERTS_PER_CHUNK = 4


def _moe_kernel(t_ref, gates_ref, hmid_ref, mod_ref, wg_ref, wu_ref, wd_ref,
                wsg_ref, wsu_ref, wsd_ref, o_ref):
    t = t_ref[...]
    gates = gates_ref[...]
    hs = _silu(jnp.dot(t, wsg_ref[...], preferred_element_type=F32)) * jnp.dot(
        t, wsu_ref[...], preferred_element_type=F32)
    acc = jnp.dot(hs.astype(BF16), wsd_ref[...], preferred_element_type=F32)
    width = EXPERTS_PER_CHUNK * D_EXPERT
    n_chunks = N_EXPERTS // EXPERTS_PER_CHUNK

    def up(c):
        cols = slice(c * width, (c + 1) * width)
        return (jnp.dot(t, wg_ref[:, cols], preferred_element_type=F32),
                jnp.dot(t, wu_ref[:, cols], preferred_element_type=F32))

    nxt = up(0)
    for c in range(n_chunks):
        hg, hu = nxt
        if c + 1 < n_chunks:
            nxt = up(c + 1)
        hid = _silu(hg) * hu
        gated = []
        for e in range(EXPERTS_PER_CHUNK):
            k = c * EXPERTS_PER_CHUNK + e
            gated.append((hid[:, e * D_EXPERT:(e + 1) * D_EXPERT] * gates[:, k:k + 1]).astype(BF16))
        acc = acc + jnp.dot(jnp.concatenate(gated, axis=1), wd_ref[c * width:(c + 1) * width, :],
                            preferred_element_type=F32)
    mod = mod_ref[...]
    o_ref[...] = hmid_ref[...] + mod[:, 5 * D_MODEL:6 * D_MODEL] * acc


def _moe(t, gates, hmid, mod_l, mod_row, wg, wu, wd, wsg, wsu, wsd, tm):
    bsz, n, d = hmid.shape
    row_map = lambda b, i: (b, i, 0)
    return pl.pallas_call(
        _moe_kernel,
        out_shape=jax.ShapeDtypeStruct((bsz, n, d), F32),
        grid=(bsz, n // tm),
        in_specs=[pl.BlockSpec((None, tm, d), row_map),
                  pl.BlockSpec((None, tm, LANES), row_map),
                  pl.BlockSpec((None, tm, d), row_map),
                  pl.BlockSpec((None, 1, 6 * d), lambda b, i: (mod_row(b), 0, 0)),
                  _const_spec(wg.shape), _const_spec(wu.shape), _const_spec(wd.shape),
                  _const_spec(wsg.shape), _const_spec(wsu.shape), _const_spec(wsd.shape)],
        out_specs=pl.BlockSpec((None, tm, d), row_map),
        compiler_params=_cparams(2),
        name="moe",
    )(t, gates, hmid, mod_l, wg, wu, wd, wsg, wsu, wsd)


def _rope_tables(n_tokens):
    rows = n_tokens // GRID_W
    row = np.repeat(np.arange(rows, dtype=np.float64), GRID_W)
    col = np.tile(np.arange(GRID_W, dtype=np.float64), rows)
    inv_freq = ROPE_BASE ** (-np.arange(AXIS_PAIRS, dtype=np.float64) / AXIS_PAIRS)
    ar, ac = row[:, None] * inv_freq, col[:, None] * inv_freq
    cos = np.concatenate([np.cos(ar), np.cos(ar), np.cos(ac), np.cos(ac)], axis=1).astype(np.float32)
    sin = np.concatenate([-np.sin(ar), np.sin(ar), -np.sin(ac), np.sin(ac)], axis=1).astype(np.float32)
    reps = (1, HEAD_PAIR // HEAD_DIM)
    return jnp.tile(jnp.asarray(cos), reps), jnp.tile(jnp.asarray(sin), reps)


def _ssm_params(lam_re, lam_im, log_dt, b_re, b_im, c_re, c_im):
    lead = lam_re.shape[:2]
    lr, li = lam_re.astype(F32), lam_im.astype(F32)
    dt = jnp.exp(log_dt.astype(F32))[..., None]
    mag = jnp.exp(lr * dt)
    ar, ai = mag * jnp.cos(li * dt), mag * jnp.sin(li * dt)
    den = lr * lr + li * li
    qr = ((ar - 1.0) * lr + ai * li) / den
    qi = (ai * lr - (ar - 1.0) * li) / den
    br, bi = b_re.astype(F32), b_im.astype(F32)
    bbar_re = qr[..., None] * br - qi[..., None] * bi
    bbar_im = qr[..., None] * bi + qi[..., None] * br
    eye = jnp.eye(SSM_GROUPS, dtype=F32)
    drive = lambda m: jnp.einsum('gk,ldgph->ldghkp', eye, m).reshape(lead + (D_SSM, N_STATE))
    read = lambda m: jnp.einsum('gk,ldghp->ldgpkh', eye, m.astype(F32)).reshape(lead + (N_STATE, D_SSM))
    bmat = jnp.concatenate([drive(bbar_re), drive(bbar_im)], axis=-1)
    cmat = jnp.concatenate([read(c_re), -read(c_im)], axis=-2)
    lamv = jnp.stack([ar, ai], axis=2).reshape(lead[0], 4, N_STATE)
    return bmat.astype(BF16), lamv, cmat.astype(BF16)


def _pair_heads(w, axis):
    shape = w.shape
    split = shape[:axis] + (N_KV_HEADS, N_Q_HEADS // N_KV_HEADS, HEAD_DIM) + shape[axis + 1:]
    return jnp.swapaxes(w.reshape(split), axis, axis + 1).reshape(shape)


def kernel(x, c, ctx, c_ctx, w_mod, b_mod, norm1_g, norm2_g, w_in, conv_w, ssm_lam_re, ssm_lam_im, ssm_log_dt, ssm_b_re, ssm_b_im, ssm_c_re, ssm_c_im, ssm_d, w_glu, b_glu, q_norm_g, k_norm_g, attn_sink, w_out, router_w, router_bias, w_exp_gate, w_exp_up, w_exp_down, w_sh_gate, w_sh_up, w_sh_down):
    bsz, n_lat, d = x.shape
    n_ctx = ctx.shape[1]
    depth = w_mod.shape[0]
    assert bsz == SUBLANES and d == D_MODEL

    mod_rows = 2 * SUBLANES
    cvec = jnp.zeros((mod_rows, d), F32).at[:bsz].set(c).at[bsz].set(c_ctx)
    mod = _modulation(cvec, w_mod, b_mod).reshape(depth, mod_rows, 1, 6 * d)
    lat_row = lambda b: b
    ctx_row = lambda b: bsz
    rope_tabs = _rope_tables(n_lat)

    tm_lat, tm_ctx, scan_steps = 512, 256, 128
    q0, kv0 = 3 * D_CONV + D_SSM, 3 * D_CONV + D_SSM + D_ATTN
    bmat, lamv, cmat = _ssm_params(ssm_lam_re, ssm_lam_im, ssm_log_dt, ssm_b_re, ssm_b_im, ssm_c_re, ssm_c_im)
    h, hc = x, ctx
    for l in range(depth):
        ctx_out = l < depth - 1
        mod_l = mod[l]
        w_in2 = jnp.concatenate([w_in[l][:, :q0], _pair_heads(w_in[l][:, q0:kv0], 1), w_in[l][:, kv0:]],
                                axis=1).astype(BF16)
        qg = jnp.tile(q_norm_g[l].astype(F32), HEAD_PAIR // HEAD_DIM)[None, :]
        kg = jnp.concatenate([jnp.tile(k_norm_g[l].astype(F32), N_KV_HEADS),
                              jnp.ones((N_KV_HEADS * HEAD_DIM,), F32)])[None, :]
        n1 = norm1_g[l][None, :]
        cv, cb, u_lat, q, kv = _in_projection(h, mod_l, lat_row, n1, w_in2, qg, kg, rope_tabs, tm_lat)
        cvc, cbc, u_ctx, qc, kvc = _in_projection(hc, mod_l, ctx_row, n1, w_in2, qg, kg, None, tm_ctx)
        zero_state = jnp.zeros((bsz, 2 * N_STATE), F32)
        yfc, yrc, hf, hr = _ssm_scan(u_ctx, zero_state, zero_state, bmat[l], lamv[l], cmat[l], scan_steps)
        yf, yr, _, _ = _ssm_scan(u_lat, hf, hr, bmat[l], lamv[l], cmat[l], scan_steps)

        sink = attn_sink[l].astype(F32)
        att = _attention(sink, q, kv, kvc, window=True)

        w_out_l = jnp.concatenate([w_out[l][:D_CONV + D_SSM], _pair_heads(w_out[l][D_CONV + D_SSM:], 0)], axis=0)
        rw = router_w[l].astype(F32)
        rw_hi = rw.astype(BF16)
        rw_lo = (rw - rw_hi.astype(F32)).astype(BF16)
        rw2 = jnp.concatenate([rw_hi.T, rw_lo.T, jnp.zeros((LANES - 2 * N_EXPERTS, d), BF16)], axis=0)
        rb = router_bias[l].astype(F32)[:, None]
        post = dict(conv_w=conv_w[l], ssm_d=ssm_d[l][None, :], w_glu=w_glu[l].astype(BF16),
                    b_glu=b_glu[l][None, :], w_out=w_out_l.astype(BF16), norm_g=norm2_g[l][None, :],
                    router_w=rw2, router_b=rb)
        experts = (w_exp_gate[l].astype(BF16).reshape(d, -1), w_exp_up[l].astype(BF16).reshape(d, -1),
                   w_exp_down[l].astype(BF16).reshape(-1, d), w_sh_gate[l].astype(BF16),
                   w_sh_up[l].astype(BF16), w_sh_down[l].astype(BF16))
        hmid, t, gates = _mixer_output(h, mod_l, lat_row, cv, cb, u_lat, yf, yr, att, tm=tm_lat, **post)
        h_new = _moe(t, gates, hmid, mod_l, lat_row, *experts, tm=tm_lat)
        if ctx_out:
            attc = _attention(sink, qc, None, kvc, window=False)
            hmid_c, tc, gates_c = _mixer_output(hc, mod_l, ctx_row, cvc, cbc, u_ctx, yfc, yrc, attc,
                                                tm=tm_ctx, **post)
            hc = _moe(tc, gates_c, hmid_c, mod_l, ctx_row, *experts, tm=tm_ctx)
        h = h_new
    return h
```

```python
import functools
import math

import jax
import jax.numpy as jnp
import numpy as np
from jax import lax
from jax.experimental import pallas as pl
from jax.experimental.pallas import tpu as pltpu

F32 = jnp.float32
BF16 = jnp.bfloat16

D_MODEL = 1024
D_CONV = 256
D_SSM = 256
SSM_GROUP = 16
SSM_GROUPS = 16
SSM_STATE = 64
N_STATE = SSM_GROUPS * SSM_STATE
HEAD_DIM = 64
D_ATTN = 512
N_Q_HEADS = 8
N_KV_HEADS = 2
AXIS_PAIRS = 16
ROPE_BASE = 10000.0
GRID_W = 64
BLOCK = 128
N_EXPERTS = 32
TOP_K = 4
D_EXPERT = 128
ROUTED_SCALE = 2.5
NORM_EPS = 1e-6
NEG_INF = -1e30
D_PROJ = 3 * D_CONV + D_SSM + D_ATTN + 2 * N_KV_HEADS * HEAD_DIM
LANES = 128
SUBLANES = 8
MXU_TILE = 256
SCAN_PIECES = 4
VMEM_LIMIT = 56 * 1024 * 1024


def _cparams(n_axes):
    return pltpu.CompilerParams(dimension_semantics=("arbitrary",) * n_axes,
                                vmem_limit_bytes=VMEM_LIMIT)


def _const_spec(shape):
    nd = len(shape)
    return pl.BlockSpec(shape, lambda *_: (0,) * nd, pipeline_mode=pl.Buffered(1))


def _silu(x):
    return x * jax.nn.sigmoid(x)


def _rms_modulate(x, gain, shift, scale):
    ms = jnp.mean(x * x, axis=-1, keepdims=True)
    return (x * lax.rsqrt(ms + NORM_EPS) * gain) * (1.0 + scale) + shift


def _mod_kernel(c_ref, w_ref, b_ref, o_ref):
    cv = c_ref[...]
    s = _silu(cv)
    s_hi = s.astype(BF16)
    s_lo = (s - s_hi.astype(F32)).astype(BF16)
    w = w_ref[...]
    w_hi = w.astype(BF16)
    w_lo = (w - w_hi.astype(F32)).astype(BF16)
    rows = cv.shape[0]
    a = jnp.dot(jnp.concatenate([s_hi, s_lo], axis=0), w_hi, preferred_element_type=F32)
    b = jnp.dot(s_hi, w_lo, preferred_element_type=F32)
    o_ref[...] = a[0:rows] + a[rows:2 * rows] + b + b_ref[...]


def _modulation(cvec, w_mod, b_mod):
    depth, d, n = w_mod.shape
    rows = cvec.shape[0]
    tn = 1536
    return pl.pallas_call(
        _mod_kernel,
        out_shape=jax.ShapeDtypeStruct((depth, rows, n), F32),
        grid=(depth, n // tn),
        in_specs=[pl.BlockSpec((rows, d), lambda l, j: (0, 0)),
                  pl.BlockSpec((None, d, tn), lambda l, j: (l, 0, j)),
                  pl.BlockSpec((None, 1, tn), lambda l, j: (l, 0, j))],
        out_specs=pl.BlockSpec((None, rows, tn), lambda l, j: (l, 0, j)),
        compiler_params=_cparams(2),
        name="modulation",
    )(cvec, w_mod, b_mod.reshape(depth, 1, n))


HEAD_PAIR = 4 * HEAD_DIM
LOG2E = math.log2(math.e)
Q_SCALE = HEAD_DIM ** -0.5 * LOG2E


def _inproj_kernel(*refs, rope):
    if rope:
        (x_ref, mod_ref, g_ref, w_ref, qg_ref, kg_ref, ones_ref, perm_ref, cos_ref, sin_ref,
         cv_ref, cb_ref, u_ref, q_ref, kv_ref) = refs
    else:
        (x_ref, mod_ref, g_ref, w_ref, qg_ref, kg_ref, ones_ref,
         cv_ref, cb_ref, u_ref, q_ref, kv_ref) = refs
    mod = mod_ref[...]
    mb = _rms_modulate(x_ref[...], g_ref[...], mod[:, 0:D_MODEL], mod[:, D_MODEL:2 * D_MODEL]).astype(BF16)

    def proj(c0, width):
        return jnp.dot(mb, w_ref[:, c0:c0 + width], preferred_element_type=F32)

    conv_v = proj(2 * D_CONV, D_CONV) * proj(0, D_CONV)
    conv_b = proj(D_CONV, D_CONV)
    u = proj(3 * D_CONV, D_SSM)
    q0 = 3 * D_CONV + D_SSM
    kv0 = q0 + D_ATTN
    blocks = [(q0 + jj * HEAD_PAIR, qg_ref, q_ref, jj * HEAD_PAIR, Q_SCALE, False)
              for jj in range(D_ATTN // HEAD_PAIR)] + [(kv0, kg_ref, kv_ref, 0, 1.0, True)]
    is_k = lax.broadcasted_iota(jnp.int32, (1, HEAD_PAIR), 1) < N_KV_HEADS * HEAD_DIM
    xs = [proj(blk[0], HEAD_PAIR) for blk in blocks]
    ssqs = [jnp.dot((x * x).astype(BF16), ones_ref[...], preferred_element_type=F32) for x in xs]
    norms = [lax.rsqrt(ssq * (1.0 / HEAD_DIM) + NORM_EPS) for ssq in ssqs]
    norms = [jnp.where(is_k, nm, 1.0) if blk[5] else nm for nm, blk in zip(norms, blocks)]
    xns = [x * nm * blk[1][...] for x, nm, blk in zip(xs, norms, blocks)]
    if rope:
        partners = [jnp.dot(xn.astype(BF16), perm_ref[...], preferred_element_type=F32) for xn in xns]
        cos, sin = cos_ref[...], sin_ref[...]
        xns = [xn * (jnp.where(is_k, cos, 1.0) if blk[5] else cos) + pt * (jnp.where(is_k, sin, 0.0) if blk[5] else sin)
               for xn, pt, blk in zip(xns, partners, blocks)]
    for xn, (_, _, out_ref, o0, scale, _) in zip(xns, blocks):
        if scale != 1.0:
            xn = xn * scale
        out_ref[:, o0:o0 + HEAD_PAIR] = xn.astype(BF16)
    cv_ref[...] = conv_v
    cb_ref[...] = conv_b
    u_ref[...] = u


def _in_projection(h, mod_l, mod_row, norm_g, w_in2, qg, kg, rope_tabs, tm):
    bsz, t, d = h.shape
    rope = rope_tabs is not None
    row_map = lambda b, i: (b, i, 0)
    in_specs = [pl.BlockSpec((None, tm, d), row_map),
                pl.BlockSpec((None, 1, 6 * d), lambda b, i: (mod_row(b), 0, 0)),
                _const_spec((1, d)),
                _const_spec((d, D_PROJ)),
                _const_spec((1, HEAD_PAIR)),
                _const_spec((1, HEAD_PAIR)),
                _const_spec((HEAD_PAIR, HEAD_PAIR))]
    lane = jnp.arange(HEAD_PAIR)
    head_ones = (lane[:, None] // HEAD_DIM == lane[None, :] // HEAD_DIM).astype(BF16)
    args = [h, mod_l, norm_g, w_in2, qg, kg, head_ones]
    if rope:
        swap = (lane[:, None] == (lane[None, :] ^ AXIS_PAIRS)).astype(BF16)
        in_specs += [_const_spec((HEAD_PAIR, HEAD_PAIR))]
        in_specs += [pl.BlockSpec((tm, HEAD_PAIR), lambda b, i: (i, 0))] * 2
        args += [swap] + list(rope_tabs)
    out_shape = (jax.ShapeDtypeStruct((bsz, t, D_CONV), F32),
                 jax.ShapeDtypeStruct((bsz, t, D_CONV), F32),
                 jax.ShapeDtypeStruct((bsz, t, D_SSM), F32),
                 jax.ShapeDtypeStruct((bsz, t, D_ATTN), BF16),
                 jax.ShapeDtypeStruct((bsz, t, HEAD_PAIR), BF16))
    out_specs = (pl.BlockSpec((None, tm, D_CONV), row_map),
                 pl.BlockSpec((None, tm, D_CONV), row_map),
                 pl.BlockSpec((None, tm, D_SSM), row_map),
                 pl.BlockSpec((None, tm, D_ATTN), row_map),
                 pl.BlockSpec((None, tm, HEAD_PAIR), row_map))
    return pl.pallas_call(
        functools.partial(_inproj_kernel, rope=rope),
        out_shape=out_shape, grid=(bsz, t // tm), in_specs=in_specs, out_specs=out_specs,
        compiler_params=_cparams(2),
        name="in_projection_lat" if rope else "in_projection_ctx",
    )(*args)


def _scan_kernel(uf_ref, ur_ref, h0f_ref, h0r_ref, bmat_ref, lam_ref, cmat_ref,
                 yf_ref, yr_ref, hf_ref, hr_ref, *, steps):
    @pl.when(pl.program_id(0) == 0)
    def _():
        hf_ref[...] = h0f_ref[...]
        hr_ref[...] = h0r_ref[...]

    piece = steps // SCAN_PIECES

    def drive(u_ref, d, p):
        u = jnp.concatenate([u_ref[:, t, :] for t in range(p * piece, (p + 1) * piece)], axis=0)
        return jnp.dot(u.astype(BF16), bmat_ref[d], preferred_element_type=F32)

    def recur(bu, d, h, descending):
        l_re = jnp.broadcast_to(lam_ref[2 * d:2 * d + 1, :], (SUBLANES, N_STATE))
        l_im = jnp.broadcast_to(lam_ref[2 * d + 1:2 * d + 2, :], (SUBLANES, N_STATE))
        h_re, h_im = h
        states = [None] * piece
        for t in (range(piece - 1, -1, -1) if descending else range(piece)):
            rows = slice(t * SUBLANES, (t + 1) * SUBLANES)
            h_re, h_im = (l_re * h_re - l_im * h_im + bu[rows, 0:N_STATE],
                          l_re * h_im + l_im * h_re + bu[rows, N_STATE:2 * N_STATE])
            states[t] = (h_re, h_im)
        return (h_re, h_im), states

    def readout(states, d, y_ref, p):
        s_re = jnp.concatenate([s[0] for s in states], axis=0).astype(BF16)
        s_im = jnp.concatenate([s[1] for s in states], axis=0).astype(BF16)
        y = jnp.dot(jnp.concatenate([s_re, s_im], axis=1), cmat_ref[d], preferred_element_type=F32)
        for t in range(piece):
            y_ref[:, p * piece + t, :] = y[t * SUBLANES:(t + 1) * SUBLANES, :]

    hf = (hf_ref[:, 0:N_STATE], hf_ref[:, N_STATE:2 * N_STATE])
    hr = (hr_ref[:, 0:N_STATE], hr_ref[:, N_STATE:2 * N_STATE])
    for p in range(SCAN_PIECES):
        q = SCAN_PIECES - 1 - p
        hf, sf = recur(drive(uf_ref, 0, p), 0, hf, False)
        hr, sr = recur(drive(ur_ref, 1, q), 1, hr, True)
        readout(sf, 0, yf_ref, p)
        readout(sr, 1, yr_ref, q)
    hf_ref[:, 0:N_STATE], hf_ref[:, N_STATE:2 * N_STATE] = hf
    hr_ref[:, 0:N_STATE], hr_ref[:, N_STATE:2 * N_STATE] = hr


def _ssm_scan(u, h0f, h0r, bmat, lamv, cmat, steps):
    bsz, t, _ = u.shape
    n = t // steps
    blk = lambda f: pl.BlockSpec((bsz, steps, D_SSM), lambda j: (0, f(j), 0))
    fwd, rev = blk(lambda j: j), blk(lambda j: n - 1 - j)
    carry = jax.ShapeDtypeStruct((SUBLANES, 2 * N_STATE), F32)
    return pl.pallas_call(
        functools.partial(_scan_kernel, steps=steps),
        out_shape=(jax.ShapeDtypeStruct(u.shape, F32),) * 2 + (carry, carry),
        grid=(n,),
        in_specs=[fwd, rev,
                  _const_spec((SUBLANES, 2 * N_STATE)),
                  _const_spec((SUBLANES, 2 * N_STATE)),
                  _const_spec((2, D_SSM, 2 * N_STATE)),
                  _const_spec((4, N_STATE)),
                  _const_spec((2, 2 * N_STATE, D_SSM))],
        out_specs=(fwd, rev,
                   pl.BlockSpec((SUBLANES, 2 * N_STATE), lambda j: (0, 0)),
                   pl.BlockSpec((SUBLANES, 2 * N_STATE), lambda j: (0, 0))),
        compiler_params=_cparams(1),
        name="s5_scan",
    )(u, u, h0f, h0r, bmat, lamv, cmat)


MAX_Q_SUB = 4


def _attn_kernel(sink_ref, q_ref, *refs, window):
    q_sub = q_ref.shape[0] // BLOCK
    if window:
        kv_refs, kvc_ref, o_ref = refs[:q_sub + 2], refs[q_sub + 2], refs[q_sub + 3]
    else:
        kvc_ref, o_ref = refs
    n = pl.program_id(1)
    last_block = pl.num_programs(1) * q_sub - 1
    rows = 2 * BLOCK
    lane = lax.broadcasted_iota(jnp.int32, (1, LANES), 1)
    lo = lane < HEAD_DIM
    qi = lax.broadcasted_iota(jnp.int32, (rows, BLOCK), 0) & (BLOCK - 1)
    kj = lax.broadcasted_iota(jnp.int32, (rows, BLOCK), 1)
    upper_head = lax.broadcasted_iota(jnp.int32, (rows, 1), 0) >= BLOCK
    zero = jnp.zeros((), BF16)
    nt = (((1,), (1,)), ((), ()))
    k_cols, v_cols = slice(0, LANES), slice(LANES, 2 * LANES)
    units = [(a, j) for a in range(q_sub) for j in range(D_ATTN // LANES)]
    qs = []
    for a, j in units:
        qp = q_ref[a * BLOCK:(a + 1) * BLOCK, j * LANES:(j + 1) * LANES]
        qs.append(jnp.concatenate([jnp.where(lo, qp, zero), jnp.where(lo, zero, qp)], axis=0))
    s_all = []
    for (a, j), q in zip(units, qs):
        scores = []
        if window:
            block = n * q_sub + a
            s = lax.dot_general(q, kv_refs[a][:, k_cols], nt, preferred_element_type=F32)
            scores.append(jnp.where((kj >= qi) & (block > 0), s, NEG_INF))
            scores.append(lax.dot_general(q, kv_refs[a + 1][:, k_cols], nt, preferred_element_type=F32))
            s = lax.dot_general(q, kv_refs[a + 2][:, k_cols], nt, preferred_element_type=F32)
            scores.append(jnp.where((kj <= qi) & (block < last_block), s, NEG_INF))
        scores.append(lax.dot_general(q, kvc_ref[:, k_cols], nt, preferred_element_type=F32))
        s_all.append(jnp.concatenate(scores, axis=1))
    sinks = [jnp.where(upper_head, sink_ref[N_Q_HEADS // 2 + j], sink_ref[j]) * LOG2E for _, j in units]
    ms = [jnp.maximum(sk, jnp.max(s, axis=-1, keepdims=True)) for sk, s in zip(sinks, s_all)]
    es = [jnp.exp2(s - m) for s, m in zip(s_all, ms)]
    denoms = [jnp.exp2(sk - m) + jnp.sum(e, axis=-1, keepdims=True) for sk, m, e in zip(sinks, ms, es)]
    accs = []
    for (a, j), e in zip(units, es):
        values = [kv_refs[a + w][:, v_cols] for w in range(3)] if window else []
        values = jnp.concatenate(values + [kvc_ref[:, v_cols]], axis=0)
        accs.append(jnp.dot(e.astype(BF16), values, preferred_element_type=F32))
    for (a, j), acc, denom in zip(units, accs, denoms):
        out = acc / denom
        o_ref[a * BLOCK:(a + 1) * BLOCK, j * LANES:(j + 1) * LANES] = jnp.where(
            lo, out[0:BLOCK], out[BLOCK:2 * BLOCK]).astype(BF16)


def _attention(sink, q, kv, kv_ctx, window):
    bsz, t, _ = q.shape
    n_ctx = kv_ctx.shape[1]
    nb = t // BLOCK
    q_sub = min(MAX_Q_SUB, nb)
    tq = q_sub * BLOCK
    in_specs = [pl.BlockSpec(memory_space=pltpu.SMEM),
                pl.BlockSpec((None, tq, D_ATTN), lambda b, n: (b, n, 0))]
    args = [sink, q]
    if window:
        for w in range(q_sub + 2):
            in_specs.append(pl.BlockSpec(
                (None, BLOCK, HEAD_PAIR),
                lambda b, n, w=w: (b, jnp.clip(n * q_sub + w - 1, 0, nb - 1), 0)))
            args.append(kv)
    in_specs += [pl.BlockSpec((None, n_ctx, HEAD_PAIR), lambda b, n: (b, 0, 0))]
    args += [kv_ctx]
    return pl.pallas_call(
        functools.partial(_attn_kernel, window=window),
        out_shape=jax.ShapeDtypeStruct((bsz, t, D_ATTN), BF16),
        grid=(bsz, t // tq), in_specs=in_specs,
        out_specs=pl.BlockSpec((None, tq, D_ATTN), lambda b, n: (b, n, 0)),
        compiler_params=_cparams(2),
        name="window_attention" if window else "context_attention",
    )(*args)


MIX_PARTS = 1


def _mixout_kernel(h_ref, mod_ref, cv_ref, cvp_ref, cvn_ref, cb_ref, u_ref, yf_ref, yr_ref, att_ref,
                   convw_ref, d_ref, wglu_ref, bglu_ref, wout_ref, g2_ref, rw_ref, rb_ref,
                   hmid_ref, t_ref, gates_ref):
    i = pl.program_id(1)
    last = pl.num_programs(1) - 1
    tm = cv_ref.shape[0]
    cv = cv_ref[...]
    row = lax.broadcasted_iota(jnp.int32, (tm, 1), 0)
    before = jnp.where(i > 0, cvp_ref[SUBLANES - 1:SUBLANES, :], 0.0)
    after = jnp.where(i < last, cvn_ref[0:1, :], 0.0)
    down = jnp.where(row == 0, before, pltpu.roll(cv, 1, 0))
    up = jnp.where(row == tm - 1, after, pltpu.roll(cv, tm - 1, 0))
    w = convw_ref[...]
    y_conv = cb_ref[...] * (w[0:1, :] * down + w[1:2, :] * cv + w[2:3, :] * up)
    mod = mod_ref[...]
    nt = (((1,), (1,)), ((), ()))
    eidx = lax.broadcasted_iota(jnp.int32, (N_EXPERTS, 1), 0).astype(F32)

    def part(rows):
        y = d_ref[...] * u_ref[rows, :] + yf_ref[rows, :] + yr_ref[rows, :]
        gl = jax.nn.gelu(y)
        z = jnp.dot(gl.astype(BF16), wglu_ref[...], preferred_element_type=F32) + bglu_ref[...]
        y_ssm = gl * jax.nn.sigmoid(z)
        mix = (jnp.dot(y_conv[rows, :].astype(BF16), wout_ref[0:D_CONV, :], preferred_element_type=F32)
               + jnp.dot(y_ssm.astype(BF16), wout_ref[D_CONV:D_CONV + D_SSM, :], preferred_element_type=F32)
               + jnp.dot(att_ref[rows, :], wout_ref[D_CONV + D_SSM:, :], preferred_element_type=F32))
        h_mid = h_ref[rows, :] + mod[:, 2 * D_MODEL:3 * D_MODEL] * mix
        t = _rms_modulate(h_mid, g2_ref[...], mod[:, 3 * D_MODEL:4 * D_MODEL], mod[:, 4 * D_MODEL:5 * D_MODEL])
        t_hi = t.astype(BF16)
        t_lo = (t - t_hi.astype(F32)).astype(BF16)
        a = lax.dot_general(rw_ref[...], t_hi, nt, preferred_element_type=F32)
        b = lax.dot_general(rw_ref[...], t_lo, nt, preferred_element_type=F32)
        logits = a[0:N_EXPERTS] + a[N_EXPERTS:2 * N_EXPERTS] + b[0:N_EXPERTS]
        scores = jax.nn.sigmoid(logits)
        biased = scores + rb_ref[...]
        sel = jnp.zeros(scores.shape, F32)
        for _ in range(TOP_K):
            best = jnp.max(biased, axis=0, keepdims=True)
            first = jnp.min(jnp.where(biased == best, eidx, float(N_EXPERTS)), axis=0, keepdims=True)
            pick = eidx == first
            sel = jnp.where(pick, scores, sel)
            biased = jnp.where(pick, -jnp.inf, biased)
        gates = sel / jnp.sum(sel, axis=0, keepdims=True) * ROUTED_SCALE
        padded = jnp.concatenate([gates, jnp.zeros((LANES - N_EXPERTS, gates.shape[1]), F32)], axis=0)
        return h_mid, t_hi, padded.T

    parts = [slice(p * tm // MIX_PARTS, (p + 1) * tm // MIX_PARTS) for p in range(MIX_PARTS)]
    results = [part(rows) for rows in parts]
    for rows, (h_mid, t_hi, gates) in zip(parts, results):
        hmid_ref[rows, :] = h_mid
        t_ref[rows, :] = t_hi
        gates_ref[rows, :] = gates


def _mixer_output(h, mod_l, mod_row, cv, cb, u, yf, yr, att,
                  conv_w, ssm_d, w_glu, b_glu, w_out, norm_g, router_w, router_b, tm):
    bsz, t, d = h.shape
    halo = tm // SUBLANES
    n_halo = t // SUBLANES
    row_map = lambda b, i: (b, i, 0)
    u_spec = pl.BlockSpec((None, tm, D_SSM), row_map)
    in_specs = [pl.BlockSpec((None, tm, d), row_map),
                pl.BlockSpec((None, 1, 6 * d), lambda b, i: (mod_row(b), 0, 0)),
                pl.BlockSpec((None, tm, D_CONV), row_map),
                pl.BlockSpec((None, SUBLANES, D_CONV), lambda b, i: (b, jnp.maximum(i * halo - 1, 0), 0)),
                pl.BlockSpec((None, SUBLANES, D_CONV), lambda b, i: (b, jnp.minimum((i + 1) * halo, n_halo - 1), 0)),
                pl.BlockSpec((None, tm, D_CONV), row_map),
                u_spec, u_spec, u_spec,
                pl.BlockSpec((None, tm, D_ATTN), row_map),
                _const_spec((3, D_CONV)),
                _const_spec((1, D_SSM)),
                _const_spec((D_SSM, D_SSM)),
                _const_spec((1, D_SSM)),
                _const_spec((d, d)),
                _const_spec((1, d)),
                _const_spec((LANES, d)),
                _const_spec((N_EXPERTS, 1))]
    out_shape = (jax.ShapeDtypeStruct((bsz, t, d), F32),
                 jax.ShapeDtypeStruct((bsz, t, d), BF16),
                 jax.ShapeDtypeStruct((bsz, t, LANES), F32))
    out_specs = (pl.BlockSpec((None, tm, d), row_map),
                 pl.BlockSpec((None, tm, d), row_map),
                 pl.BlockSpec((None, tm, LANES), row_map))
    return pl.pallas_call(
        _mixout_kernel, out_shape=out_shape, grid=(bsz, t // tm),
        in_specs=in_specs, out_specs=out_specs,
        compiler_params=_cparams(2),
        name="mixer_output",
    )(h, mod_l, cv, cv, cv, cb, u, yf, yr, att,
      conv_w, ssm_d, w_glu, b_glu, w_out, norm_g, router_w, router_b)


EXPERTS_PER_CHUNK = 4


def _moe_kernel(t_ref, gates_ref, hmid_ref, mod_ref, wg_ref, wu_ref, wd_ref,
                wsg_ref, wsu_ref, wsd_ref, o_ref):
    t = t_ref[...]
    gates = gates_ref[...]
    hs = _silu(jnp.dot(t, wsg_ref[...], preferred_element_type=F32)) * jnp.dot(
        t, wsu_ref[...], preferred_element_type=F32)
    acc = jnp.dot(hs.astype(BF16), wsd_ref[...], preferred_element_type=F32)
    width = EXPERTS_PER_CHUNK * D_EXPERT
    n_chunks = N_EXPERTS // EXPERTS_PER_CHUNK

    def up(c):
        cols = slice(c * width, (c + 1) * width)
        return (jnp.dot(t, wg_ref[:, cols], preferred_element_type=F32),
                jnp.dot(t, wu_ref[:, cols], preferred_element_type=F32))

    nxt = up(0)
    for c in range(n_chunks):
        hg, hu = nxt
        if c + 1 < n_chunks:
            nxt = up(c + 1)
        hid = _silu(hg) * hu
        gated = []
        for e in range(EXPERTS_PER_CHUNK):
            k = c * EXPERTS_PER_CHUNK + e
            gated.append((hid[:, e * D_EXPERT:(e + 1) * D_EXPERT] * gates[:, k:k + 1]).astype(BF16))
        acc = acc + jnp.dot(jnp.concatenate(gated, axis=1), wd_ref[c * width:(c + 1) * width, :],
                            preferred_element_type=F32)
    mod = mod_ref[...]
    o_ref[...] = hmid_ref[...] + mod[:, 5 * D_MODEL:6 * D_MODEL] * acc


def _moe(t, gates, hmid, mod_l, mod_row, wg, wu, wd, wsg, wsu, wsd, tm):
    bsz, n, d = hmid.shape
    row_map = lambda b, i: (b, i, 0)
    return pl.pallas_call(
        _moe_kernel,
        out_shape=jax.ShapeDtypeStruct((bsz, n, d), F32),
        grid=(bsz, n // tm),
        in_specs=[pl.BlockSpec((None, tm, d), row_map),
                  pl.BlockSpec((None, tm, LANES), row_map),
                  pl.BlockSpec((None, tm, d), row_map),
                  pl.BlockSpec((None, 1, 6 * d), lambda b, i: (mod_row(b), 0, 0)),
                  _const_spec(wg.shape), _const_spec(wu.shape), _const_spec(wd.shape),
                  _const_spec(wsg.shape), _const_spec(wsu.shape), _const_spec(wsd.shape)],
        out_specs=pl.BlockSpec((None, tm, d), row_map),
        compiler_params=_cparams(2),
        name="moe",
    )(t, gates, hmid, mod_l, wg, wu, wd, wsg, wsu, wsd)


def _rope_tables(n_tokens):
    rows = n_tokens // GRID_W
    row = np.repeat(np.arange(rows, dtype=np.float64), GRID_W)
    col = np.tile(np.arange(GRID_W, dtype=np.float64), rows)
    inv_freq = ROPE_BASE ** (-np.arange(AXIS_PAIRS, dtype=np.float64) / AXIS_PAIRS)
    ar, ac = row[:, None] * inv_freq, col[:, None] * inv_freq
    cos = np.concatenate([np.cos(ar), np.cos(ar), np.cos(ac), np.cos(ac)], axis=1).astype(np.float32)
    sin = np.concatenate([-np.sin(ar), np.sin(ar), -np.sin(ac), np.sin(ac)], axis=1).astype(np.float32)
    reps = (1, HEAD_PAIR // HEAD_DIM)
    return jnp.tile(jnp.asarray(cos), reps), jnp.tile(jnp.asarray(sin), reps)


def _ssm_params(lam_re, lam_im, log_dt, b_re, b_im, c_re, c_im):
    lead = lam_re.shape[:2]
    lr, li = lam_re.astype(F32), lam_im.astype(F32)
    dt = jnp.exp(log_dt.astype(F32))[..., None]
    mag = jnp.exp(lr * dt)
    ar, ai = mag * jnp.cos(li * dt), mag * jnp.sin(li * dt)
    den = lr * lr + li * li
    qr = ((ar - 1.0) * lr + ai * li) / den
    qi = (ai * lr - (ar - 1.0) * li) / den
    br, bi = b_re.astype(F32), b_im.astype(F32)
    bbar_re = qr[..., None] * br - qi[..., None] * bi
    bbar_im = qr[..., None] * bi + qi[..., None] * br
    chan_group = jnp.arange(D_SSM) // SSM_GROUP
    state_group = jnp.arange(N_STATE) // SSM_STATE

    def drive(m):
        rows = jnp.swapaxes(m, -1, -2).reshape(lead + (D_SSM, SSM_STATE))
        return jnp.where(chan_group[:, None] == state_group[None, :], jnp.tile(rows, (1, 1, 1, SSM_GROUPS)), 0.0)

    def read(m):
        cols = jnp.moveaxis(m.astype(F32), -1, 2).reshape(lead + (SSM_STATE, D_SSM))
        return jnp.where(state_group[:, None] == chan_group[None, :], jnp.tile(cols, (1, 1, SSM_GROUPS, 1)), 0.0)

    bmat = jnp.concatenate([drive(bbar_re), drive(bbar_im)], axis=-1)
    cmat = jnp.concatenate([read(c_re), -read(c_im)], axis=-2)
    lamv = jnp.stack([ar, ai], axis=2).reshape(lead[0], 4, N_STATE)
    return bmat.astype(BF16), lamv, cmat.astype(BF16)


def _pair_heads(w, axis):
    shape = w.shape
    split = shape[:axis] + (N_KV_HEADS, N_Q_HEADS // N_KV_HEADS, HEAD_DIM) + shape[axis + 1:]
    return jnp.swapaxes(w.reshape(split), axis, axis + 1).reshape(shape)


def kernel(x, c, ctx, c_ctx, w_mod, b_mod, norm1_g, norm2_g, w_in, conv_w, ssm_lam_re, ssm_lam_im, ssm_log_dt, ssm_b_re, ssm_b_im, ssm_c_re, ssm_c_im, ssm_d, w_glu, b_glu, q_norm_g, k_norm_g, attn_sink, w_out, router_w, router_bias, w_exp_gate, w_exp_up, w_exp_down, w_sh_gate, w_sh_up, w_sh_down):
    bsz, n_lat, d = x.shape
    n_ctx = ctx.shape[1]
    depth = w_mod.shape[0]
    assert bsz == SUBLANES and d == D_MODEL

    mod_rows = 2 * SUBLANES
    cvec = jnp.zeros((mod_rows, d), F32).at[:bsz].set(c).at[bsz].set(c_ctx)
    mod = _modulation(cvec, w_mod, b_mod).reshape(depth, mod_rows, 1, 6 * d)
    lat_row = lambda b: b
    ctx_row = lambda b: bsz
    rope_tabs = _rope_tables(n_lat)

    tm_lat, tm_ctx, scan_steps = 512, 256, 128
    q0, kv0 = 3 * D_CONV + D_SSM, 3 * D_CONV + D_SSM + D_ATTN
    bmat, lamv, cmat = _ssm_params(ssm_lam_re, ssm_lam_im, ssm_log_dt, ssm_b_re, ssm_b_im, ssm_c_re, ssm_c_im)
    h, hc = x, ctx
    for l in range(depth):
        ctx_out = l < depth - 1
        mod_l = mod[l]
        w_in2 = jnp.concatenate([w_in[l][:, :q0], _pair_heads(w_in[l][:, q0:kv0], 1), w_in[l][:, kv0:]],
                                axis=1).astype(BF16)
        qg = jnp.tile(q_norm_g[l].astype(F32), HEAD_PAIR // HEAD_DIM)[None, :]
        kg = jnp.concatenate([jnp.tile(k_norm_g[l].astype(F32), N_KV_HEADS),
                              jnp.ones((N_KV_HEADS * HEAD_DIM,), F32)])[None, :]
        n1 = norm1_g[l][None, :]
        cv, cb, u_lat, q, kv = _in_projection(h, mod_l, lat_row, n1, w_in2, qg, kg, rope_tabs, tm_lat)
        cvc, cbc, u_ctx, qc, kvc = _in_projection(hc, mod_l, ctx_row, n1, w_in2, qg, kg, None, tm_ctx)
        zero_state = jnp.zeros((bsz, 2 * N_STATE), F32)
        yfc, yrc, hf, hr = _ssm_scan(u_ctx, zero_state, zero_state, bmat[l], lamv[l], cmat[l], scan_steps)
        yf, yr, _, _ = _ssm_scan(u_lat, hf, hr, bmat[l], lamv[l], cmat[l], scan_steps)

        sink = attn_sink[l].astype(F32)
        att = _attention(sink, q, kv, kvc, window=True)

        w_out_l = jnp.concatenate([w_out[l][:D_CONV + D_SSM], _pair_heads(w_out[l][D_CONV + D_SSM:], 0)], axis=0)
        rw = router_w[l].astype(F32)
        rw_hi = rw.astype(BF16)
        rw_lo = (rw - rw_hi.astype(F32)).astype(BF16)
        rw2 = jnp.concatenate([rw_hi.T, rw_lo.T, jnp.zeros((LANES - 2 * N_EXPERTS, d), BF16)], axis=0)
        rb = router_bias[l].astype(F32)[:, None]
        post = dict(conv_w=conv_w[l], ssm_d=ssm_d[l][None, :], w_glu=w_glu[l].astype(BF16),
                    b_glu=b_glu[l][None, :], w_out=w_out_l.astype(BF16), norm_g=norm2_g[l][None, :],
                    router_w=rw2, router_b=rb)
        experts = (w_exp_gate[l].astype(BF16).reshape(d, -1), w_exp_up[l].astype(BF16).reshape(d, -1),
                   w_exp_down[l].astype(BF16).reshape(-1, d), w_sh_gate[l].astype(BF16),
                   w_sh_up[l].astype(BF16), w_sh_down[l].astype(BF16))
        hmid, t, gates = _mixer_output(h, mod_l, lat_row, cv, cb, u_lat, yf, yr, att, tm=tm_lat, **post)
        h_new = _moe(t, gates, hmid, mod_l, lat_row, *experts, tm=tm_lat)
        if ctx_out:
            attc = _attention(sink, qc, None, kvc, window=False)
            hmid_c, tc, gates_c = _mixer_output(hc, mod_l, ctx_row, cvc, cbc, u_ctx, yfc, yrc, attc,
                                                tm=tm_ctx, **post)
            hc = _moe(tc, gates_c, hmid_c, mod_l, ctx_row, *experts, tm=tm_ctx)
        h = h_new
    return h
```

```python
import functools
import math

import jax
import jax.numpy as jnp
import numpy as np
from jax import lax
from jax.experimental import pallas as pl
from jax.experimental.pallas import tpu as pltpu

F32 = jnp.float32
BF16 = jnp.bfloat16

D_MODEL = 1024
D_CONV = 256
D_SSM = 256
SSM_GROUP = 16
SSM_GROUPS = 16
SSM_STATE = 64
N_STATE = SSM_GROUPS * SSM_STATE
HEAD_DIM = 64
D_ATTN = 512
N_Q_HEADS = 8
N_KV_HEADS = 2
AXIS_PAIRS = 16
ROPE_BASE = 10000.0
GRID_W = 64
BLOCK = 128
N_EXPERTS = 32
TOP_K = 4
D_EXPERT = 128
ROUTED_SCALE = 2.5
NORM_EPS = 1e-6
NEG_INF = -1e30
D_PROJ = 3 * D_CONV + D_SSM + D_ATTN + 2 * N_KV_HEADS * HEAD_DIM
LANES = 128
SUBLANES = 8
MXU_TILE = 256
SCAN_PIECES = 4
VMEM_LIMIT = 56 * 1024 * 1024


def _cparams(n_axes):
    return pltpu.CompilerParams(dimension_semantics=("arbitrary",) * n_axes,
                                vmem_limit_bytes=VMEM_LIMIT)


def _const_spec(shape):
    nd = len(shape)
    return pl.BlockSpec(shape, lambda *_: (0,) * nd, pipeline_mode=pl.Buffered(1))


def _silu(x):
    return x * jax.nn.sigmoid(x)


def _rms_modulate(x, gain, shift, scale):
    ms = jnp.mean(x * x, axis=-1, keepdims=True)
    return (x * lax.rsqrt(ms + NORM_EPS) * gain) * (1.0 + scale) + shift


def _mod_kernel(c_ref, w_ref, b_ref, o_ref):
    cv = c_ref[...]
    s = _silu(cv)
    s_hi = s.astype(BF16)
    s_lo = (s - s_hi.astype(F32)).astype(BF16)
    w = w_ref[...]
    w_hi = w.astype(BF16)
    w_lo = (w - w_hi.astype(F32)).astype(BF16)
    rows = cv.shape[0]
    a = jnp.dot(jnp.concatenate([s_hi, s_lo], axis=0), w_hi, preferred_element_type=F32)
    b = jnp.dot(s_hi, w_lo, preferred_element_type=F32)
    o_ref[...] = a[0:rows] + a[rows:2 * rows] + b + b_ref[...]


def _modulation(cvec, w_mod, b_mod):
    depth, d, n = w_mod.shape
    rows = cvec.shape[0]
    tn = 1536
    return pl.pallas_call(
        _mod_kernel,
        out_shape=jax.ShapeDtypeStruct((depth, rows, n), F32),
        grid=(depth, n // tn),
        in_specs=[pl.BlockSpec((rows, d), lambda l, j: (0, 0)),
                  pl.BlockSpec((None, d, tn), lambda l, j: (l, 0, j)),
                  pl.BlockSpec((None, 1, tn), lambda l, j: (l, 0, j))],
        out_specs=pl.BlockSpec((None, rows, tn), lambda l, j: (l, 0, j)),
        compiler_params=_cparams(2),
        name="modulation",
    )(cvec, w_mod, b_mod.reshape(depth, 1, n))


HEAD_PAIR = 4 * HEAD_DIM
LOG2E = math.log2(math.e)
Q_SCALE = HEAD_DIM ** -0.5 * LOG2E


def _inproj_kernel(*refs, rope):
    if rope:
        (x_ref, mod_ref, g_ref, w_ref, qg_ref, kg_ref, ones_ref, perm_ref, cos_ref, sin_ref,
         cv_ref, cb_ref, u_ref, q_ref, kv_ref) = refs
    else:
        (x_ref, mod_ref, g_ref, w_ref, qg_ref, kg_ref, ones_ref,
         cv_ref, cb_ref, u_ref, q_ref, kv_ref) = refs
    mod = mod_ref[...]
    mb = _rms_modulate(x_ref[...], g_ref[...], mod[:, 0:D_MODEL], mod[:, D_MODEL:2 * D_MODEL]).astype(BF16)

    def proj(c0, width):
        return jnp.dot(mb, w_ref[:, c0:c0 + width], preferred_element_type=F32)

    conv_v = proj(2 * D_CONV, D_CONV) * proj(0, D_CONV)
    conv_b = proj(D_CONV, D_CONV)
    u = proj(3 * D_CONV, D_SSM)
    q0 = 3 * D_CONV + D_SSM
    kv0 = q0 + D_ATTN
    blocks = [(q0 + jj * HEAD_PAIR, qg_ref, q_ref, jj * HEAD_PAIR, Q_SCALE, False)
              for jj in range(D_ATTN // HEAD_PAIR)] + [(kv0, kg_ref, kv_ref, 0, 1.0, True)]
    is_k = lax.broadcasted_iota(jnp.int32, (1, HEAD_PAIR), 1) < N_KV_HEADS * HEAD_DIM
    xs = [proj(blk[0], HEAD_PAIR) for blk in blocks]
    ssqs = [jnp.dot((x * x).astype(BF16), ones_ref[...], preferred_element_type=F32) for x in xs]
    norms = [lax.rsqrt(ssq * (1.0 / HEAD_DIM) + NORM_EPS) for ssq in ssqs]
    norms = [jnp.where(is_k, nm, 1.0) if blk[5] else nm for nm, blk in zip(norms, blocks)]
    xns = [x * nm * blk[1][...] for x, nm, blk in zip(xs, norms, blocks)]
    if rope:
        partners = [jnp.dot(xn.astype(BF16), perm_ref[...], preferred_element_type=F32) for xn in xns]
        cos, sin = cos_ref[...], sin_ref[...]
        xns = [xn * (jnp.where(is_k, cos, 1.0) if blk[5] else cos) + pt * (jnp.where(is_k, sin, 0.0) if blk[5] else sin)
               for xn, pt, blk in zip(xns, partners, blocks)]
    for xn, (_, _, out_ref, o0, scale, _) in zip(xns, blocks):
        if scale != 1.0:
            xn = xn * scale
        out_ref[:, o0:o0 + HEAD_PAIR] = xn.astype(BF16)
    cv_ref[...] = conv_v
    cb_ref[...] = conv_b
    u_ref[...] = u


def _in_projection(h, mod_l, mod_row, norm_g, w_in2, qg, kg, rope_tabs, tm):
    bsz, t, d = h.shape
    rope = rope_tabs is not None
    row_map = lambda b, i: (b, i, 0)
    in_specs = [pl.BlockSpec((None, tm, d), row_map),
                pl.BlockSpec((None, 1, 6 * d), lambda b, i: (mod_row(b), 0, 0)),
                _const_spec((1, d)),
                _const_spec((d, D_PROJ)),
                _const_spec((1, HEAD_PAIR)),
                _const_spec((1, HEAD_PAIR)),
                _const_spec((HEAD_PAIR, HEAD_PAIR))]
    lane = jnp.arange(HEAD_PAIR)
    head_ones = (lane[:, None] // HEAD_DIM == lane[None, :] // HEAD_DIM).astype(BF16)
    args = [h, mod_l, norm_g, w_in2, qg, kg, head_ones]
    if rope:
        swap = (lane[:, None] == (lane[None, :] ^ AXIS_PAIRS)).astype(BF16)
        in_specs += [_const_spec((HEAD_PAIR, HEAD_PAIR))]
        in_specs += [pl.BlockSpec((tm, HEAD_PAIR), lambda b, i: (i, 0))] * 2
        args += [swap] + list(rope_tabs)
    out_shape = (jax.ShapeDtypeStruct((bsz, t, D_CONV), F32),
                 jax.ShapeDtypeStruct((bsz, t, D_CONV), F32),
                 jax.ShapeDtypeStruct((bsz, t, D_SSM), F32),
                 jax.ShapeDtypeStruct((bsz, t, D_ATTN), BF16),
                 jax.ShapeDtypeStruct((bsz, t, HEAD_PAIR), BF16))
    out_specs = (pl.BlockSpec((None, tm, D_CONV), row_map),
                 pl.BlockSpec((None, tm, D_CONV), row_map),
                 pl.BlockSpec((None, tm, D_SSM), row_map),
                 pl.BlockSpec((None, tm, D_ATTN), row_map),
                 pl.BlockSpec((None, tm, HEAD_PAIR), row_map))
    return pl.pallas_call(
        functools.partial(_inproj_kernel, rope=rope),
        out_shape=out_shape, grid=(bsz, t // tm), in_specs=in_specs, out_specs=out_specs,
        compiler_params=_cparams(2),
        name="in_projection_lat" if rope else "in_projection_ctx",
    )(*args)


def _scan_kernel(uf_ref, ur_ref, h0f_ref, h0r_ref, bmat_ref, lam_ref, cmat_ref,
                 yf_ref, yr_ref, hf_ref, hr_ref, *, steps):
    @pl.when(pl.program_id(0) == 0)
    def _():
        hf_ref[...] = h0f_ref[...]
        hr_ref[...] = h0r_ref[...]

    piece = steps // SCAN_PIECES

    def drive(u_ref, d, p):
        u = jnp.concatenate([u_ref[:, t, :] for t in range(p * piece, (p + 1) * piece)], axis=0)
        return jnp.dot(u.astype(BF16), bmat_ref[d], preferred_element_type=F32)

    def recur(bu, d, h, descending):
        l_re = jnp.broadcast_to(lam_ref[2 * d:2 * d + 1, :], (SUBLANES, N_STATE))
        l_im = jnp.broadcast_to(lam_ref[2 * d + 1:2 * d + 2, :], (SUBLANES, N_STATE))
        h_re, h_im = h
        states = [None] * piece
        for t in (range(piece - 1, -1, -1) if descending else range(piece)):
            rows = slice(t * SUBLANES, (t + 1) * SUBLANES)
            h_re, h_im = (l_re * h_re - l_im * h_im + bu[rows, 0:N_STATE],
                          l_re * h_im + l_im * h_re + bu[rows, N_STATE:2 * N_STATE])
            states[t] = (h_re, h_im)
        return (h_re, h_im), states

    def readout(states, d, y_ref, p):
        s_re = jnp.concatenate([s[0] for s in states], axis=0).astype(BF16)
        s_im = jnp.concatenate([s[1] for s in states], axis=0).astype(BF16)
        y = jnp.dot(jnp.concatenate([s_re, s_im], axis=1), cmat_ref[d], preferred_element_type=F32)
        for t in range(piece):
            y_ref[:, p * piece + t, :] = y[t * SUBLANES:(t + 1) * SUBLANES, :]

    hf = (hf_ref[:, 0:N_STATE], hf_ref[:, N_STATE:2 * N_STATE])
    hr = (hr_ref[:, 0:N_STATE], hr_ref[:, N_STATE:2 * N_STATE])
    for p in range(SCAN_PIECES):
        q = SCAN_PIECES - 1 - p
        hf, sf = recur(drive(uf_ref, 0, p), 0, hf, False)
        hr, sr = recur(drive(ur_ref, 1, q), 1, hr, True)
        readout(sf, 0, yf_ref, p)
        readout(sr, 1, yr_ref, q)
    hf_ref[:, 0:N_STATE], hf_ref[:, N_STATE:2 * N_STATE] = hf
    hr_ref[:, 0:N_STATE], hr_ref[:, N_STATE:2 * N_STATE] = hr


def _ssm_scan(u, h0f, h0r, bmat, lamv, cmat, steps):
    bsz, t, _ = u.shape
    n = t // steps
    blk = lambda f: pl.BlockSpec((bsz, steps, D_SSM), lambda j: (0, f(j), 0))
    fwd, rev = blk(lambda j: j), blk(lambda j: n - 1 - j)
    carry = jax.ShapeDtypeStruct((SUBLANES, 2 * N_STATE), F32)
    return pl.pallas_call(
        functools.partial(_scan_kernel, steps=steps),
        out_shape=(jax.ShapeDtypeStruct(u.shape, F32),) * 2 + (carry, carry),
        grid=(n,),
        in_specs=[fwd, rev,
                  _const_spec((SUBLANES, 2 * N_STATE)),
                  _const_spec((SUBLANES, 2 * N_STATE)),
                  _const_spec((2, D_SSM, 2 * N_STATE)),
                  _const_spec((4, N_STATE)),
                  _const_spec((2, 2 * N_STATE, D_SSM))],
        out_specs=(fwd, rev,
                   pl.BlockSpec((SUBLANES, 2 * N_STATE), lambda j: (0, 0)),
                   pl.BlockSpec((SUBLANES, 2 * N_STATE), lambda j: (0, 0))),
        compiler_params=_cparams(1),
        name="s5_scan",
    )(u, u, h0f, h0r, bmat, lamv, cmat)


MAX_Q_SUB = 8


def _attn_kernel(sink_ref, q_ref, *refs, window):
    q_sub = q_ref.shape[0] // BLOCK
    if window:
        kv_refs, kvc_ref, o_ref = refs[:q_sub + 2], refs[q_sub + 2], refs[q_sub + 3]
    else:
        kvc_ref, o_ref = refs
    n = pl.program_id(1)
    last_block = pl.num_programs(1) * q_sub - 1
    rows = 2 * BLOCK
    lane = lax.broadcasted_iota(jnp.int32, (1, LANES), 1)
    lo = lane < HEAD_DIM
    qi = lax.broadcasted_iota(jnp.int32, (rows, BLOCK), 0) & (BLOCK - 1)
    kj = lax.broadcasted_iota(jnp.int32, (rows, BLOCK), 1)
    upper_head = lax.broadcasted_iota(jnp.int32, (rows, 1), 0) >= BLOCK
    zero = jnp.zeros((), BF16)
    nt = (((1,), (1,)), ((), ()))
    k_cols, v_cols = slice(0, LANES), slice(LANES, 2 * LANES)
    units = [(a, j) for a in range(q_sub) for j in range(D_ATTN // LANES)]
    qs = []
    for a, j in units:
        qp = q_ref[a * BLOCK:(a + 1) * BLOCK, j * LANES:(j + 1) * LANES]
        qs.append(jnp.concatenate([jnp.where(lo, qp, zero), jnp.where(lo, zero, qp)], axis=0))
    s_all = []
    for (a, j), q in zip(units, qs):
        scores = []
        if window:
            block = n * q_sub + a
            s = lax.dot_general(q, kv_refs[a][:, k_cols], nt, preferred_element_type=F32)
            scores.append(jnp.where((kj >= qi) & (block > 0), s, NEG_INF))
            scores.append(lax.dot_general(q, kv_refs[a + 1][:, k_cols], nt, preferred_element_type=F32))
            s = lax.dot_general(q, kv_refs[a + 2][:, k_cols], nt, preferred_element_type=F32)
            scores.append(jnp.where((kj <= qi) & (block < last_block), s, NEG_INF))
        scores.append(lax.dot_general(q, kvc_ref[:, k_cols], nt, preferred_element_type=F32))
        s_all.append(jnp.concatenate(scores, axis=1))
    sinks = [jnp.where(upper_head, sink_ref[N_Q_HEADS // 2 + j], sink_ref[j]) * LOG2E for _, j in units]
    ms = [jnp.maximum(sk, jnp.max(s, axis=-1, keepdims=True)) for sk, s in zip(sinks, s_all)]
    es = [jnp.exp2(s - m) for s, m in zip(s_all, ms)]
    denoms = [jnp.exp2(sk - m) + jnp.sum(e, axis=-1, keepdims=True) for sk, m, e in zip(sinks, ms, es)]
    accs = []
    for (a, j), e in zip(units, es):
        values = [kv_refs[a + w][:, v_cols] for w in range(3)] if window else []
        values = jnp.concatenate(values + [kvc_ref[:, v_cols]], axis=0)
        accs.append(jnp.dot(e.astype(BF16), values, preferred_element_type=F32))
    for (a, j), acc, denom in zip(units, accs, denoms):
        out = acc / denom
        o_ref[a * BLOCK:(a + 1) * BLOCK, j * LANES:(j + 1) * LANES] = jnp.where(
            lo, out[0:BLOCK], out[BLOCK:2 * BLOCK]).astype(BF16)


def _attention(sink, q, kv, kv_ctx, window):
    bsz, t, _ = q.shape
    n_ctx = kv_ctx.shape[1]
    nb = t // BLOCK
    q_sub = min(MAX_Q_SUB, nb)
    tq = q_sub * BLOCK
    in_specs = [pl.BlockSpec(memory_space=pltpu.SMEM),
                pl.BlockSpec((None, tq, D_ATTN), lambda b, n: (b, n, 0))]
    args = [sink, q]
    if window:
        for w in range(q_sub + 2):
            in_specs.append(pl.BlockSpec(
                (None, BLOCK, HEAD_PAIR),
                lambda b, n, w=w: (b, jnp.clip(n * q_sub + w - 1, 0, nb - 1), 0)))
            args.append(kv)
    in_specs += [pl.BlockSpec((None, n_ctx, HEAD_PAIR), lambda b, n: (b, 0, 0))]
    args += [kv_ctx]
    return pl.pallas_call(
        functools.partial(_attn_kernel, window=window),
        out_shape=jax.ShapeDtypeStruct((bsz, t, D_ATTN), BF16),
        grid=(bsz, t // tq), in_specs=in_specs,
        out_specs=pl.BlockSpec((None, tq, D_ATTN), lambda b, n: (b, n, 0)),
        compiler_params=_cparams(2),
        name="window_attention" if window else "context_attention",
    )(*args)


MIX_PARTS = 1


def _mixout_kernel(h_ref, mod_ref, cv_ref, cvp_ref, cvn_ref, cb_ref, u_ref, yf_ref, yr_ref, att_ref,
                   convw_ref, d_ref, wglu_ref, bglu_ref, wout_ref, g2_ref, rw_ref, rb_ref,
                   hmid_ref, t_ref, gates_ref):
    i = pl.program_id(1)
    last = pl.num_programs(1) - 1
    tm = cv_ref.shape[0]
    cv = cv_ref[...]
    row = lax.broadcasted_iota(jnp.int32, (tm, 1), 0)
    before = jnp.where(i > 0, cvp_ref[SUBLANES - 1:SUBLANES, :], 0.0)
    after = jnp.where(i < last, cvn_ref[0:1, :], 0.0)
    down = jnp.where(row == 0, before, pltpu.roll(cv, 1, 0))
    up = jnp.where(row == tm - 1, after, pltpu.roll(cv, tm - 1, 0))
    w = convw_ref[...]
    y_conv = cb_ref[...] * (w[0:1, :] * down + w[1:2, :] * cv + w[2:3, :] * up)
    mod = mod_ref[...]
    nt = (((1,), (1,)), ((), ()))
    eidx = lax.broadcasted_iota(jnp.int32, (N_EXPERTS, 1), 0).astype(F32)

    def part(rows):
        y = d_ref[...] * u_ref[rows, :] + yf_ref[rows, :] + yr_ref[rows, :]
        gl = jax.nn.gelu(y)
        z = jnp.dot(gl.astype(BF16), wglu_ref[...], preferred_element_type=F32) + bglu_ref[...]
        y_ssm = gl * jax.nn.sigmoid(z)
        mix = (jnp.dot(y_conv[rows, :].astype(BF16), wout_ref[0:D_CONV, :], preferred_element_type=F32)
               + jnp.dot(y_ssm.astype(BF16), wout_ref[D_CONV:D_CONV + D_SSM, :], preferred_element_type=F32)
               + jnp.dot(att_ref[rows, :], wout_ref[D_CONV + D_SSM:, :], preferred_element_type=F32))
        h_mid = h_ref[rows, :] + mod[:, 2 * D_MODEL:3 * D_MODEL] * mix
        t = _rms_modulate(h_mid, g2_ref[...], mod[:, 3 * D_MODEL:4 * D_MODEL], mod[:, 4 * D_MODEL:5 * D_MODEL])
        t_hi = t.astype(BF16)
        t_lo = (t - t_hi.astype(F32)).astype(BF16)
        a = lax.dot_general(rw_ref[...], t_hi, nt, preferred_element_type=F32)
        b = lax.dot_general(rw_ref[...], t_lo, nt, preferred_element_type=F32)
        logits = a[0:N_EXPERTS] + a[N_EXPERTS:2 * N_EXPERTS] + b[0:N_EXPERTS]
        scores = jax.nn.sigmoid(logits)
        biased = scores + rb_ref[...]
        sel = jnp.zeros(scores.shape, F32)
        for _ in range(TOP_K):
            best = jnp.max(biased, axis=0, keepdims=True)
            first = jnp.min(jnp.where(biased == best, eidx, float(N_EXPERTS)), axis=0, keepdims=True)
            pick = eidx == first
            sel = jnp.where(pick, scores, sel)
            biased = jnp.where(pick, -jnp.inf, biased)
        gates = sel / jnp.sum(sel, axis=0, keepdims=True) * ROUTED_SCALE
        padded = jnp.concatenate([gates, jnp.zeros((LANES - N_EXPERTS, gates.shape[1]), F32)], axis=0)
        return h_mid, t_hi, padded.T

    parts = [slice(p * tm // MIX_PARTS, (p + 1) * tm // MIX_PARTS) for p in range(MIX_PARTS)]
    results = [part(rows) for rows in parts]
    for rows, (h_mid, t_hi, gates) in zip(parts, results):
        hmid_ref[rows, :] = h_mid
        t_ref[rows, :] = t_hi
        gates_ref[rows, :] = gates


def _mixer_output(h, mod_l, mod_row, cv, cb, u, yf, yr, att,
                  conv_w, ssm_d, w_glu, b_glu, w_out, norm_g, router_w, router_b, tm):
    bsz, t, d = h.shape
    halo = tm // SUBLANES
    n_halo = t // SUBLANES
    row_map = lambda b, i: (b, i, 0)
    u_spec = pl.BlockSpec((None, tm, D_SSM), row_map)
    in_specs = [pl.BlockSpec((None, tm, d), row_map),
                pl.BlockSpec((None, 1, 6 * d), lambda b, i: (mod_row(b), 0, 0)),
                pl.BlockSpec((None, tm, D_CONV), row_map),
                pl.BlockSpec((None, SUBLANES, D_CONV), lambda b, i: (b, jnp.maximum(i * halo - 1, 0), 0)),
                pl.BlockSpec((None, SUBLANES, D_CONV), lambda b, i: (b, jnp.minimum((i + 1) * halo, n_halo - 1), 0)),
                pl.BlockSpec((None, tm, D_CONV), row_map),
                u_spec, u_spec, u_spec,
                pl.BlockSpec((None, tm, D_ATTN), row_map),
                _const_spec((3, D_CONV)),
                _const_spec((1, D_SSM)),
                _const_spec((D_SSM, D_SSM)),
                _const_spec((1, D_SSM)),
                _const_spec((d, d)),
                _const_spec((1, d)),
                _const_spec((LANES, d)),
                _const_spec((N_EXPERTS, 1))]
    out_shape = (jax.ShapeDtypeStruct((bsz, t, d), F32),
                 jax.ShapeDtypeStruct((bsz, t, d), BF16),
                 jax.ShapeDtypeStruct((bsz, t, LANES), F32))
    out_specs = (pl.BlockSpec((None, tm, d), row_map),
                 pl.BlockSpec((None, tm, d), row_map),
                 pl.BlockSpec((None, tm, LANES), row_map))
    return pl.pallas_call(
        _mixout_kernel, out_shape=out_shape, grid=(bsz, t // tm),
        in_specs=in_specs, out_specs=out_specs,
        compiler_params=_cparams(2),
        name="mixer_output",
    )(h, mod_l, cv, cv, cv, cb, u, yf, yr, att,
      conv_w, ssm_d, w_glu, b_glu, w_out, norm_g, router_w, router_b)


EXPERTS_PER_CHUNK = 4


def _moe_kernel(t_ref, gates_ref, hmid_ref, mod_ref, wg_ref, wu_ref, wd_ref,
                wsg_ref, wsu_ref, wsd_ref, o_ref):
    t = t_ref[...]
    gates = gates_ref[...]
    hs = _silu(jnp.dot(t, wsg_ref[...], preferred_element_type=F32)) * jnp.dot(
        t, wsu_ref[...], preferred_element_type=F32)
    acc = jnp.dot(hs.astype(BF16), wsd_ref[...], preferred_element_type=F32)
    width = EXPERTS_PER_CHUNK * D_EXPERT
    n_chunks = N_EXPERTS // EXPERTS_PER_CHUNK

    def up(c):
        cols = slice(c * width, (c + 1) * width)
        return (jnp.dot(t, wg_ref[:, cols], preferred_element_type=F32),
                jnp.dot(t, wu_ref[:, cols], preferred_element_type=F32))

    nxt = up(0)
    for c in range(n_chunks):
        hg, hu = nxt
        if c + 1 < n_chunks:
            nxt = up(c + 1)
        hid = _silu(hg) * hu
        gated = []
        for e in range(EXPERTS_PER_CHUNK):
            k = c * EXPERTS_PER_CHUNK + e
            gated.append((hid[:, e * D_EXPERT:(e + 1) * D_EXPERT] * gates[:, k:k + 1]).astype(BF16))
        acc = acc + jnp.dot(jnp.concatenate(gated, axis=1), wd_ref[c * width:(c + 1) * width, :],
                            preferred_element_type=F32)
    mod = mod_ref[...]
    o_ref[...] = hmid_ref[...] + mod[:, 5 * D_MODEL:6 * D_MODEL] * acc


def _moe(t, gates, hmid, mod_l, mod_row, wg, wu, wd, wsg, wsu, wsd, tm):
    bsz, n, d = hmid.shape
    row_map = lambda b, i: (b, i, 0)
    return pl.pallas_call(
        _moe_kernel,
        out_shape=jax.ShapeDtypeStruct((bsz, n, d), F32),
        grid=(bsz, n // tm),
        in_specs=[pl.BlockSpec((None, tm, d), row_map),
                  pl.BlockSpec((None, tm, LANES), row_map),
                  pl.BlockSpec((None, tm, d), row_map),
                  pl.BlockSpec((None, 1, 6 * d), lambda b, i: (mod_row(b), 0, 0)),
                  _const_spec(wg.shape), _const_spec(wu.shape), _const_spec(wd.shape),
                  _const_spec(wsg.shape), _const_spec(wsu.shape), _const_spec(wsd.shape)],
        out_specs=pl.BlockSpec((None, tm, d), row_map),
        compiler_params=_cparams(2),
        name="moe",
    )(t, gates, hmid, mod_l, wg, wu, wd, wsg, wsu, wsd)


def _rope_tables(n_tokens):
    rows = n_tokens // GRID_W
    row = np.repeat(np.arange(rows, dtype=np.float64), GRID_W)
    col = np.tile(np.arange(GRID_W, dtype=np.float64), rows)
    inv_freq = ROPE_BASE ** (-np.arange(AXIS_PAIRS, dtype=np.float64) / AXIS_PAIRS)
    ar, ac = row[:, None] * inv_freq, col[:, None] * inv_freq
    cos = np.concatenate([np.cos(ar), np.cos(ar), np.cos(ac), np.cos(ac)], axis=1).astype(np.float32)
    sin = np.concatenate([-np.sin(ar), np.sin(ar), -np.sin(ac), np.sin(ac)], axis=1).astype(np.float32)
    reps = (1, HEAD_PAIR // HEAD_DIM)
    return jnp.tile(jnp.asarray(cos), reps), jnp.tile(jnp.asarray(sin), reps)


def _ssm_params(lam_re, lam_im, log_dt, b_re, b_im, c_re, c_im):
    lead = lam_re.shape[:2]
    lr, li = lam_re.astype(F32), lam_im.astype(F32)
    dt = jnp.exp(log_dt.astype(F32))[..., None]
    mag = jnp.exp(lr * dt)
    ar, ai = mag * jnp.cos(li * dt), mag * jnp.sin(li * dt)
    den = lr * lr + li * li
    qr = ((ar - 1.0) * lr + ai * li) / den
    qi = (ai * lr - (ar - 1.0) * li) / den
    br, bi = b_re.astype(F32), b_im.astype(F32)
    bbar_re = qr[..., None] * br - qi[..., None] * bi
    bbar_im = qr[..., None] * bi + qi[..., None] * br
    chan_group = jnp.arange(D_SSM) // SSM_GROUP
    state_group = jnp.arange(N_STATE) // SSM_STATE

    def drive(m):
        rows = jnp.swapaxes(m, -1, -2).reshape(lead + (D_SSM, SSM_STATE))
        return jnp.where(chan_group[:, None] == state_group[None, :], jnp.tile(rows, (1, 1, 1, SSM_GROUPS)), 0.0)

    def read(m):
        cols = jnp.moveaxis(m.astype(F32), -1, 2).reshape(lead + (SSM_STATE, D_SSM))
        return jnp.where(state_group[:, None] == chan_group[None, :], jnp.tile(cols, (1, 1, SSM_GROUPS, 1)), 0.0)

    bmat = jnp.concatenate([drive(bbar_re), drive(bbar_im)], axis=-1)
    cmat = jnp.concatenate([read(c_re), -read(c_im)], axis=-2)
    lamv = jnp.stack([ar, ai], axis=2).reshape(lead[0], 4, N_STATE)
    return bmat.astype(BF16), lamv, cmat.astype(BF16)


def _pair_heads(w, axis):
    shape = w.shape
    split = shape[:axis] + (N_KV_HEADS, N_Q_HEADS // N_KV_HEADS, HEAD_DIM) + shape[axis + 1:]
    return jnp.swapaxes(w.reshape(split), axis, axis + 1).reshape(shape)


def kernel(x, c, ctx, c_ctx, w_mod, b_mod, norm1_g, norm2_g, w_in, conv_w, ssm_lam_re, ssm_lam_im, ssm_log_dt, ssm_b_re, ssm_b_im, ssm_c_re, ssm_c_im, ssm_d, w_glu, b_glu, q_norm_g, k_norm_g, attn_sink, w_out, router_w, router_bias, w_exp_gate, w_exp_up, w_exp_down, w_sh_gate, w_sh_up, w_sh_down):
    bsz, n_lat, d = x.shape
    n_ctx = ctx.shape[1]
    depth = w_mod.shape[0]
    assert bsz == SUBLANES and d == D_MODEL

    mod_rows = 2 * SUBLANES
    cvec = jnp.zeros((mod_rows, d), F32).at[:bsz].set(c).at[bsz].set(c_ctx)
    mod = _modulation(cvec, w_mod, b_mod).reshape(depth, mod_rows, 1, 6 * d)
    lat_row = lambda b: b
    ctx_row = lambda b: bsz
    rope_tabs = _rope_tables(n_lat)

    tm_lat, tm_moe, tm_ctx, scan_steps = 1024, 512, 256, 128
    q0, kv0 = 3 * D_CONV + D_SSM, 3 * D_CONV + D_SSM + D_ATTN
    bmat, lamv, cmat = _ssm_params(ssm_lam_re, ssm_lam_im, ssm_log_dt, ssm_b_re, ssm_b_im, ssm_c_re, ssm_c_im)
    h, hc = x, ctx
    for l in range(depth):
        ctx_out = l < depth - 1
        mod_l = mod[l]
        w_in2 = jnp.concatenate([w_in[l][:, :q0], _pair_heads(w_in[l][:, q0:kv0], 1), w_in[l][:, kv0:]],
                                axis=1).astype(BF16)
        qg = jnp.tile(q_norm_g[l].astype(F32), HEAD_PAIR // HEAD_DIM)[None, :]
        kg = jnp.concatenate([jnp.tile(k_norm_g[l].astype(F32), N_KV_HEADS),
                              jnp.ones((N_KV_HEADS * HEAD_DIM,), F32)])[None, :]
        n1 = norm1_g[l][None, :]
        cv, cb, u_lat, q, kv = _in_projection(h, mod_l, lat_row, n1, w_in2, qg, kg, rope_tabs, tm_lat)
        cvc, cbc, u_ctx, qc, kvc = _in_projection(hc, mod_l, ctx_row, n1, w_in2, qg, kg, None, tm_ctx)
        zero_state = jnp.zeros((bsz, 2 * N_STATE), F32)
        yfc, yrc, hf, hr = _ssm_scan(u_ctx, zero_state, zero_state, bmat[l], lamv[l], cmat[l], scan_steps)
        yf, yr, _, _ = _ssm_scan(u_lat, hf, hr, bmat[l], lamv[l], cmat[l], scan_steps)

        sink = attn_sink[l].astype(F32)
        att = _attention(sink, q, kv, kvc, window=True)

        w_out_l = jnp.concatenate([w_out[l][:D_CONV + D_SSM], _pair_heads(w_out[l][D_CONV + D_SSM:], 0)], axis=0)
        rw = router_w[l].astype(F32)
        rw_hi = rw.astype(BF16)
        rw_lo = (rw - rw_hi.astype(F32)).astype(BF16)
        rw2 = jnp.concatenate([rw_hi.T, rw_lo.T, jnp.zeros((LANES - 2 * N_EXPERTS, d), BF16)], axis=0)
        rb = router_bias[l].astype(F32)[:, None]
        post = dict(conv_w=conv_w[l], ssm_d=ssm_d[l][None, :], w_glu=w_glu[l].astype(BF16),
                    b_glu=b_glu[l][None, :], w_out=w_out_l.astype(BF16), norm_g=norm2_g[l][None, :],
                    router_w=rw2, router_b=rb)
        experts = (w_exp_gate[l].astype(BF16).reshape(d, -1), w_exp_up[l].astype(BF16).reshape(d, -1),
                   w_exp_down[l].astype(BF16).reshape(-1, d), w_sh_gate[l].astype(BF16),
                   w_sh_up[l].astype(BF16), w_sh_down[l].astype(BF16))
        hmid, t, gates = _mixer_output(h, mod_l, lat_row, cv, cb, u_lat, yf, yr, att, tm=tm_lat, **post)
        h_new = _moe(t, gates, hmid, mod_l, lat_row, *experts, tm=tm_moe)
        if ctx_out:
            attc = _attention(sink, qc, None, kvc, window=False)
            hmid_c, tc, gates_c = _mixer_output(hc, mod_l, ctx_row, cvc, cbc, u_ctx, yfc, yrc, attc,
                                                tm=tm_ctx, **post)
            hc = _moe(tc, gates_c, hmid_c, mod_l, ctx_row, *experts, tm=tm_ctx)
        h = h_new
    return h
```

```python
import functools
import math

import jax
import jax.numpy as jnp
import numpy as np
from jax import lax
from jax.experimental import pallas as pl
from jax.experimental.pallas import tpu as pltpu

F32 = jnp.float32
BF16 = jnp.bfloat16

D_MODEL = 1024
D_CONV = 256
D_SSM = 256
SSM_GROUP = 16
SSM_GROUPS = 16
SSM_STATE = 64
N_STATE = SSM_GROUPS * SSM_STATE
HEAD_DIM = 64
D_ATTN = 512
N_Q_HEADS = 8
N_KV_HEADS = 2
AXIS_PAIRS = 16
ROPE_BASE = 10000.0
GRID_W = 64
BLOCK = 128
N_EXPERTS = 32
TOP_K = 4
D_EXPERT = 128
ROUTED_SCALE = 2.5
NORM_EPS = 1e-6
NEG_INF = -1e30
D_PROJ = 3 * D_CONV + D_SSM + D_ATTN + 2 * N_KV_HEADS * HEAD_DIM
LANES = 128
SUBLANES = 8
MXU_TILE = 256
SCAN_PIECE = 32
VMEM_LIMIT = 56 * 1024 * 1024


def _cparams(n_axes):
    return pltpu.CompilerParams(dimension_semantics=("arbitrary",) * n_axes,
                                vmem_limit_bytes=VMEM_LIMIT)


def _const_spec(shape):
    nd = len(shape)
    return pl.BlockSpec(shape, lambda *_: (0,) * nd, pipeline_mode=pl.Buffered(1))


def _silu(x):
    return x * jax.nn.sigmoid(x)


def _rms_modulate(x, gain, shift, scale):
    ms = jnp.mean(x * x, axis=-1, keepdims=True)
    return (x * lax.rsqrt(ms + NORM_EPS) * gain) * (1.0 + scale) + shift


def _mod_kernel(c_ref, w_ref, b_ref, o_ref):
    cv = c_ref[...]
    s = _silu(cv)
    s_hi = s.astype(BF16)
    s_lo = (s - s_hi.astype(F32)).astype(BF16)
    w = w_ref[...]
    w_hi = w.astype(BF16)
    w_lo = (w - w_hi.astype(F32)).astype(BF16)
    rows = cv.shape[0]
    a = jnp.dot(jnp.concatenate([s_hi, s_lo], axis=0), w_hi, preferred_element_type=F32)
    b = jnp.dot(s_hi, w_lo, preferred_element_type=F32)
    o_ref[...] = a[0:rows] + a[rows:2 * rows] + b + b_ref[...]


def _modulation(cvec, w_mod, b_mod):
    depth, d, n = w_mod.shape
    rows = cvec.shape[0]
    tn = 1536
    return pl.pallas_call(
        _mod_kernel,
        out_shape=jax.ShapeDtypeStruct((depth, rows, n), F32),
        grid=(depth, n // tn),
        in_specs=[pl.BlockSpec((rows, d), lambda l, j: (0, 0)),
                  pl.BlockSpec((None, d, tn), lambda l, j: (l, 0, j)),
                  pl.BlockSpec((None, 1, tn), lambda l, j: (l, 0, j))],
        out_specs=pl.BlockSpec((None, rows, tn), lambda l, j: (l, 0, j)),
        compiler_params=_cparams(2),
        name="modulation",
    )(cvec, w_mod, b_mod.reshape(depth, 1, n))


HEAD_PAIR = 4 * HEAD_DIM
LOG2E = math.log2(math.e)
Q_SCALE = HEAD_DIM ** -0.5 * LOG2E


def _inproj_kernel(*refs, rope):
    if rope:
        (x_ref, mod_ref, g_ref, w_ref, qg_ref, kg_ref, ones_ref, perm_ref, cos_ref, sin_ref,
         cv_ref, cb_ref, u_ref, q_ref, kv_ref) = refs
    else:
        (x_ref, mod_ref, g_ref, w_ref, qg_ref, kg_ref, ones_ref,
         cv_ref, cb_ref, u_ref, q_ref, kv_ref) = refs
    mod = mod_ref[...]
    mb = _rms_modulate(x_ref[...], g_ref[...], mod[:, 0:D_MODEL], mod[:, D_MODEL:2 * D_MODEL]).astype(BF16)

    def proj(c0, width):
        return jnp.dot(mb, w_ref[:, c0:c0 + width], preferred_element_type=F32)

    conv_v = proj(2 * D_CONV, D_CONV) * proj(0, D_CONV)
    conv_b = proj(D_CONV, D_CONV)
    u = proj(3 * D_CONV, D_SSM)
    q0 = 3 * D_CONV + D_SSM
    kv0 = q0 + D_ATTN
    blocks = [(q0 + jj * HEAD_PAIR, qg_ref, q_ref, jj * HEAD_PAIR, Q_SCALE, False)
              for jj in range(D_ATTN // HEAD_PAIR)] + [(kv0, kg_ref, kv_ref, 0, 1.0, True)]
    is_k = lax.broadcasted_iota(jnp.int32, (1, HEAD_PAIR), 1) < N_KV_HEADS * HEAD_DIM
    xs = [proj(blk[0], HEAD_PAIR) for blk in blocks]
    ssqs = [jnp.dot((x * x).astype(BF16), ones_ref[...], preferred_element_type=F32) for x in xs]
    norms = [lax.rsqrt(ssq * (1.0 / HEAD_DIM) + NORM_EPS) for ssq in ssqs]
    norms = [jnp.where(is_k, nm, 1.0) if blk[5] else nm for nm, blk in zip(norms, blocks)]
    xns = [x * nm * blk[1][...] for x, nm, blk in zip(xs, norms, blocks)]
    if rope:
        partners = [jnp.dot(xn.astype(BF16), perm_ref[...], preferred_element_type=F32) for xn in xns]
        cos, sin = cos_ref[...], sin_ref[...]
        xns = [xn * (jnp.where(is_k, cos, 1.0) if blk[5] else cos) + pt * (jnp.where(is_k, sin, 0.0) if blk[5] else sin)
               for xn, pt, blk in zip(xns, partners, blocks)]
    for xn, (_, _, out_ref, o0, scale, _) in zip(xns, blocks):
        if scale != 1.0:
            xn = xn * scale
        out_ref[:, o0:o0 + HEAD_PAIR] = xn.astype(BF16)
    cv_ref[...] = conv_v
    cb_ref[...] = conv_b
    u_ref[...] = u


def _in_projection(h, mod_l, mod_row, norm_g, w_in2, qg, kg, rope_tabs, tm):
    bsz, t, d = h.shape
    rope = rope_tabs is not None
    row_map = lambda b, i: (b, i, 0)
    in_specs = [pl.BlockSpec((None, tm, d), row_map),
                pl.BlockSpec((None, 1, 6 * d), lambda b, i: (mod_row(b), 0, 0)),
                _const_spec((1, d)),
                _const_spec((d, D_PROJ)),
                _const_spec((1, HEAD_PAIR)),
                _const_spec((1, HEAD_PAIR)),
                _const_spec((HEAD_PAIR, HEAD_PAIR))]
    lane = jnp.arange(HEAD_PAIR)
    head_ones = (lane[:, None] // HEAD_DIM == lane[None, :] // HEAD_DIM).astype(BF16)
    args = [h, mod_l, norm_g, w_in2, qg, kg, head_ones]
    if rope:
        swap = (lane[:, None] == (lane[None, :] ^ AXIS_PAIRS)).astype(BF16)
        in_specs += [_const_spec((HEAD_PAIR, HEAD_PAIR))]
        in_specs += [pl.BlockSpec((tm, HEAD_PAIR), lambda b, i: (i, 0))] * 2
        args += [swap] + list(rope_tabs)
    out_shape = (jax.ShapeDtypeStruct((bsz, t, D_CONV), F32),
                 jax.ShapeDtypeStruct((bsz, t, D_CONV), F32),
                 jax.ShapeDtypeStruct((bsz, t, D_SSM), F32),
                 jax.ShapeDtypeStruct((bsz, t, D_ATTN), BF16),
                 jax.ShapeDtypeStruct((bsz, t, HEAD_PAIR), BF16))
    out_specs = (pl.BlockSpec((None, tm, D_CONV), row_map),
                 pl.BlockSpec((None, tm, D_CONV), row_map),
                 pl.BlockSpec((None, tm, D_SSM), row_map),
                 pl.BlockSpec((None, tm, D_ATTN), row_map),
                 pl.BlockSpec((None, tm, HEAD_PAIR), row_map))
    return pl.pallas_call(
        functools.partial(_inproj_kernel, rope=rope),
        out_shape=out_shape, grid=(bsz, t // tm), in_specs=in_specs, out_specs=out_specs,
        compiler_params=_cparams(2),
        name="in_projection_lat" if rope else "in_projection_ctx",
    )(*args)


def _scan_kernel(uf_ref, ur_ref, h0f_ref, h0r_ref, bmat_ref, lam_ref, cmat_ref,
                 yf_ref, yr_ref, hf_ref, hr_ref, *, steps):
    @pl.when(pl.program_id(0) == 0)
    def _():
        hf_ref[...] = h0f_ref[...]
        hr_ref[...] = h0r_ref[...]

    piece = SCAN_PIECE
    n_pieces = steps // piece

    def drive(u_ref, d, p):
        u = jnp.concatenate([u_ref[:, t, :] for t in range(p * piece, (p + 1) * piece)], axis=0)
        return jnp.dot(u.astype(BF16), bmat_ref[d], preferred_element_type=F32)

    def recur(bu, d, h, descending):
        l_re = jnp.broadcast_to(lam_ref[2 * d:2 * d + 1, :], (SUBLANES, N_STATE))
        l_im = jnp.broadcast_to(lam_ref[2 * d + 1:2 * d + 2, :], (SUBLANES, N_STATE))
        h_re, h_im = h
        states = [None] * piece
        for t in (range(piece - 1, -1, -1) if descending else range(piece)):
            rows = slice(t * SUBLANES, (t + 1) * SUBLANES)
            h_re, h_im = (l_re * h_re - l_im * h_im + bu[rows, 0:N_STATE],
                          l_re * h_im + l_im * h_re + bu[rows, N_STATE:2 * N_STATE])
            states[t] = (h_re, h_im)
        return (h_re, h_im), states

    def readout(states, d, y_ref, p):
        s_re = jnp.concatenate([s[0] for s in states], axis=0).astype(BF16)
        s_im = jnp.concatenate([s[1] for s in states], axis=0).astype(BF16)
        y = jnp.dot(jnp.concatenate([s_re, s_im], axis=1), cmat_ref[d], preferred_element_type=F32)
        for t in range(piece):
            y_ref[:, p * piece + t, :] = y[t * SUBLANES:(t + 1) * SUBLANES, :]

    hf = (hf_ref[:, 0:N_STATE], hf_ref[:, N_STATE:2 * N_STATE])
    hr = (hr_ref[:, 0:N_STATE], hr_ref[:, N_STATE:2 * N_STATE])
    for p in range(n_pieces):
        q = n_pieces - 1 - p
        hf, sf = recur(drive(uf_ref, 0, p), 0, hf, False)
        hr, sr = recur(drive(ur_ref, 1, q), 1, hr, True)
        readout(sf, 0, yf_ref, p)
        readout(sr, 1, yr_ref, q)
    hf_ref[:, 0:N_STATE], hf_ref[:, N_STATE:2 * N_STATE] = hf
    hr_ref[:, 0:N_STATE], hr_ref[:, N_STATE:2 * N_STATE] = hr


def _ssm_scan(u, h0f, h0r, bmat, lamv, cmat, steps):
    bsz, t, _ = u.shape
    n = t // steps
    blk = lambda f: pl.BlockSpec((bsz, steps, D_SSM), lambda j: (0, f(j), 0))
    fwd, rev = blk(lambda j: j), blk(lambda j: n - 1 - j)
    carry = jax.ShapeDtypeStruct((SUBLANES, 2 * N_STATE), F32)
    return pl.pallas_call(
        functools.partial(_scan_kernel, steps=steps),
        out_shape=(jax.ShapeDtypeStruct(u.shape, F32),) * 2 + (carry, carry),
        grid=(n,),
        in_specs=[fwd, rev,
                  _const_spec((SUBLANES, 2 * N_STATE)),
                  _const_spec((SUBLANES, 2 * N_STATE)),
                  _const_spec((2, D_SSM, 2 * N_STATE)),
                  _const_spec((4, N_STATE)),
                  _const_spec((2, 2 * N_STATE, D_SSM))],
        out_specs=(fwd, rev,
                   pl.BlockSpec((SUBLANES, 2 * N_STATE), lambda j: (0, 0)),
                   pl.BlockSpec((SUBLANES, 2 * N_STATE), lambda j: (0, 0))),
        compiler_params=_cparams(1),
        name="s5_scan",
    )(u, u, h0f, h0r, bmat, lamv, cmat)


MAX_Q_SUB = 8


def _attn_kernel(sink_ref, q_ref, *refs, window):
    q_sub = q_ref.shape[0] // BLOCK
    if window:
        kv_refs, kvc_ref, o_ref = refs[:q_sub + 2], refs[q_sub + 2], refs[q_sub + 3]
    else:
        kvc_ref, o_ref = refs
    n = pl.program_id(1)
    last_block = pl.num_programs(1) * q_sub - 1
    rows = 2 * BLOCK
    lane = lax.broadcasted_iota(jnp.int32, (1, LANES), 1)
    lo = lane < HEAD_DIM
    qi = lax.broadcasted_iota(jnp.int32, (rows, BLOCK), 0) & (BLOCK - 1)
    kj = lax.broadcasted_iota(jnp.int32, (rows, BLOCK), 1)
    upper_head = lax.broadcasted_iota(jnp.int32, (rows, 1), 0) >= BLOCK
    zero = jnp.zeros((), BF16)
    nt = (((1,), (1,)), ((), ()))
    k_cols, v_cols = slice(0, LANES), slice(LANES, 2 * LANES)
    units = [(a, j) for a in range(q_sub) for j in range(D_ATTN // LANES)]
    qs = []
    for a, j in units:
        qp = q_ref[a * BLOCK:(a + 1) * BLOCK, j * LANES:(j + 1) * LANES]
        qs.append(jnp.concatenate([jnp.where(lo, qp, zero), jnp.where(lo, zero, qp)], axis=0))
    s_all = []
    for (a, j), q in zip(units, qs):
        scores = []
        if window:
            block = n * q_sub + a
            s = lax.dot_general(q, kv_refs[a][:, k_cols], nt, preferred_element_type=F32)
            scores.append(jnp.where((kj >= qi) & (block > 0), s, NEG_INF))
            scores.append(lax.dot_general(q, kv_refs[a + 1][:, k_cols], nt, preferred_element_type=F32))
            s = lax.dot_general(q, kv_refs[a + 2][:, k_cols], nt, preferred_element_type=F32)
            scores.append(jnp.where((kj <= qi) & (block < last_block), s, NEG_INF))
        scores.append(lax.dot_general(q, kvc_ref[:, k_cols], nt, preferred_element_type=F32))
        s_all.append(jnp.concatenate(scores, axis=1))
    sinks = [jnp.where(upper_head, sink_ref[N_Q_HEADS // 2 + j], sink_ref[j]) * LOG2E for _, j in units]
    ms = [jnp.maximum(sk, jnp.max(s, axis=-1, keepdims=True)) for sk, s in zip(sinks, s_all)]
    es = [jnp.exp2(s - m) for s, m in zip(s_all, ms)]
    denoms = [jnp.exp2(sk - m) + jnp.sum(e, axis=-1, keepdims=True) for sk, m, e in zip(sinks, ms, es)]
    accs = []
    for (a, j), e in zip(units, es):
        values = [kv_refs[a + w][:, v_cols] for w in range(3)] if window else []
        values = jnp.concatenate(values + [kvc_ref[:, v_cols]], axis=0)
        accs.append(jnp.dot(e.astype(BF16), values, preferred_element_type=F32))
    for (a, j), acc, denom in zip(units, accs, denoms):
        out = acc / denom
        o_ref[a * BLOCK:(a + 1) * BLOCK, j * LANES:(j + 1) * LANES] = jnp.where(
            lo, out[0:BLOCK], out[BLOCK:2 * BLOCK]).astype(BF16)


def _attention(sink, q, kv, kv_ctx, window):
    bsz, t, _ = q.shape
    n_ctx = kv_ctx.shape[1]
    nb = t // BLOCK
    q_sub = min(MAX_Q_SUB, nb)
    tq = q_sub * BLOCK
    in_specs = [pl.BlockSpec(memory_space=pltpu.SMEM),
                pl.BlockSpec((None, tq, D_ATTN), lambda b, n: (b, n, 0))]
    args = [sink, q]
    if window:
        for w in range(q_sub + 2):
            in_specs.append(pl.BlockSpec(
                (None, BLOCK, HEAD_PAIR),
                lambda b, n, w=w: (b, jnp.clip(n * q_sub + w - 1, 0, nb - 1), 0)))
            args.append(kv)
    in_specs += [pl.BlockSpec((None, n_ctx, HEAD_PAIR), lambda b, n: (b, 0, 0))]
    args += [kv_ctx]
    return pl.pallas_call(
        functools.partial(_attn_kernel, window=window),
        out_shape=jax.ShapeDtypeStruct((bsz, t, D_ATTN), BF16),
        grid=(bsz, t // tq), in_specs=in_specs,
        out_specs=pl.BlockSpec((None, tq, D_ATTN), lambda b, n: (b, n, 0)),
        compiler_params=_cparams(2),
        name="window_attention" if window else "context_attention",
    )(*args)


MIX_PARTS = 1


def _mixout_kernel(h_ref, mod_ref, cv_ref, cvp_ref, cvn_ref, cb_ref, u_ref, yf_ref, yr_ref, att_ref,
                   convw_ref, d_ref, wglu_ref, bglu_ref, wout_ref, g2_ref, rw_ref, rb_ref,
                   hmid_ref, t_ref, gates_ref):
    i = pl.program_id(1)
    last = pl.num_programs(1) - 1
    tm = cv_ref.shape[0]
    cv = cv_ref[...]
    row = lax.broadcasted_iota(jnp.int32, (tm, 1), 0)
    before = jnp.where(i > 0, cvp_ref[SUBLANES - 1:SUBLANES, :], 0.0)
    after = jnp.where(i < last, cvn_ref[0:1, :], 0.0)
    down = jnp.where(row == 0, before, pltpu.roll(cv, 1, 0))
    up = jnp.where(row == tm - 1, after, pltpu.roll(cv, tm - 1, 0))
    w = convw_ref[...]
    y_conv = cb_ref[...] * (w[0:1, :] * down + w[1:2, :] * cv + w[2:3, :] * up)
    mod = mod_ref[...]
    nt = (((1,), (1,)), ((), ()))
    eidx = lax.broadcasted_iota(jnp.int32, (N_EXPERTS, 1), 0).astype(F32)

    def part(rows):
        y = d_ref[...] * u_ref[rows, :] + yf_ref[rows, :] + yr_ref[rows, :]
        gl = jax.nn.gelu(y)
        z = jnp.dot(gl.astype(BF16), wglu_ref[...], preferred_element_type=F32) + bglu_ref[...]
        y_ssm = gl * jax.nn.sigmoid(z)
        mix = (jnp.dot(y_conv[rows, :].astype(BF16), wout_ref[0:D_CONV, :], preferred_element_type=F32)
               + jnp.dot(y_ssm.astype(BF16), wout_ref[D_CONV:D_CONV + D_SSM, :], preferred_element_type=F32)
               + jnp.dot(att_ref[rows, :], wout_ref[D_CONV + D_SSM:, :], preferred_element_type=F32))
        h_mid = h_ref[rows, :] + mod[:, 2 * D_MODEL:3 * D_MODEL] * mix
        t = _rms_modulate(h_mid, g2_ref[...], mod[:, 3 * D_MODEL:4 * D_MODEL], mod[:, 4 * D_MODEL:5 * D_MODEL])
        t_hi = t.astype(BF16)
        t_lo = (t - t_hi.astype(F32)).astype(BF16)
        a = lax.dot_general(rw_ref[...], t_hi, nt, preferred_element_type=F32)
        b = lax.dot_general(rw_ref[...], t_lo, nt, preferred_element_type=F32)
        logits = a[0:N_EXPERTS] + a[N_EXPERTS:2 * N_EXPERTS] + b[0:N_EXPERTS]
        scores = jax.nn.sigmoid(logits)
        biased = scores + rb_ref[...]
        sel = jnp.zeros(scores.shape, F32)
        for _ in range(TOP_K):
            best = jnp.max(biased, axis=0, keepdims=True)
            first = jnp.min(jnp.where(biased == best, eidx, float(N_EXPERTS)), axis=0, keepdims=True)
            pick = eidx == first
            sel = jnp.where(pick, scores, sel)
            biased = jnp.where(pick, -jnp.inf, biased)
        gates = sel / jnp.sum(sel, axis=0, keepdims=True) * ROUTED_SCALE
        padded = jnp.concatenate([gates, jnp.zeros((LANES - N_EXPERTS, gates.shape[1]), F32)], axis=0)
        return h_mid, t_hi, padded.T

    parts = [slice(p * tm // MIX_PARTS, (p + 1) * tm // MIX_PARTS) for p in range(MIX_PARTS)]
    results = [part(rows) for rows in parts]
    for rows, (h_mid, t_hi, gates) in zip(parts, results):
        hmid_ref[rows, :] = h_mid
        t_ref[rows, :] = t_hi
        gates_ref[rows, :] = gates


def _mixer_output(h, mod_l, mod_row, cv, cb, u, yf, yr, att,
                  conv_w, ssm_d, w_glu, b_glu, w_out, norm_g, router_w, router_b, tm):
    bsz, t, d = h.shape
    halo = tm // SUBLANES
    n_halo = t // SUBLANES
    row_map = lambda b, i: (b, i, 0)
    u_spec = pl.BlockSpec((None, tm, D_SSM), row_map)
    in_specs = [pl.BlockSpec((None, tm, d), row_map),
                pl.BlockSpec((None, 1, 6 * d), lambda b, i: (mod_row(b), 0, 0)),
                pl.BlockSpec((None, tm, D_CONV), row_map),
                pl.BlockSpec((None, SUBLANES, D_CONV), lambda b, i: (b, jnp.maximum(i * halo - 1, 0), 0)),
                pl.BlockSpec((None, SUBLANES, D_CONV), lambda b, i: (b, jnp.minimum((i + 1) * halo, n_halo - 1), 0)),
                pl.BlockSpec((None, tm, D_CONV), row_map),
                u_spec, u_spec, u_spec,
                pl.BlockSpec((None, tm, D_ATTN), row_map),
                _const_spec((3, D_CONV)),
                _const_spec((1, D_SSM)),
                _const_spec((D_SSM, D_SSM)),
                _const_spec((1, D_SSM)),
                _const_spec((d, d)),
                _const_spec((1, d)),
                _const_spec((LANES, d)),
                _const_spec((N_EXPERTS, 1))]
    out_shape = (jax.ShapeDtypeStruct((bsz, t, d), F32),
                 jax.ShapeDtypeStruct((bsz, t, d), BF16),
                 jax.ShapeDtypeStruct((bsz, t, LANES), F32))
    out_specs = (pl.BlockSpec((None, tm, d), row_map),
                 pl.BlockSpec((None, tm, d), row_map),
                 pl.BlockSpec((None, tm, LANES), row_map))
    return pl.pallas_call(
        _mixout_kernel, out_shape=out_shape, grid=(bsz, t // tm),
        in_specs=in_specs, out_specs=out_specs,
        compiler_params=_cparams(2),
        name="mixer_output",
    )(h, mod_l, cv, cv, cv, cb, u, yf, yr, att,
      conv_w, ssm_d, w_glu, b_glu, w_out, norm_g, router_w, router_b)


EXPERTS_PER_CHUNK = 4


def _moe_kernel(t_ref, gates_ref, hmid_ref, mod_ref, wg_ref, wu_ref, wd_ref,
                wsg_ref, wsu_ref, wsd_ref, o_ref):
    t = t_ref[...]
    gates = gates_ref[...]
    hs = _silu(jnp.dot(t, wsg_ref[...], preferred_element_type=F32)) * jnp.dot(
        t, wsu_ref[...], preferred_element_type=F32)
    acc = jnp.dot(hs.astype(BF16), wsd_ref[...], preferred_element_type=F32)
    width = EXPERTS_PER_CHUNK * D_EXPERT
    n_chunks = N_EXPERTS // EXPERTS_PER_CHUNK

    def up(c):
        cols = slice(c * width, (c + 1) * width)
        return (jnp.dot(t, wg_ref[:, cols], preferred_element_type=F32),
                jnp.dot(t, wu_ref[:, cols], preferred_element_type=F32))

    nxt = up(0)
    for c in range(n_chunks):
        hg, hu = nxt
        if c + 1 < n_chunks:
            nxt = up(c + 1)
        hid = _silu(hg) * hu
        gated = []
        for e in range(EXPERTS_PER_CHUNK):
            k = c * EXPERTS_PER_CHUNK + e
            gated.append((hid[:, e * D_EXPERT:(e + 1) * D_EXPERT] * gates[:, k:k + 1]).astype(BF16))
        acc = acc + jnp.dot(jnp.concatenate(gated, axis=1), wd_ref[c * width:(c + 1) * width, :],
                            preferred_element_type=F32)
    mod = mod_ref[...]
    o_ref[...] = hmid_ref[...] + mod[:, 5 * D_MODEL:6 * D_MODEL] * acc


def _moe(t, gates, hmid, mod_l, mod_row, wg, wu, wd, wsg, wsu, wsd, tm):
    bsz, n, d = hmid.shape
    row_map = lambda b, i: (b, i, 0)
    return pl.pallas_call(
        _moe_kernel,
        out_shape=jax.ShapeDtypeStruct((bsz, n, d), F32),
        grid=(bsz, n // tm),
        in_specs=[pl.BlockSpec((None, tm, d), row_map),
                  pl.BlockSpec((None, tm, LANES), row_map),
                  pl.BlockSpec((None, tm, d), row_map),
                  pl.BlockSpec((None, 1, 6 * d), lambda b, i: (mod_row(b), 0, 0)),
                  _const_spec(wg.shape), _const_spec(wu.shape), _const_spec(wd.shape),
                  _const_spec(wsg.shape), _const_spec(wsu.shape), _const_spec(wsd.shape)],
        out_specs=pl.BlockSpec((None, tm, d), row_map),
        compiler_params=_cparams(2),
        name="moe",
    )(t, gates, hmid, mod_l, wg, wu, wd, wsg, wsu, wsd)


def _rope_tables(n_tokens):
    rows = n_tokens // GRID_W
    row = np.repeat(np.arange(rows, dtype=np.float64), GRID_W)
    col = np.tile(np.arange(GRID_W, dtype=np.float64), rows)
    inv_freq = ROPE_BASE ** (-np.arange(AXIS_PAIRS, dtype=np.float64) / AXIS_PAIRS)
    ar, ac = row[:, None] * inv_freq, col[:, None] * inv_freq
    cos = np.concatenate([np.cos(ar), np.cos(ar), np.cos(ac), np.cos(ac)], axis=1).astype(np.float32)
    sin = np.concatenate([-np.sin(ar), np.sin(ar), -np.sin(ac), np.sin(ac)], axis=1).astype(np.float32)
    reps = (1, HEAD_PAIR // HEAD_DIM)
    return jnp.tile(jnp.asarray(cos), reps), jnp.tile(jnp.asarray(sin), reps)


def _ssm_params(lam_re, lam_im, log_dt, b_re, b_im, c_re, c_im):
    lead = lam_re.shape[:2]
    lr, li = lam_re.astype(F32), lam_im.astype(F32)
    dt = jnp.exp(log_dt.astype(F32))[..., None]
    mag = jnp.exp(lr * dt)
    ar, ai = mag * jnp.cos(li * dt), mag * jnp.sin(li * dt)
    den = lr * lr + li * li
    qr = ((ar - 1.0) * lr + ai * li) / den
    qi = (ai * lr - (ar - 1.0) * li) / den
    br, bi = b_re.astype(F32), b_im.astype(F32)
    bbar_re = qr[..., None] * br - qi[..., None] * bi
    bbar_im = qr[..., None] * bi + qi[..., None] * br
    chan_group = jnp.arange(D_SSM) // SSM_GROUP
    state_group = jnp.arange(N_STATE) // SSM_STATE

    def drive(m):
        rows = jnp.swapaxes(m, -1, -2).reshape(lead + (D_SSM, SSM_STATE))
        return jnp.where(chan_group[:, None] == state_group[None, :], jnp.tile(rows, (1, 1, 1, SSM_GROUPS)), 0.0)

    def read(m):
        cols = jnp.moveaxis(m.astype(F32), -1, 2).reshape(lead + (SSM_STATE, D_SSM))
        return jnp.where(state_group[:, None] == chan_group[None, :], jnp.tile(cols, (1, 1, SSM_GROUPS, 1)), 0.0)

    bmat = jnp.concatenate([drive(bbar_re), drive(bbar_im)], axis=-1)
    cmat = jnp.concatenate([read(c_re), -read(c_im)], axis=-2)
    lamv = jnp.stack([ar, ai], axis=2).reshape(lead[0], 4, N_STATE)
    return bmat.astype(BF16), lamv, cmat.astype(BF16)


def _pair_heads(w, axis):
    shape = w.shape
    split = shape[:axis] + (N_KV_HEADS, N_Q_HEADS // N_KV_HEADS, HEAD_DIM) + shape[axis + 1:]
    return jnp.swapaxes(w.reshape(split), axis, axis + 1).reshape(shape)


def kernel(x, c, ctx, c_ctx, w_mod, b_mod, norm1_g, norm2_g, w_in, conv_w, ssm_lam_re, ssm_lam_im, ssm_log_dt, ssm_b_re, ssm_b_im, ssm_c_re, ssm_c_im, ssm_d, w_glu, b_glu, q_norm_g, k_norm_g, attn_sink, w_out, router_w, router_bias, w_exp_gate, w_exp_up, w_exp_down, w_sh_gate, w_sh_up, w_sh_down):
    bsz, n_lat, d = x.shape
    n_ctx = ctx.shape[1]
    depth = w_mod.shape[0]
    assert bsz == SUBLANES and d == D_MODEL

    mod_rows = 2 * SUBLANES
    cvec = jnp.zeros((mod_rows, d), F32).at[:bsz].set(c).at[bsz].set(c_ctx)
    mod = _modulation(cvec, w_mod, b_mod).reshape(depth, mod_rows, 1, 6 * d)
    lat_row = lambda b: b
    ctx_row = lambda b: bsz
    rope_tabs = _rope_tables(n_lat)

    tm_lat, tm_moe, tm_ctx, scan_steps = 1024, 512, 256, 256
    q0, kv0 = 3 * D_CONV + D_SSM, 3 * D_CONV + D_SSM + D_ATTN
    bmat, lamv, cmat = _ssm_params(ssm_lam_re, ssm_lam_im, ssm_log_dt, ssm_b_re, ssm_b_im, ssm_c_re, ssm_c_im)
    h, hc = x, ctx
    for l in range(depth):
        ctx_out = l < depth - 1
        mod_l = mod[l]
        w_in2 = jnp.concatenate([w_in[l][:, :q0], _pair_heads(w_in[l][:, q0:kv0], 1), w_in[l][:, kv0:]],
                                axis=1).astype(BF16)
        qg = jnp.tile(q_norm_g[l].astype(F32), HEAD_PAIR // HEAD_DIM)[None, :]
        kg = jnp.concatenate([jnp.tile(k_norm_g[l].astype(F32), N_KV_HEADS),
                              jnp.ones((N_KV_HEADS * HEAD_DIM,), F32)])[None, :]
        n1 = norm1_g[l][None, :]
        cv, cb, u_lat, q, kv = _in_projection(h, mod_l, lat_row, n1, w_in2, qg, kg, rope_tabs, tm_lat)
        cvc, cbc, u_ctx, qc, kvc = _in_projection(hc, mod_l, ctx_row, n1, w_in2, qg, kg, None, tm_ctx)
        zero_state = jnp.zeros((bsz, 2 * N_STATE), F32)
        yfc, yrc, hf, hr = _ssm_scan(u_ctx, zero_state, zero_state, bmat[l], lamv[l], cmat[l], scan_steps)
        yf, yr, _, _ = _ssm_scan(u_lat, hf, hr, bmat[l], lamv[l], cmat[l], scan_steps)

        sink = attn_sink[l].astype(F32)
        att = _attention(sink, q, kv, kvc, window=True)

        w_out_l = jnp.concatenate([w_out[l][:D_CONV + D_SSM], _pair_heads(w_out[l][D_CONV + D_SSM:], 0)], axis=0)
        rw = router_w[l].astype(F32)
        rw_hi = rw.astype(BF16)
        rw_lo = (rw - rw_hi.astype(F32)).astype(BF16)
        rw2 = jnp.concatenate([rw_hi.T, rw_lo.T, jnp.zeros((LANES - 2 * N_EXPERTS, d), BF16)], axis=0)
        rb = router_bias[l].astype(F32)[:, None]
        post = dict(conv_w=conv_w[l], ssm_d=ssm_d[l][None, :], w_glu=w_glu[l].astype(BF16),
                    b_glu=b_glu[l][None, :], w_out=w_out_l.astype(BF16), norm_g=norm2_g[l][None, :],
                    router_w=rw2, router_b=rb)
        experts = (w_exp_gate[l].astype(BF16).reshape(d, -1), w_exp_up[l].astype(BF16).reshape(d, -1),
                   w_exp_down[l].astype(BF16).reshape(-1, d), w_sh_gate[l].astype(BF16),
                   w_sh_up[l].astype(BF16), w_sh_down[l].astype(BF16))
        hmid, t, gates = _mixer_output(h, mod_l, lat_row, cv, cb, u_lat, yf, yr, att, tm=tm_lat, **post)
        h_new = _moe(t, gates, hmid, mod_l, lat_row, *experts, tm=tm_moe)
        if ctx_out:
            attc = _attention(sink, qc, None, kvc, window=False)
            hmid_c, tc, gates_c = _mixer_output(hc, mod_l, ctx_row, cvc, cbc, u_ctx, yfc, yrc, attc,
                                                tm=tm_ctx, **post)
            hc = _moe(tc, gates_c, hmid_c, mod_l, ctx_row, *experts, tm=tm_ctx)
        h = h_new
    return h
```

```python
import functools
import math

import jax
import jax.numpy as jnp
import numpy as np
from jax import lax
from jax.experimental import pallas as pl
from jax.experimental.pallas import tpu as pltpu

F32 = jnp.float32
BF16 = jnp.bfloat16

D_MODEL = 1024
D_CONV = 256
D_SSM = 256
SSM_GROUP = 16
SSM_GROUPS = 16
SSM_STATE = 64
N_STATE = SSM_GROUPS * SSM_STATE
HEAD_DIM = 64
D_ATTN = 512
N_Q_HEADS = 8
N_KV_HEADS = 2
AXIS_PAIRS = 16
ROPE_BASE = 10000.0
GRID_W = 64
BLOCK = 128
N_EXPERTS = 32
TOP_K = 4
D_EXPERT = 128
ROUTED_SCALE = 2.5
NORM_EPS = 1e-6
NEG_INF = -1e30
D_PROJ = 3 * D_CONV + D_SSM + D_ATTN + 2 * N_KV_HEADS * HEAD_DIM
LANES = 128
SUBLANES = 8
SCAN_PIECE = 32
VMEM_LIMIT = 56 * 1024 * 1024


def _cparams(n_axes):
    return pltpu.CompilerParams(dimension_semantics=("arbitrary",) * n_axes,
                                vmem_limit_bytes=VMEM_LIMIT)


def _const_spec(shape):
    nd = len(shape)
    return pl.BlockSpec(shape, lambda *_: (0,) * nd, pipeline_mode=pl.Buffered(1))


def _silu(x):
    return x * jax.nn.sigmoid(x)


def _rms_modulate(x, gain, shift, scale):
    ms = jnp.mean(x * x, axis=-1, keepdims=True)
    return (x * lax.rsqrt(ms + NORM_EPS) * gain) * (1.0 + scale) + shift


def _mod_kernel(c_ref, w_ref, b_ref, o_ref):
    cv = c_ref[...]
    s = _silu(cv)
    s_hi = s.astype(BF16)
    s_lo = (s - s_hi.astype(F32)).astype(BF16)
    w = w_ref[...]
    w_hi = w.astype(BF16)
    w_lo = (w - w_hi.astype(F32)).astype(BF16)
    rows = cv.shape[0]
    a = jnp.dot(jnp.concatenate([s_hi, s_lo], axis=0), w_hi, preferred_element_type=F32)
    b = jnp.dot(s_hi, w_lo, preferred_element_type=F32)
    o_ref[...] = a[0:rows] + a[rows:2 * rows] + b + b_ref[...]


def _modulation(cvec, w_mod, b_mod):
    depth, d, n = w_mod.shape
    rows = cvec.shape[0]
    tn = 3072
    return pl.pallas_call(
        _mod_kernel,
        out_shape=jax.ShapeDtypeStruct((depth, rows, n), F32),
        grid=(depth, n // tn),
        in_specs=[pl.BlockSpec((rows, d), lambda l, j: (0, 0)),
                  pl.BlockSpec((None, d, tn), lambda l, j: (l, 0, j)),
                  pl.BlockSpec((None, 1, tn), lambda l, j: (l, 0, j))],
        out_specs=pl.BlockSpec((None, rows, tn), lambda l, j: (l, 0, j)),
        compiler_params=_cparams(2),
        name="modulation",
    )(cvec, w_mod, b_mod.reshape(depth, 1, n))


HEAD_PAIR = 4 * HEAD_DIM
LOG2E = math.log2(math.e)
Q_SCALE = HEAD_DIM ** -0.5 * LOG2E


def _inproj_kernel(*refs, rope):
    if rope:
        (x_ref, mod_ref, g_ref, w_ref, qg_ref, kg_ref, ones_ref, perm_ref, cos_ref, sin_ref,
         cv_ref, cb_ref, u_ref, q_ref, kv_ref) = refs
    else:
        (x_ref, mod_ref, g_ref, w_ref, qg_ref, kg_ref, ones_ref,
         cv_ref, cb_ref, u_ref, q_ref, kv_ref) = refs
    mod = mod_ref[...]
    mb = _rms_modulate(x_ref[...], g_ref[...], mod[:, 0:D_MODEL], mod[:, D_MODEL:2 * D_MODEL]).astype(BF16)

    def proj(c0, width):
        return jnp.dot(mb, w_ref[:, c0:c0 + width], preferred_element_type=F32)

    conv_v = proj(2 * D_CONV, D_CONV) * proj(0, D_CONV)
    conv_b = proj(D_CONV, D_CONV)
    u = proj(3 * D_CONV, D_SSM)
    q0 = 3 * D_CONV + D_SSM
    kv0 = q0 + D_ATTN
    blocks = [(q0 + jj * HEAD_PAIR, qg_ref, q_ref, jj * HEAD_PAIR, Q_SCALE, False)
              for jj in range(D_ATTN // HEAD_PAIR)] + [(kv0, kg_ref, kv_ref, 0, 1.0, True)]
    is_k = lax.broadcasted_iota(jnp.int32, (1, HEAD_PAIR), 1) < N_KV_HEADS * HEAD_DIM
    xs = [proj(blk[0], HEAD_PAIR) for blk in blocks]
    ssqs = [jnp.dot((x * x).astype(BF16), ones_ref[...], preferred_element_type=F32) for x in xs]
    norms = [lax.rsqrt(ssq * (1.0 / HEAD_DIM) + NORM_EPS) for ssq in ssqs]
    norms = [jnp.where(is_k, nm, 1.0) if blk[5] else nm for nm, blk in zip(norms, blocks)]
    xns = [x * nm * blk[1][...] for x, nm, blk in zip(xs, norms, blocks)]
    if rope:
        partners = [jnp.dot(xn.astype(BF16), perm_ref[...], preferred_element_type=F32) for xn in xns]
        cos, sin = cos_ref[...], sin_ref[...]
        xns = [xn * (jnp.where(is_k, cos, 1.0) if blk[5] else cos) + pt * (jnp.where(is_k, sin, 0.0) if blk[5] else sin)
               for xn, pt, blk in zip(xns, partners, blocks)]
    for xn, (_, _, out_ref, o0, scale, _) in zip(xns, blocks):
        if scale != 1.0:
            xn = xn * scale
        out_ref[:, o0:o0 + HEAD_PAIR] = xn.astype(BF16)
    cv_ref[...] = conv_v
    cb_ref[...] = conv_b
    u_ref[...] = u


def _in_projection(h, mod_l, mod_row, norm_g, w_in2, qg, kg, rope_tabs, tm):
    bsz, t, d = h.shape
    rope = rope_tabs is not None
    row_map = lambda b, i: (b, i, 0)
    in_specs = [pl.BlockSpec((None, tm, d), row_map),
                pl.BlockSpec((None, 1, 6 * d), lambda b, i: (mod_row(b), 0, 0)),
                _const_spec((1, d)),
                _const_spec((d, D_PROJ)),
                _const_spec((1, HEAD_PAIR)),
                _const_spec((1, HEAD_PAIR)),
                _const_spec((HEAD_PAIR, HEAD_PAIR))]
    lane = jnp.arange(HEAD_PAIR)
    head_ones = (lane[:, None] // HEAD_DIM == lane[None, :] // HEAD_DIM).astype(BF16)
    args = [h, mod_l, norm_g, w_in2, qg, kg, head_ones]
    if rope:
        swap = (lane[:, None] == (lane[None, :] ^ AXIS_PAIRS)).astype(BF16)
        in_specs += [_const_spec((HEAD_PAIR, HEAD_PAIR))]
        in_specs += [pl.BlockSpec((tm, HEAD_PAIR), lambda b, i: (i, 0))] * 2
        args += [swap] + list(rope_tabs)
    out_shape = (jax.ShapeDtypeStruct((bsz, t, D_CONV), F32),
                 jax.ShapeDtypeStruct((bsz, t, D_CONV), F32),
                 jax.ShapeDtypeStruct((bsz, t, D_SSM), F32),
                 jax.ShapeDtypeStruct((bsz, t, D_ATTN), BF16),
                 jax.ShapeDtypeStruct((bsz, t, HEAD_PAIR), BF16))
    out_specs = (pl.BlockSpec((None, tm, D_CONV), row_map),
                 pl.BlockSpec((None, tm, D_CONV), row_map),
                 pl.BlockSpec((None, tm, D_SSM), row_map),
                 pl.BlockSpec((None, tm, D_ATTN), row_map),
                 pl.BlockSpec((None, tm, HEAD_PAIR), row_map))
    return pl.pallas_call(
        functools.partial(_inproj_kernel, rope=rope),
        out_shape=out_shape, grid=(bsz, t // tm), in_specs=in_specs, out_specs=out_specs,
        compiler_params=_cparams(2),
        name="in_projection_lat" if rope else "in_projection_ctx",
    )(*args)


def _scan_kernel(uf_ref, ur_ref, h0f_ref, h0r_ref, bmat_ref, lam_ref, cmat_ref,
                 yf_ref, yr_ref, hf_ref, hr_ref, *, steps):
    @pl.when(pl.program_id(0) == 0)
    def _():
        hf_ref[...] = h0f_ref[...]
        hr_ref[...] = h0r_ref[...]

    piece = SCAN_PIECE
    n_pieces = steps // piece

    def drive(u_ref, d, p):
        u = jnp.concatenate([u_ref[:, t, :] for t in range(p * piece, (p + 1) * piece)], axis=0)
        return jnp.dot(u.astype(BF16), bmat_ref[d], preferred_element_type=F32)

    def recur(bu, d, h, descending):
        l_re = jnp.broadcast_to(lam_ref[2 * d:2 * d + 1, :], (SUBLANES, N_STATE))
        l_im = jnp.broadcast_to(lam_ref[2 * d + 1:2 * d + 2, :], (SUBLANES, N_STATE))
        h_re, h_im = h
        states = [None] * piece
        for t in (range(piece - 1, -1, -1) if descending else range(piece)):
            rows = slice(t * SUBLANES, (t + 1) * SUBLANES)
            h_re, h_im = (l_re * h_re - l_im * h_im + bu[rows, 0:N_STATE],
                          l_re * h_im + l_im * h_re + bu[rows, N_STATE:2 * N_STATE])
            states[t] = (h_re, h_im)
        return (h_re, h_im), states

    def readout(states, d, y_ref, p):
        s_re = jnp.concatenate([s[0] for s in states], axis=0).astype(BF16)
        s_im = jnp.concatenate([s[1] for s in states], axis=0).astype(BF16)
        y = jnp.dot(jnp.concatenate([s_re, s_im], axis=1), cmat_ref[d], preferred_element_type=F32)
        for t in range(piece):
            y_ref[:, p * piece + t, :] = y[t * SUBLANES:(t + 1) * SUBLANES, :]

    hf = (hf_ref[:, 0:N_STATE], hf_ref[:, N_STATE:2 * N_STATE])
    hr = (hr_ref[:, 0:N_STATE], hr_ref[:, N_STATE:2 * N_STATE])
    for p in range(n_pieces):
        q = n_pieces - 1 - p
        hf, sf = recur(drive(uf_ref, 0, p), 0, hf, False)
        hr, sr = recur(drive(ur_ref, 1, q), 1, hr, True)
        readout(sf, 0, yf_ref, p)
        readout(sr, 1, yr_ref, q)
    hf_ref[:, 0:N_STATE], hf_ref[:, N_STATE:2 * N_STATE] = hf
    hr_ref[:, 0:N_STATE], hr_ref[:, N_STATE:2 * N_STATE] = hr


def _ssm_scan(u, h0f, h0r, bmat, lamv, cmat, steps):
    bsz, t, _ = u.shape
    n = t // steps
    blk = lambda f: pl.BlockSpec((bsz, steps, D_SSM), lambda j: (0, f(j), 0))
    fwd, rev = blk(lambda j: j), blk(lambda j: n - 1 - j)
    carry = jax.ShapeDtypeStruct((SUBLANES, 2 * N_STATE), F32)
    return pl.pallas_call(
        functools.partial(_scan_kernel, steps=steps),
        out_shape=(jax.ShapeDtypeStruct(u.shape, F32),) * 2 + (carry, carry),
        grid=(n,),
        in_specs=[fwd, rev,
                  _const_spec((SUBLANES, 2 * N_STATE)),
                  _const_spec((SUBLANES, 2 * N_STATE)),
                  _const_spec((2, D_SSM, 2 * N_STATE)),
                  _const_spec((4, N_STATE)),
                  _const_spec((2, 2 * N_STATE, D_SSM))],
        out_specs=(fwd, rev,
                   pl.BlockSpec((SUBLANES, 2 * N_STATE), lambda j: (0, 0)),
                   pl.BlockSpec((SUBLANES, 2 * N_STATE), lambda j: (0, 0))),
        compiler_params=_cparams(1),
        name="s5_scan",
    )(u, u, h0f, h0r, bmat, lamv, cmat)


MAX_Q_SUB = 8


def _attn_kernel(sink_ref, q_ref, *refs, window):
    q_sub = q_ref.shape[0] // BLOCK
    if window:
        kv_refs, kvc_ref, o_ref = refs[:q_sub + 2], refs[q_sub + 2], refs[q_sub + 3]
    else:
        kvc_ref, o_ref = refs
    n = pl.program_id(1)
    last_block = pl.num_programs(1) * q_sub - 1
    rows = 2 * BLOCK
    lane = lax.broadcasted_iota(jnp.int32, (1, LANES), 1)
    lo = lane < HEAD_DIM
    qi = lax.broadcasted_iota(jnp.int32, (rows, BLOCK), 0) & (BLOCK - 1)
    kj = lax.broadcasted_iota(jnp.int32, (rows, BLOCK), 1)
    upper_head = lax.broadcasted_iota(jnp.int32, (rows, 1), 0) >= BLOCK
    zero = jnp.zeros((), BF16)
    nt = (((1,), (1,)), ((), ()))
    k_cols, v_cols = slice(0, LANES), slice(LANES, 2 * LANES)
    units = [(a, j) for a in range(q_sub) for j in range(D_ATTN // LANES)]
    qs = []
    for a, j in units:
        qp = q_ref[a * BLOCK:(a + 1) * BLOCK, j * LANES:(j + 1) * LANES]
        qs.append(jnp.concatenate([jnp.where(lo, qp, zero), jnp.where(lo, zero, qp)], axis=0))
    s_all = []
    for (a, j), q in zip(units, qs):
        scores = []
        if window:
            block = n * q_sub + a
            s = lax.dot_general(q, kv_refs[a][:, k_cols], nt, preferred_element_type=F32)
            scores.append(jnp.where((kj >= qi) & (block > 0), s, NEG_INF))
            scores.append(lax.dot_general(q, kv_refs[a + 1][:, k_cols], nt, preferred_element_type=F32))
            s = lax.dot_general(q, kv_refs[a + 2][:, k_cols], nt, preferred_element_type=F32)
            scores.append(jnp.where((kj <= qi) & (block < last_block), s, NEG_INF))
        scores.append(lax.dot_general(q, kvc_ref[:, k_cols], nt, preferred_element_type=F32))
        s_all.append(jnp.concatenate(scores, axis=1))
    sinks = [jnp.where(upper_head, sink_ref[N_Q_HEADS // 2 + j], sink_ref[j]) * LOG2E for _, j in units]
    ms = [jnp.maximum(sk, jnp.max(s, axis=-1, keepdims=True)) for sk, s in zip(sinks, s_all)]
    es = [jnp.exp2(s - m) for s, m in zip(s_all, ms)]
    denoms = [jnp.exp2(sk - m) + jnp.sum(e, axis=-1, keepdims=True) for sk, m, e in zip(sinks, ms, es)]
    accs = []
    for (a, j), e in zip(units, es):
        values = [kv_refs[a + w][:, v_cols] for w in range(3)] if window else []
        values = jnp.concatenate(values + [kvc_ref[:, v_cols]], axis=0)
        accs.append(jnp.dot(e.astype(BF16), values, preferred_element_type=F32))
    for (a, j), acc, denom in zip(units, accs, denoms):
        out = acc / denom
        o_ref[a * BLOCK:(a + 1) * BLOCK, j * LANES:(j + 1) * LANES] = jnp.where(
            lo, out[0:BLOCK], out[BLOCK:2 * BLOCK]).astype(BF16)


def _attention(sink, q, kv, kv_ctx, window):
    bsz, t, _ = q.shape
    n_ctx = kv_ctx.shape[1]
    nb = t // BLOCK
    q_sub = min(MAX_Q_SUB, nb)
    tq = q_sub * BLOCK
    in_specs = [pl.BlockSpec(memory_space=pltpu.SMEM),
                pl.BlockSpec((None, tq, D_ATTN), lambda b, n: (b, n, 0))]
    args = [sink, q]
    if window:
        for w in range(q_sub + 2):
            in_specs.append(pl.BlockSpec(
                (None, BLOCK, HEAD_PAIR),
                lambda b, n, w=w: (b, jnp.clip(n * q_sub + w - 1, 0, nb - 1), 0)))
            args.append(kv)
    in_specs += [pl.BlockSpec((None, n_ctx, HEAD_PAIR), lambda b, n: (b, 0, 0))]
    args += [kv_ctx]
    return pl.pallas_call(
        functools.partial(_attn_kernel, window=window),
        out_shape=jax.ShapeDtypeStruct((bsz, t, D_ATTN), BF16),
        grid=(bsz, t // tq), in_specs=in_specs,
        out_specs=pl.BlockSpec((None, tq, D_ATTN), lambda b, n: (b, n, 0)),
        compiler_params=_cparams(2),
        name="window_attention" if window else "context_attention",
    )(*args)


def _mixout_kernel(h_ref, mod_ref, cv_ref, cvp_ref, cvn_ref, cb_ref, u_ref, yf_ref, yr_ref, att_ref,
                   convw_ref, d_ref, wglu_ref, bglu_ref, wout_ref, g2_ref, rw_ref, rb_ref,
                   hmid_ref, t_ref, gates_ref):
    i = pl.program_id(1)
    last = pl.num_programs(1) - 1
    tm = cv_ref.shape[0]
    cv = cv_ref[...]
    row = lax.broadcasted_iota(jnp.int32, (tm, 1), 0)
    before = jnp.where(i > 0, cvp_ref[SUBLANES - 1:SUBLANES, :], 0.0)
    after = jnp.where(i < last, cvn_ref[0:1, :], 0.0)
    down = jnp.where(row == 0, before, pltpu.roll(cv, 1, 0))
    up = jnp.where(row == tm - 1, after, pltpu.roll(cv, tm - 1, 0))
    w = convw_ref[...]
    y_conv = cb_ref[...] * (w[0:1, :] * down + w[1:2, :] * cv + w[2:3, :] * up)
    mod = mod_ref[...]
    nt = (((1,), (1,)), ((), ()))
    eidx = lax.broadcasted_iota(jnp.int32, (N_EXPERTS, 1), 0).astype(F32)

    y = d_ref[...] * u_ref[...] + yf_ref[...] + yr_ref[...]
    gl = jax.nn.gelu(y)
    z = jnp.dot(gl.astype(BF16), wglu_ref[...], preferred_element_type=F32) + bglu_ref[...]
    y_ssm = gl * jax.nn.sigmoid(z)
    mix = (jnp.dot(y_conv.astype(BF16), wout_ref[0:D_CONV, :], preferred_element_type=F32)
           + jnp.dot(y_ssm.astype(BF16), wout_ref[D_CONV:D_CONV + D_SSM, :], preferred_element_type=F32)
           + jnp.dot(att_ref[...], wout_ref[D_CONV + D_SSM:, :], preferred_element_type=F32))
    h_mid = h_ref[...] + mod[:, 2 * D_MODEL:3 * D_MODEL] * mix
    t = _rms_modulate(h_mid, g2_ref[...], mod[:, 3 * D_MODEL:4 * D_MODEL], mod[:, 4 * D_MODEL:5 * D_MODEL])
    t_hi = t.astype(BF16)
    t_lo = (t - t_hi.astype(F32)).astype(BF16)
    a = lax.dot_general(rw_ref[...], t_hi, nt, preferred_element_type=F32)
    b = lax.dot_general(rw_ref[...], t_lo, nt, preferred_element_type=F32)
    logits = a[0:N_EXPERTS] + a[N_EXPERTS:2 * N_EXPERTS] + b[0:N_EXPERTS]
    scores = jax.nn.sigmoid(logits)
    biased = scores + rb_ref[...]
    sel = jnp.zeros(scores.shape, F32)
    for _ in range(TOP_K):
        best = jnp.max(biased, axis=0, keepdims=True)
        first = jnp.min(jnp.where(biased == best, eidx, float(N_EXPERTS)), axis=0, keepdims=True)
        pick = eidx == first
        sel = jnp.where(pick, scores, sel)
        biased = jnp.where(pick, -jnp.inf, biased)
    gates = sel / jnp.sum(sel, axis=0, keepdims=True) * ROUTED_SCALE
    padded = jnp.concatenate([gates, jnp.zeros((LANES - N_EXPERTS, gates.shape[1]), F32)], axis=0)
    hmid_ref[...] = h_mid
    t_ref[...] = t_hi
    gates_ref[...] = padded.T


def _mixer_output(h, mod_l, mod_row, cv, cb, u, yf, yr, att,
                  conv_w, ssm_d, w_glu, b_glu, w_out, norm_g, router_w, router_b, tm):
    bsz, t, d = h.shape
    halo = tm // SUBLANES
    n_halo = t // SUBLANES
    row_map = lambda b, i: (b, i, 0)
    u_spec = pl.BlockSpec((None, tm, D_SSM), row_map)
    in_specs = [pl.BlockSpec((None, tm, d), row_map),
                pl.BlockSpec((None, 1, 6 * d), lambda b, i: (mod_row(b), 0, 0)),
                pl.BlockSpec((None, tm, D_CONV), row_map),
                pl.BlockSpec((None, SUBLANES, D_CONV), lambda b, i: (b, jnp.maximum(i * halo - 1, 0), 0)),
                pl.BlockSpec((None, SUBLANES, D_CONV), lambda b, i: (b, jnp.minimum((i + 1) * halo, n_halo - 1), 0)),
                pl.BlockSpec((None, tm, D_CONV), row_map),
                u_spec, u_spec, u_spec,
                pl.BlockSpec((None, tm, D_ATTN), row_map),
                _const_spec((3, D_CONV)),
                _const_spec((1, D_SSM)),
                _const_spec((D_SSM, D_SSM)),
                _const_spec((1, D_SSM)),
                _const_spec((d, d)),
                _const_spec((1, d)),
                _const_spec((LANES, d)),
                _const_spec((N_EXPERTS, 1))]
    out_shape = (jax.ShapeDtypeStruct((bsz, t, d), F32),
                 jax.ShapeDtypeStruct((bsz, t, d), BF16),
                 jax.ShapeDtypeStruct((bsz, t, LANES), F32))
    out_specs = (pl.BlockSpec((None, tm, d), row_map),
                 pl.BlockSpec((None, tm, d), row_map),
                 pl.BlockSpec((None, tm, LANES), row_map))
    return pl.pallas_call(
        _mixout_kernel, out_shape=out_shape, grid=(bsz, t // tm),
        in_specs=in_specs, out_specs=out_specs,
        compiler_params=_cparams(2),
        name="mixer_output",
    )(h, mod_l, cv, cv, cv, cb, u, yf, yr, att,
      conv_w, ssm_d, w_glu, b_glu, w_out, norm_g, router_w, router_b)


EXPERTS_PER_CHUNK = 4


def _moe_kernel(t_ref, gates_ref, hmid_ref, mod_ref, wg_ref, wu_ref, wd_ref,
                wsg_ref, wsu_ref, wsd_ref, o_ref):
    t = t_ref[...]
    gates = gates_ref[...]
    hs = _silu(jnp.dot(t, wsg_ref[...], preferred_element_type=F32)) * jnp.dot(
        t, wsu_ref[...], preferred_element_type=F32)
    acc = jnp.dot(hs.astype(BF16), wsd_ref[...], preferred_element_type=F32)
    width = EXPERTS_PER_CHUNK * D_EXPERT
    n_chunks = N_EXPERTS // EXPERTS_PER_CHUNK

    def up(c):
        cols = slice(c * width, (c + 1) * width)
        return (jnp.dot(t, wg_ref[:, cols], preferred_element_type=F32),
                jnp.dot(t, wu_ref[:, cols], preferred_element_type=F32))

    nxt = up(0)
    for c in range(n_chunks):
        hg, hu = nxt
        if c + 1 < n_chunks:
            nxt = up(c + 1)
        hid = _silu(hg) * hu
        gated = []
        for e in range(EXPERTS_PER_CHUNK):
            k = c * EXPERTS_PER_CHUNK + e
            gated.append((hid[:, e * D_EXPERT:(e + 1) * D_EXPERT] * gates[:, k:k + 1]).astype(BF16))
        acc = acc + jnp.dot(jnp.concatenate(gated, axis=1), wd_ref[c * width:(c + 1) * width, :],
                            preferred_element_type=F32)
    mod = mod_ref[...]
    o_ref[...] = hmid_ref[...] + mod[:, 5 * D_MODEL:6 * D_MODEL] * acc


def _moe(t, gates, hmid, mod_l, mod_row, wg, wu, wd, wsg, wsu, wsd, tm):
    bsz, n, d = hmid.shape
    row_map = lambda b, i: (b, i, 0)
    return pl.pallas_call(
        _moe_kernel,
        out_shape=jax.ShapeDtypeStruct((bsz, n, d), F32),
        grid=(bsz, n // tm),
        in_specs=[pl.BlockSpec((None, tm, d), row_map),
                  pl.BlockSpec((None, tm, LANES), row_map),
                  pl.BlockSpec((None, tm, d), row_map),
                  pl.BlockSpec((None, 1, 6 * d), lambda b, i: (mod_row(b), 0, 0)),
                  _const_spec(wg.shape), _const_spec(wu.shape), _const_spec(wd.shape),
                  _const_spec(wsg.shape), _const_spec(wsu.shape), _const_spec(wsd.shape)],
        out_specs=pl.BlockSpec((None, tm, d), row_map),
        compiler_params=_cparams(2),
        name="moe",
    )(t, gates, hmid, mod_l, wg, wu, wd, wsg, wsu, wsd)


def _rope_tables(n_tokens):
    rows = n_tokens // GRID_W
    row = np.repeat(np.arange(rows, dtype=np.float64), GRID_W)
    col = np.tile(np.arange(GRID_W, dtype=np.float64), rows)
    inv_freq = ROPE_BASE ** (-np.arange(AXIS_PAIRS, dtype=np.float64) / AXIS_PAIRS)
    ar, ac = row[:, None] * inv_freq, col[:, None] * inv_freq
    cos = np.concatenate([np.cos(ar), np.cos(ar), np.cos(ac), np.cos(ac)], axis=1).astype(np.float32)
    sin = np.concatenate([-np.sin(ar), np.sin(ar), -np.sin(ac), np.sin(ac)], axis=1).astype(np.float32)
    reps = (1, HEAD_PAIR // HEAD_DIM)
    return jnp.tile(jnp.asarray(cos), reps), jnp.tile(jnp.asarray(sin), reps)


def _ssm_params(lam_re, lam_im, log_dt, b_re, b_im, c_re, c_im):
    lead = lam_re.shape[:2]
    lr, li = lam_re.astype(F32), lam_im.astype(F32)
    dt = jnp.exp(log_dt.astype(F32))[..., None]
    mag = jnp.exp(lr * dt)
    ar, ai = mag * jnp.cos(li * dt), mag * jnp.sin(li * dt)
    den = lr * lr + li * li
    qr = ((ar - 1.0) * lr + ai * li) / den
    qi = (ai * lr - (ar - 1.0) * li) / den
    br, bi = b_re.astype(F32), b_im.astype(F32)
    bbar_re = qr[..., None] * br - qi[..., None] * bi
    bbar_im = qr[..., None] * bi + qi[..., None] * br
    chan_group = jnp.arange(D_SSM) // SSM_GROUP
    state_group = jnp.arange(N_STATE) // SSM_STATE

    def drive(m):
        rows = jnp.swapaxes(m, -1, -2).reshape(lead + (D_SSM, SSM_STATE))
        return jnp.where(chan_group[:, None] == state_group[None, :], jnp.tile(rows, (1, 1, 1, SSM_GROUPS)), 0.0)

    def read(m):
        cols = jnp.moveaxis(m.astype(F32), -1, 2).reshape(lead + (SSM_STATE, D_SSM))
        return jnp.where(state_group[:, None] == chan_group[None, :], jnp.tile(cols, (1, 1, SSM_GROUPS, 1)), 0.0)

    bmat = jnp.concatenate([drive(bbar_re), drive(bbar_im)], axis=-1)
    cmat = jnp.concatenate([read(c_re), -read(c_im)], axis=-2)
    lamv = jnp.stack([ar, ai], axis=2).reshape(lead[0], 4, N_STATE)
    return bmat.astype(BF16), lamv, cmat.astype(BF16)


def _pair_heads(w, axis):
    shape = w.shape
    split = shape[:axis] + (N_KV_HEADS, N_Q_HEADS // N_KV_HEADS, HEAD_DIM) + shape[axis + 1:]
    return jnp.swapaxes(w.reshape(split), axis, axis + 1).reshape(shape)


def kernel(x, c, ctx, c_ctx, w_mod, b_mod, norm1_g, norm2_g, w_in, conv_w, ssm_lam_re, ssm_lam_im, ssm_log_dt, ssm_b_re, ssm_b_im, ssm_c_re, ssm_c_im, ssm_d, w_glu, b_glu, q_norm_g, k_norm_g, attn_sink, w_out, router_w, router_bias, w_exp_gate, w_exp_up, w_exp_down, w_sh_gate, w_sh_up, w_sh_down):
    bsz, n_lat, d = x.shape
    n_ctx = ctx.shape[1]
    depth = w_mod.shape[0]
    assert bsz == SUBLANES and d == D_MODEL

    mod_rows = 2 * SUBLANES
    cvec = jnp.zeros((mod_rows, d), F32).at[:bsz].set(c).at[bsz].set(c_ctx)
    mod = _modulation(cvec, w_mod, b_mod).reshape(depth, mod_rows, 1, 6 * d)
    lat_row = lambda b: b
    ctx_row = lambda b: bsz
    rope_tabs = _rope_tables(n_lat)

    tm_lat, tm_moe, tm_ctx, scan_steps = 1024, 512, 256, 256
    q0, kv0 = 3 * D_CONV + D_SSM, 3 * D_CONV + D_SSM + D_ATTN
    bmat, lamv, cmat = _ssm_params(ssm_lam_re, ssm_lam_im, ssm_log_dt, ssm_b_re, ssm_b_im, ssm_c_re, ssm_c_im)
    h, hc = x, ctx
    for l in range(depth):
        ctx_out = l < depth - 1
        mod_l = mod[l]
        w_in2 = jnp.concatenate([w_in[l][:, :q0], _pair_heads(w_in[l][:, q0:kv0], 1), w_in[l][:, kv0:]],
                                axis=1).astype(BF16)
        qg = jnp.tile(q_norm_g[l].astype(F32), HEAD_PAIR // HEAD_DIM)[None, :]
        kg = jnp.concatenate([jnp.tile(k_norm_g[l].astype(F32), N_KV_HEADS),
                              jnp.ones((N_KV_HEADS * HEAD_DIM,), F32)])[None, :]
        n1 = norm1_g[l][None, :]
        cv, cb, u_lat, q, kv = _in_projection(h, mod_l, lat_row, n1, w_in2, qg, kg, rope_tabs, tm_lat)
        cvc, cbc, u_ctx, qc, kvc = _in_projection(hc, mod_l, ctx_row, n1, w_in2, qg, kg, None, tm_ctx)
        zero_state = jnp.zeros((bsz, 2 * N_STATE), F32)
        yfc, yrc, hf, hr = _ssm_scan(u_ctx, zero_state, zero_state, bmat[l], lamv[l], cmat[l], scan_steps)
        yf, yr, _, _ = _ssm_scan(u_lat, hf, hr, bmat[l], lamv[l], cmat[l], scan_steps)

        sink = attn_sink[l].astype(F32)
        att = _attention(sink, q, kv, kvc, window=True)

        w_out_l = jnp.concatenate([w_out[l][:D_CONV + D_SSM], _pair_heads(w_out[l][D_CONV + D_SSM:], 0)], axis=0)
        rw = router_w[l].astype(F32)
        rw_hi = rw.astype(BF16)
        rw_lo = (rw - rw_hi.astype(F32)).astype(BF16)
        rw2 = jnp.concatenate([rw_hi.T, rw_lo.T, jnp.zeros((LANES - 2 * N_EXPERTS, d), BF16)], axis=0)
        rb = router_bias[l].astype(F32)[:, None]
        post = dict(conv_w=conv_w[l], ssm_d=ssm_d[l][None, :], w_glu=w_glu[l].astype(BF16),
                    b_glu=b_glu[l][None, :], w_out=w_out_l.astype(BF16), norm_g=norm2_g[l][None, :],
                    router_w=rw2, router_b=rb)
        experts = (w_exp_gate[l].astype(BF16).reshape(d, -1), w_exp_up[l].astype(BF16).reshape(d, -1),
                   w_exp_down[l].astype(BF16).reshape(-1, d), w_sh_gate[l].astype(BF16),
                   w_sh_up[l].astype(BF16), w_sh_down[l].astype(BF16))
        hmid, t, gates = _mixer_output(h, mod_l, lat_row, cv, cb, u_lat, yf, yr, att, tm=tm_lat, **post)
        h_new = _moe(t, gates, hmid, mod_l, lat_row, *experts, tm=tm_moe)
        if ctx_out:
            attc = _attention(sink, qc, None, kvc, window=False)
            hmid_c, tc, gates_c = _mixer_output(hc, mod_l, ctx_row, cvc, cbc, u_ctx, yfc, yrc, attc,
                                                tm=tm_ctx, **post)
            hc = _moe(tc, gates_c, hmid_c, mod_l, ctx_row, *experts, tm=tm_ctx)
        h = h_new
    return h
```

```python
import functools
import math

import jax
import jax.numpy as jnp
import numpy as np
from jax import lax
from jax.experimental import pallas as pl
from jax.experimental.pallas import tpu as pltpu

F32 = jnp.float32
BF16 = jnp.bfloat16

D_MODEL = 1024
D_CONV = 256
D_SSM = 256
SSM_GROUP = 16
SSM_GROUPS = 16
SSM_STATE = 64
N_STATE = SSM_GROUPS * SSM_STATE
HEAD_DIM = 64
D_ATTN = 512
N_Q_HEADS = 8
N_KV_HEADS = 2
AXIS_PAIRS = 16
ROPE_BASE = 10000.0
GRID_W = 64
BLOCK = 128
N_EXPERTS = 32
TOP_K = 4
D_EXPERT = 128
ROUTED_SCALE = 2.5
NORM_EPS = 1e-6
NEG_INF = -1e30
D_PROJ = 3 * D_CONV + D_SSM + D_ATTN + 2 * N_KV_HEADS * HEAD_DIM
LANES = 128
SUBLANES = 8
SCAN_PIECE = 32
VMEM_LIMIT = 56 * 1024 * 1024


def _cparams(n_axes):
    return pltpu.CompilerParams(dimension_semantics=("arbitrary",) * n_axes,
                                vmem_limit_bytes=VMEM_LIMIT)


def _const_spec(shape):
    nd = len(shape)
    return pl.BlockSpec(shape, lambda *_: (0,) * nd, pipeline_mode=pl.Buffered(1))


def _silu(x):
    return x * jax.nn.sigmoid(x)


def _rms_modulate(x, gain, shift, scale):
    ms = jnp.mean(x * x, axis=-1, keepdims=True)
    return (x * lax.rsqrt(ms + NORM_EPS) * gain) * (1.0 + scale) + shift


def _mod_kernel(c_ref, w_ref, b_ref, o_ref):
    cv = c_ref[...]
    s = _silu(cv)
    s_hi = s.astype(BF16)
    s_lo = (s - s_hi.astype(F32)).astype(BF16)
    w = w_ref[...]
    w_hi = w.astype(BF16)
    w_lo = (w - w_hi.astype(F32)).astype(BF16)
    rows = cv.shape[0]
    a = jnp.dot(jnp.concatenate([s_hi, s_lo], axis=0), w_hi, preferred_element_type=F32)
    b = jnp.dot(s_hi, w_lo, preferred_element_type=F32)
    o_ref[...] = a[0:rows] + a[rows:2 * rows] + b + b_ref[...]


def _modulation(cvec, w_mod, b_mod):
    depth, d, n = w_mod.shape
    rows = cvec.shape[0]
    tn = 1536
    return pl.pallas_call(
        _mod_kernel,
        out_shape=jax.ShapeDtypeStruct((depth, rows, n), F32),
        grid=(depth, n // tn),
        in_specs=[pl.BlockSpec((rows, d), lambda l, j: (0, 0)),
                  pl.BlockSpec((None, d, tn), lambda l, j: (l, 0, j)),
                  pl.BlockSpec((None, 1, tn), lambda l, j: (l, 0, j))],
        out_specs=pl.BlockSpec((None, rows, tn), lambda l, j: (l, 0, j)),
        compiler_params=_cparams(2),
        name="modulation",
    )(cvec, w_mod, b_mod.reshape(depth, 1, n))


HEAD_PAIR = 4 * HEAD_DIM
LOG2E = math.log2(math.e)
Q_SCALE = HEAD_DIM ** -0.5 * LOG2E


def _inproj_kernel(*refs, rope):
    if rope:
        (x_ref, mod_ref, g_ref, w_ref, qg_ref, kg_ref, ones_ref, perm_ref, cos_ref, sin_ref,
         cv_ref, cb_ref, u_ref, q_ref, kv_ref) = refs
    else:
        (x_ref, mod_ref, g_ref, w_ref, qg_ref, kg_ref, ones_ref,
         cv_ref, cb_ref, u_ref, q_ref, kv_ref) = refs
    mod = mod_ref[...]
    mb = _rms_modulate(x_ref[...], g_ref[...], mod[:, 0:D_MODEL], mod[:, D_MODEL:2 * D_MODEL]).astype(BF16)

    def proj(c0, width):
        return jnp.dot(mb, w_ref[:, c0:c0 + width], preferred_element_type=F32)

    conv_v = proj(2 * D_CONV, D_CONV) * proj(0, D_CONV)
    conv_b = proj(D_CONV, D_CONV)
    u = proj(3 * D_CONV, D_SSM)
    q0 = 3 * D_CONV + D_SSM
    kv0 = q0 + D_ATTN
    blocks = [(q0 + jj * HEAD_PAIR, qg_ref, q_ref, jj * HEAD_PAIR, Q_SCALE, False)
              for jj in range(D_ATTN // HEAD_PAIR)] + [(kv0, kg_ref, kv_ref, 0, 1.0, True)]
    is_k = lax.broadcasted_iota(jnp.int32, (1, HEAD_PAIR), 1) < N_KV_HEADS * HEAD_DIM
    xs = [proj(blk[0], HEAD_PAIR) for blk in blocks]
    ssqs = [jnp.dot((x * x).astype(BF16), ones_ref[...], preferred_element_type=F32) for x in xs]
    norms = [lax.rsqrt(ssq * (1.0 / HEAD_DIM) + NORM_EPS) for ssq in ssqs]
    norms = [jnp.where(is_k, nm, 1.0) if blk[5] else nm for nm, blk in zip(norms, blocks)]
    xns = [x * nm * blk[1][...] for x, nm, blk in zip(xs, norms, blocks)]
    if rope:
        partners = [jnp.dot(xn.astype(BF16), perm_ref[...], preferred_element_type=F32) for xn in xns]
        cos, sin = cos_ref[...], sin_ref[...]
        xns = [xn * (jnp.where(is_k, cos, 1.0) if blk[5] else cos) + pt * (jnp.where(is_k, sin, 0.0) if blk[5] else sin)
               for xn, pt, blk in zip(xns, partners, blocks)]
    for xn, (_, _, out_ref, o0, scale, _) in zip(xns, blocks):
        if scale != 1.0:
            xn = xn * scale
        out_ref[:, o0:o0 + HEAD_PAIR] = xn.astype(BF16)
    cv_ref[...] = conv_v
    cb_ref[...] = conv_b
    u_ref[...] = u


def _in_projection(h, mod_l, mod_row, norm_g, w_in2, qg, kg, rope_tabs, tm):
    bsz, t, d = h.shape
    rope = rope_tabs is not None
    row_map = lambda b, i: (b, i, 0)
    in_specs = [pl.BlockSpec((None, tm, d), row_map),
                pl.BlockSpec((None, 1, 6 * d), lambda b, i: (mod_row(b), 0, 0)),
                _const_spec((1, d)),
                _const_spec((d, D_PROJ)),
                _const_spec((1, HEAD_PAIR)),
                _const_spec((1, HEAD_PAIR)),
                _const_spec((HEAD_PAIR, HEAD_PAIR))]
    lane = jnp.arange(HEAD_PAIR)
    head_ones = (lane[:, None] // HEAD_DIM == lane[None, :] // HEAD_DIM).astype(BF16)
    args = [h, mod_l, norm_g, w_in2, qg, kg, head_ones]
    if rope:
        swap = (lane[:, None] == (lane[None, :] ^ AXIS_PAIRS)).astype(BF16)
        in_specs += [_const_spec((HEAD_PAIR, HEAD_PAIR))]
        in_specs += [pl.BlockSpec((tm, HEAD_PAIR), lambda b, i: (i, 0))] * 2
        args += [swap] + list(rope_tabs)
    out_shape = (jax.ShapeDtypeStruct((bsz, t, D_CONV), F32),
                 jax.ShapeDtypeStruct((bsz, t, D_CONV), F32),
                 jax.ShapeDtypeStruct((bsz, t, D_SSM), F32),
                 jax.ShapeDtypeStruct((bsz, t, D_ATTN), BF16),
                 jax.ShapeDtypeStruct((bsz, t, HEAD_PAIR), BF16))
    out_specs = (pl.BlockSpec((None, tm, D_CONV), row_map),
                 pl.BlockSpec((None, tm, D_CONV), row_map),
                 pl.BlockSpec((None, tm, D_SSM), row_map),
                 pl.BlockSpec((None, tm, D_ATTN), row_map),
                 pl.BlockSpec((None, tm, HEAD_PAIR), row_map))
    return pl.pallas_call(
        functools.partial(_inproj_kernel, rope=rope),
        out_shape=out_shape, grid=(bsz, t // tm), in_specs=in_specs, out_specs=out_specs,
        compiler_params=_cparams(2),
        name="in_projection_lat" if rope else "in_projection_ctx",
    )(*args)


def _scan_kernel(uf_ref, ur_ref, h0f_ref, h0r_ref, bmat_ref, lam_ref, cmat_ref,
                 yf_ref, yr_ref, hf_ref, hr_ref, *, steps):
    @pl.when(pl.program_id(0) == 0)
    def _():
        hf_ref[...] = h0f_ref[...]
        hr_ref[...] = h0r_ref[...]

    piece = SCAN_PIECE
    n_pieces = steps // piece

    def drive(u_ref, d, p):
        u = jnp.concatenate([u_ref[:, t, :] for t in range(p * piece, (p + 1) * piece)], axis=0)
        return jnp.dot(u.astype(BF16), bmat_ref[d], preferred_element_type=F32)

    def recur(bu, d, h, descending):
        l_re = jnp.broadcast_to(lam_ref[2 * d:2 * d + 1, :], (SUBLANES, N_STATE))
        l_im = jnp.broadcast_to(lam_ref[2 * d + 1:2 * d + 2, :], (SUBLANES, N_STATE))
        h_re, h_im = h
        states = [None] * piece
        for t in (range(piece - 1, -1, -1) if descending else range(piece)):
            rows = slice(t * SUBLANES, (t + 1) * SUBLANES)
            h_re, h_im = (l_re * h_re - l_im * h_im + bu[rows, 0:N_STATE],
                          l_re * h_im + l_im * h_re + bu[rows, N_STATE:2 * N_STATE])
            states[t] = (h_re, h_im)
        return (h_re, h_im), states

    def readout(states, d, y_ref, p):
        s_re = jnp.concatenate([s[0] for s in states], axis=0).astype(BF16)
        s_im = jnp.concatenate([s[1] for s in states], axis=0).astype(BF16)
        y = jnp.dot(jnp.concatenate([s_re, s_im], axis=1), cmat_ref[d], preferred_element_type=F32)
        for t in range(piece):
            y_ref[:, p * piece + t, :] = y[t * SUBLANES:(t + 1) * SUBLANES, :]

    hf = (hf_ref[:, 0:N_STATE], hf_ref[:, N_STATE:2 * N_STATE])
    hr = (hr_ref[:, 0:N_STATE], hr_ref[:, N_STATE:2 * N_STATE])
    for p in range(n_pieces):
        q = n_pieces - 1 - p
        hf, sf = recur(drive(uf_ref, 0, p), 0, hf, False)
        hr, sr = recur(drive(ur_ref, 1, q), 1, hr, True)
        readout(sf, 0, yf_ref, p)
        readout(sr, 1, yr_ref, q)
    hf_ref[:, 0:N_STATE], hf_ref[:, N_STATE:2 * N_STATE] = hf
    hr_ref[:, 0:N_STATE], hr_ref[:, N_STATE:2 * N_STATE] = hr


def _ssm_scan(u, h0f, h0r, bmat, lamv, cmat, steps):
    bsz, t, _ = u.shape
    n = t // steps
    blk = lambda f: pl.BlockSpec((bsz, steps, D_SSM), lambda j: (0, f(j), 0))
    fwd, rev = blk(lambda j: j), blk(lambda j: n - 1 - j)
    carry = jax.ShapeDtypeStruct((SUBLANES, 2 * N_STATE), F32)
    return pl.pallas_call(
        functools.partial(_scan_kernel, steps=steps),
        out_shape=(jax.ShapeDtypeStruct(u.shape, F32),) * 2 + (carry, carry),
        grid=(n,),
        in_specs=[fwd, rev,
                  _const_spec((SUBLANES, 2 * N_STATE)),
                  _const_spec((SUBLANES, 2 * N_STATE)),
                  _const_spec((2, D_SSM, 2 * N_STATE)),
                  _const_spec((4, N_STATE)),
                  _const_spec((2, 2 * N_STATE, D_SSM))],
        out_specs=(fwd, rev,
                   pl.BlockSpec((SUBLANES, 2 * N_STATE), lambda j: (0, 0)),
                   pl.BlockSpec((SUBLANES, 2 * N_STATE), lambda j: (0, 0))),
        compiler_params=_cparams(1),
        name="s5_scan",
    )(u, u, h0f, h0r, bmat, lamv, cmat)


MAX_Q_SUB = 8


def _attn_kernel(sink_ref, q_ref, *refs, window):
    q_sub = q_ref.shape[0] // BLOCK
    if window:
        kv_refs, kvc_ref, o_ref = refs[:q_sub + 2], refs[q_sub + 2], refs[q_sub + 3]
    else:
        kvc_ref, o_ref = refs
    n = pl.program_id(1)
    last_block = pl.num_programs(1) * q_sub - 1
    rows = 2 * BLOCK
    lane = lax.broadcasted_iota(jnp.int32, (1, LANES), 1)
    lo = lane < HEAD_DIM
    qi = lax.broadcasted_iota(jnp.int32, (rows, BLOCK), 0) & (BLOCK - 1)
    kj = lax.broadcasted_iota(jnp.int32, (rows, BLOCK), 1)
    upper_head = lax.broadcasted_iota(jnp.int32, (rows, 1), 0) >= BLOCK
    zero = jnp.zeros((), BF16)
    nt = (((1,), (1,)), ((), ()))
    k_cols, v_cols = slice(0, LANES), slice(LANES, 2 * LANES)
    units = [(a, j) for a in range(q_sub) for j in range(D_ATTN // LANES)]
    qs = []
    for a, j in units:
        qp = q_ref[a * BLOCK:(a + 1) * BLOCK, j * LANES:(j + 1) * LANES]
        qs.append(jnp.concatenate([jnp.where(lo, qp, zero), jnp.where(lo, zero, qp)], axis=0))
    s_all = []
    for (a, j), q in zip(units, qs):
        scores = []
        if window:
            block = n * q_sub + a
            s = lax.dot_general(q, kv_refs[a][:, k_cols], nt, preferred_element_type=F32)
            scores.append(jnp.where((kj >= qi) & (block > 0), s, NEG_INF))
            scores.append(lax.dot_general(q, kv_refs[a + 1][:, k_cols], nt, preferred_element_type=F32))
            s = lax.dot_general(q, kv_refs[a + 2][:, k_cols], nt, preferred_element_type=F32)
            scores.append(jnp.where((kj <= qi) & (block < last_block), s, NEG_INF))
        scores.append(lax.dot_general(q, kvc_ref[:, k_cols], nt, preferred_element_type=F32))
        s_all.append(jnp.concatenate(scores, axis=1))
    sinks = [jnp.where(upper_head, sink_ref[N_Q_HEADS // 2 + j], sink_ref[j]) * LOG2E for _, j in units]
    ms = [jnp.maximum(sk, jnp.max(s, axis=-1, keepdims=True)) for sk, s in zip(sinks, s_all)]
    es = [jnp.exp2(s - m) for s, m in zip(s_all, ms)]
    denoms = [jnp.exp2(sk - m) + jnp.sum(e, axis=-1, keepdims=True) for sk, m, e in zip(sinks, ms, es)]
    accs = []
    for (a, j), e in zip(units, es):
        values = [kv_refs[a + w][:, v_cols] for w in range(3)] if window else []
        values = jnp.concatenate(values + [kvc_ref[:, v_cols]], axis=0)
        accs.append(jnp.dot(e.astype(BF16), values, preferred_element_type=F32))
    for (a, j), acc, denom in zip(units, accs, denoms):
        out = acc / denom
        o_ref[a * BLOCK:(a + 1) * BLOCK, j * LANES:(j + 1) * LANES] = jnp.where(
            lo, out[0:BLOCK], out[BLOCK:2 * BLOCK]).astype(BF16)


def _attention(sink, q, kv, kv_ctx, window):
    bsz, t, _ = q.shape
    n_ctx = kv_ctx.shape[1]
    nb = t // BLOCK
    q_sub = min(MAX_Q_SUB, nb)
    tq = q_sub * BLOCK
    in_specs = [pl.BlockSpec(memory_space=pltpu.SMEM),
                pl.BlockSpec((None, tq, D_ATTN), lambda b, n: (b, n, 0))]
    args = [sink, q]
    if window:
        for w in range(q_sub + 2):
            in_specs.append(pl.BlockSpec(
                (None, BLOCK, HEAD_PAIR),
                lambda b, n, w=w: (b, jnp.clip(n * q_sub + w - 1, 0, nb - 1), 0)))
            args.append(kv)
    in_specs += [pl.BlockSpec((None, n_ctx, HEAD_PAIR), lambda b, n: (b, 0, 0))]
    args += [kv_ctx]
    return pl.pallas_call(
        functools.partial(_attn_kernel, window=window),
        out_shape=jax.ShapeDtypeStruct((bsz, t, D_ATTN), BF16),
        grid=(bsz, t // tq), in_specs=in_specs,
        out_specs=pl.BlockSpec((None, tq, D_ATTN), lambda b, n: (b, n, 0)),
        compiler_params=_cparams(2),
        name="window_attention" if window else "context_attention",
    )(*args)


def _mixout_kernel(h_ref, mod_ref, cv_ref, cvp_ref, cvn_ref, cb_ref, u_ref, yf_ref, yr_ref, att_ref,
                   convw_ref, d_ref, wglu_ref, bglu_ref, wout_ref, g2_ref, rw_ref, rb_ref,
                   hmid_ref, t_ref, gates_ref):
    i = pl.program_id(1)
    last = pl.num_programs(1) - 1
    tm = cv_ref.shape[0]
    cv = cv_ref[...]
    row = lax.broadcasted_iota(jnp.int32, (tm, 1), 0)
    before = jnp.where(i > 0, cvp_ref[SUBLANES - 1:SUBLANES, :], 0.0)
    after = jnp.where(i < last, cvn_ref[0:1, :], 0.0)
    down = jnp.where(row == 0, before, pltpu.roll(cv, 1, 0))
    up = jnp.where(row == tm - 1, after, pltpu.roll(cv, tm - 1, 0))
    w = convw_ref[...]
    y_conv = cb_ref[...] * (w[0:1, :] * down + w[1:2, :] * cv + w[2:3, :] * up)
    mod = mod_ref[...]
    nt = (((1,), (1,)), ((), ()))
    eidx = lax.broadcasted_iota(jnp.int32, (N_EXPERTS, 1), 0).astype(F32)

    y = d_ref[...] * u_ref[...] + yf_ref[...] + yr_ref[...]
    gl = jax.nn.gelu(y)
    z = jnp.dot(gl.astype(BF16), wglu_ref[...], preferred_element_type=F32) + bglu_ref[...]
    y_ssm = gl * jax.nn.sigmoid(z)
    mix = (jnp.dot(y_conv.astype(BF16), wout_ref[0:D_CONV, :], preferred_element_type=F32)
           + jnp.dot(y_ssm.astype(BF16), wout_ref[D_CONV:D_CONV + D_SSM, :], preferred_element_type=F32)
           + jnp.dot(att_ref[...], wout_ref[D_CONV + D_SSM:, :], preferred_element_type=F32))
    h_mid = h_ref[...] + mod[:, 2 * D_MODEL:3 * D_MODEL] * mix
    t = _rms_modulate(h_mid, g2_ref[...], mod[:, 3 * D_MODEL:4 * D_MODEL], mod[:, 4 * D_MODEL:5 * D_MODEL])
    t_hi = t.astype(BF16)
    t_lo = (t - t_hi.astype(F32)).astype(BF16)
    a = lax.dot_general(rw_ref[...], t_hi, nt, preferred_element_type=F32)
    b = lax.dot_general(rw_ref[...], t_lo, nt, preferred_element_type=F32)
    logits = a[0:N_EXPERTS] + a[N_EXPERTS:2 * N_EXPERTS] + b[0:N_EXPERTS]
    scores = jax.nn.sigmoid(logits)
    biased = scores + rb_ref[...]
    sel = jnp.zeros(scores.shape, F32)
    for _ in range(TOP_K):
        best = jnp.max(biased, axis=0, keepdims=True)
        first = jnp.min(jnp.where(biased == best, eidx, float(N_EXPERTS)), axis=0, keepdims=True)
        pick = eidx == first
        sel = jnp.where(pick, scores, sel)
        biased = jnp.where(pick, -jnp.inf, biased)
    gates = sel / jnp.sum(sel, axis=0, keepdims=True) * ROUTED_SCALE
    padded = jnp.concatenate([gates, jnp.zeros((LANES - N_EXPERTS, gates.shape[1]), F32)], axis=0)
    hmid_ref[...] = h_mid
    t_ref[...] = t_hi
    gates_ref[...] = padded.T


def _mixer_output(h, mod_l, mod_row, cv, cb, u, yf, yr, att,
                  conv_w, ssm_d, w_glu, b_glu, w_out, norm_g, router_w, router_b, tm):
    bsz, t, d = h.shape
    halo = tm // SUBLANES
    n_halo = t // SUBLANES
    row_map = lambda b, i: (b, i, 0)
    u_spec = pl.BlockSpec((None, tm, D_SSM), row_map)
    in_specs = [pl.BlockSpec((None, tm, d), row_map),
                pl.BlockSpec((None, 1, 6 * d), lambda b, i: (mod_row(b), 0, 0)),
                pl.BlockSpec((None, tm, D_CONV), row_map),
                pl.BlockSpec((None, SUBLANES, D_CONV), lambda b, i: (b, jnp.maximum(i * halo - 1, 0), 0)),
                pl.BlockSpec((None, SUBLANES, D_CONV), lambda b, i: (b, jnp.minimum((i + 1) * halo, n_halo - 1), 0)),
                pl.BlockSpec((None, tm, D_CONV), row_map),
                u_spec, u_spec, u_spec,
                pl.BlockSpec((None, tm, D_ATTN), row_map),
                _const_spec((3, D_CONV)),
                _const_spec((1, D_SSM)),
                _const_spec((D_SSM, D_SSM)),
                _const_spec((1, D_SSM)),
                _const_spec((d, d)),
                _const_spec((1, d)),
                _const_spec((LANES, d)),
                _const_spec((N_EXPERTS, 1))]
    out_shape = (jax.ShapeDtypeStruct((bsz, t, d), F32),
                 jax.ShapeDtypeStruct((bsz, t, d), BF16),
                 jax.ShapeDtypeStruct((bsz, t, LANES), F32))
    out_specs = (pl.BlockSpec((None, tm, d), row_map),
                 pl.BlockSpec((None, tm, d), row_map),
                 pl.BlockSpec((None, tm, LANES), row_map))
    return pl.pallas_call(
        _mixout_kernel, out_shape=out_shape, grid=(bsz, t // tm),
        in_specs=in_specs, out_specs=out_specs,
        compiler_params=_cparams(2),
        name="mixer_output",
    )(h, mod_l, cv, cv, cv, cb, u, yf, yr, att,
      conv_w, ssm_d, w_glu, b_glu, w_out, norm_g, router_w, router_b)


EXPERTS_PER_CHUNK = 4


def _moe_kernel(t_ref, gates_ref, hmid_ref, mod_ref, wg_ref, wu_ref, wd_ref,
                wsg_ref, wsu_ref, wsd_ref, o_ref):
    t = t_ref[...]
    gates = gates_ref[...]
    hs = _silu(jnp.dot(t, wsg_ref[...], preferred_element_type=F32)) * jnp.dot(
        t, wsu_ref[...], preferred_element_type=F32)
    acc = jnp.dot(hs.astype(BF16), wsd_ref[...], preferred_element_type=F32)
    width = EXPERTS_PER_CHUNK * D_EXPERT
    n_chunks = N_EXPERTS // EXPERTS_PER_CHUNK

    def up(c):
        cols = slice(c * width, (c + 1) * width)
        return (jnp.dot(t, wg_ref[:, cols], preferred_element_type=F32),
                jnp.dot(t, wu_ref[:, cols], preferred_element_type=F32))

    nxt = up(0)
    for c in range(n_chunks):
        hg, hu = nxt
        if c + 1 < n_chunks:
            nxt = up(c + 1)
        hid = _silu(hg) * hu
        gated = []
        for e in range(EXPERTS_PER_CHUNK):
            k = c * EXPERTS_PER_CHUNK + e
            gated.append((hid[:, e * D_EXPERT:(e + 1) * D_EXPERT] * gates[:, k:k + 1]).astype(BF16))
        acc = acc + jnp.dot(jnp.concatenate(gated, axis=1), wd_ref[c * width:(c + 1) * width, :],
                            preferred_element_type=F32)
    mod = mod_ref[...]
    o_ref[...] = hmid_ref[...] + mod[:, 5 * D_MODEL:6 * D_MODEL] * acc


def _moe(t, gates, hmid, mod_l, mod_row, wg, wu, wd, wsg, wsu, wsd, tm):
    bsz, n, d = hmid.shape
    row_map = lambda b, i: (b, i, 0)
    return pl.pallas_call(
        _moe_kernel,
        out_shape=jax.ShapeDtypeStruct((bsz, n, d), F32),
        grid=(bsz, n // tm),
        in_specs=[pl.BlockSpec((None, tm, d), row_map),
                  pl.BlockSpec((None, tm, LANES), row_map),
                  pl.BlockSpec((None, tm, d), row_map),
                  pl.BlockSpec((None, 1, 6 * d), lambda b, i: (mod_row(b), 0, 0)),
                  _const_spec(wg.shape), _const_spec(wu.shape), _const_spec(wd.shape),
                  _const_spec(wsg.shape), _const_spec(wsu.shape), _const_spec(wsd.shape)],
        out_specs=pl.BlockSpec((None, tm, d), row_map),
        compiler_params=_cparams(2),
        name="moe",
    )(t, gates, hmid, mod_l, wg, wu, wd, wsg, wsu, wsd)


def _rope_tables(n_tokens):
    rows = n_tokens // GRID_W
    row = np.repeat(np.arange(rows, dtype=np.float64), GRID_W)
    col = np.tile(np.arange(GRID_W, dtype=np.float64), rows)
    inv_freq = ROPE_BASE ** (-np.arange(AXIS_PAIRS, dtype=np.float64) / AXIS_PAIRS)
    ar, ac = row[:, None] * inv_freq, col[:, None] * inv_freq
    cos = np.concatenate([np.cos(ar), np.cos(ar), np.cos(ac), np.cos(ac)], axis=1).astype(np.float32)
    sin = np.concatenate([-np.sin(ar), np.sin(ar), -np.sin(ac), np.sin(ac)], axis=1).astype(np.float32)
    reps = (1, HEAD_PAIR // HEAD_DIM)
    return jnp.tile(jnp.asarray(cos), reps), jnp.tile(jnp.asarray(sin), reps)


def _ssm_params(lam_re, lam_im, log_dt, b_re, b_im, c_re, c_im):
    lead = lam_re.shape[:2]
    lr, li = lam_re.astype(F32), lam_im.astype(F32)
    dt = jnp.exp(log_dt.astype(F32))[..., None]
    mag = jnp.exp(lr * dt)
    ar, ai = mag * jnp.cos(li * dt), mag * jnp.sin(li * dt)
    den = lr * lr + li * li
    qr = ((ar - 1.0) * lr + ai * li) / den
    qi = (ai * lr - (ar - 1.0) * li) / den
    br, bi = b_re.astype(F32), b_im.astype(F32)
    bbar_re = qr[..., None] * br - qi[..., None] * bi
    bbar_im = qr[..., None] * bi + qi[..., None] * br
    chan_group = jnp.arange(D_SSM) // SSM_GROUP
    state_group = jnp.arange(N_STATE) // SSM_STATE

    def drive(m):
        rows = jnp.swapaxes(m, -1, -2).reshape(lead + (D_SSM, SSM_STATE))
        return jnp.where(chan_group[:, None] == state_group[None, :], jnp.tile(rows, (1, 1, 1, SSM_GROUPS)), 0.0)

    def read(m):
        cols = jnp.moveaxis(m.astype(F32), -1, 2).reshape(lead + (SSM_STATE, D_SSM))
        return jnp.where(state_group[:, None] == chan_group[None, :], jnp.tile(cols, (1, 1, SSM_GROUPS, 1)), 0.0)

    bmat = jnp.concatenate([drive(bbar_re), drive(bbar_im)], axis=-1)
    cmat = jnp.concatenate([read(c_re), -read(c_im)], axis=-2)
    lamv = jnp.stack([ar, ai], axis=2).reshape(lead[0], 4, N_STATE)
    return bmat.astype(BF16), lamv, cmat.astype(BF16)


def _pair_heads(w, axis):
    shape = w.shape
    split = shape[:axis] + (N_KV_HEADS, N_Q_HEADS // N_KV_HEADS, HEAD_DIM) + shape[axis + 1:]
    return jnp.swapaxes(w.reshape(split), axis, axis + 1).reshape(shape)


def kernel(x, c, ctx, c_ctx, w_mod, b_mod, norm1_g, norm2_g, w_in, conv_w, ssm_lam_re, ssm_lam_im, ssm_log_dt, ssm_b_re, ssm_b_im, ssm_c_re, ssm_c_im, ssm_d, w_glu, b_glu, q_norm_g, k_norm_g, attn_sink, w_out, router_w, router_bias, w_exp_gate, w_exp_up, w_exp_down, w_sh_gate, w_sh_up, w_sh_down):
    bsz, n_lat, d = x.shape
    n_ctx = ctx.shape[1]
    depth = w_mod.shape[0]
    assert bsz == SUBLANES and d == D_MODEL

    mod_rows = 2 * SUBLANES
    cvec = jnp.zeros((mod_rows, d), F32).at[:bsz].set(c).at[bsz].set(c_ctx)
    mod = _modulation(cvec, w_mod, b_mod).reshape(depth, mod_rows, 1, 6 * d)
    lat_row = lambda b: b
    ctx_row = lambda b: bsz
    rope_tabs = _rope_tables(n_lat)

    tm_lat, tm_moe, tm_ctx, scan_steps = 1024, 512, 256, 256
    q0, kv0 = 3 * D_CONV + D_SSM, 3 * D_CONV + D_SSM + D_ATTN
    bmat, lamv, cmat = _ssm_params(ssm_lam_re, ssm_lam_im, ssm_log_dt, ssm_b_re, ssm_b_im, ssm_c_re, ssm_c_im)
    h, hc = x, ctx
    for l in range(depth):
        ctx_out = l < depth - 1
        mod_l = mod[l]
        w_in2 = jnp.concatenate([w_in[l][:, :q0], _pair_heads(w_in[l][:, q0:kv0], 1), w_in[l][:, kv0:]],
                                axis=1).astype(BF16)
        qg = jnp.tile(q_norm_g[l].astype(F32), HEAD_PAIR // HEAD_DIM)[None, :]
        kg = jnp.concatenate([jnp.tile(k_norm_g[l].astype(F32), N_KV_HEADS),
                              jnp.ones((N_KV_HEADS * HEAD_DIM,), F32)])[None, :]
        n1 = norm1_g[l][None, :]
        cv, cb, u_lat, q, kv = _in_projection(h, mod_l, lat_row, n1, w_in2, qg, kg, rope_tabs, tm_lat)
        cvc, cbc, u_ctx, qc, kvc = _in_projection(hc, mod_l, ctx_row, n1, w_in2, qg, kg, None, tm_ctx)
        zero_state = jnp.zeros((bsz, 2 * N_STATE), F32)
        yfc, yrc, hf, hr = _ssm_scan(u_ctx, zero_state, zero_state, bmat[l], lamv[l], cmat[l], scan_steps)
        yf, yr, _, _ = _ssm_scan(u_lat, hf, hr, bmat[l], lamv[l], cmat[l], scan_steps)

        sink = attn_sink[l].astype(F32)
        att = _attention(sink, q, kv, kvc, window=True)

        w_out_l = jnp.concatenate([w_out[l][:D_CONV + D_SSM], _pair_heads(w_out[l][D_CONV + D_SSM:], 0)], axis=0)
        rw = router_w[l].astype(F32)
        rw_hi = rw.astype(BF16)
        rw_lo = (rw - rw_hi.astype(F32)).astype(BF16)
        rw2 = jnp.concatenate([rw_hi.T, rw_lo.T, jnp.zeros((LANES - 2 * N_EXPERTS, d), BF16)], axis=0)
        rb = router_bias[l].astype(F32)[:, None]
        post = dict(conv_w=conv_w[l], ssm_d=ssm_d[l][None, :], w_glu=w_glu[l].astype(BF16),
                    b_glu=b_glu[l][None, :], w_out=w_out_l.astype(BF16), norm_g=norm2_g[l][None, :],
                    router_w=rw2, router_b=rb)
        experts = (w_exp_gate[l].astype(BF16).reshape(d, -1), w_exp_up[l].astype(BF16).reshape(d, -1),
                   w_exp_down[l].astype(BF16).reshape(-1, d), w_sh_gate[l].astype(BF16),
                   w_sh_up[l].astype(BF16), w_sh_down[l].astype(BF16))
        hmid, t, gates = _mixer_output(h, mod_l, lat_row, cv, cb, u_lat, yf, yr, att, tm=tm_lat, **post)
        h_new = _moe(t, gates, hmid, mod_l, lat_row, *experts, tm=tm_moe)
        if ctx_out:
            attc = _attention(sink, qc, None, kvc, window=False)
            hmid_c, tc, gates_c = _mixer_output(hc, mod_l, ctx_row, cvc, cbc, u_ctx, yfc, yrc, attc,
                                                tm=tm_ctx, **post)
            hc = _moe(tc, gates_c, hmid_c, mod_l, ctx_row, *experts, tm=tm_ctx)
        h = h_new
    return h
```

```python
import functools
import math

import jax
import jax.numpy as jnp
import numpy as np
from jax import lax
from jax.experimental import pallas as pl
from jax.experimental.pallas import tpu as pltpu

F32 = jnp.float32
BF16 = jnp.bfloat16

D_MODEL = 1024
D_CONV = 256
D_SSM = 256
SSM_GROUP = 16
SSM_GROUPS = 16
SSM_STATE = 64
N_STATE = SSM_GROUPS * SSM_STATE
HEAD_DIM = 64
D_ATTN = 512
N_Q_HEADS = 8
N_KV_HEADS = 2
AXIS_PAIRS = 16
ROPE_BASE = 10000.0
GRID_W = 64
BLOCK = 128
N_EXPERTS = 32
TOP_K = 4
D_EXPERT = 128
ROUTED_SCALE = 2.5
NORM_EPS = 1e-6
NEG_INF = -1e30
D_PROJ = 3 * D_CONV + D_SSM + D_ATTN + 2 * N_KV_HEADS * HEAD_DIM
LANES = 128
SUBLANES = 8
SCAN_PIECE = 32
VMEM_LIMIT = 56 * 1024 * 1024


def _cparams(n_axes):
    return pltpu.CompilerParams(dimension_semantics=("arbitrary",) * n_axes,
                                vmem_limit_bytes=VMEM_LIMIT)


def _const_spec(shape):
    nd = len(shape)
    return pl.BlockSpec(shape, lambda *_: (0,) * nd, pipeline_mode=pl.Buffered(1))


def _silu(x):
    return x * jax.nn.sigmoid(x)


def _rms_modulate(x, gain, shift, scale):
    ms = jnp.mean(x * x, axis=-1, keepdims=True)
    return (x * lax.rsqrt(ms + NORM_EPS) * gain) * (1.0 + scale) + shift


def _mod_kernel(c_ref, w_ref, b_ref, o_ref):
    cv = c_ref[...]
    s = _silu(cv)
    s_hi = s.astype(BF16)
    s_lo = (s - s_hi.astype(F32)).astype(BF16)
    w = w_ref[...]
    w_hi = w.astype(BF16)
    w_lo = (w - w_hi.astype(F32)).astype(BF16)
    rows = cv.shape[0]
    a = jnp.dot(jnp.concatenate([s_hi, s_lo], axis=0), w_hi, preferred_element_type=F32)
    b = jnp.dot(s_hi, w_lo, preferred_element_type=F32)
    o_ref[...] = a[0:rows] + a[rows:2 * rows] + b + b_ref[...]


def _modulation(cvec, w_mod, b_mod):
    depth, d, n = w_mod.shape
    rows = cvec.shape[0]
    tn = 1536
    return pl.pallas_call(
        _mod_kernel,
        out_shape=jax.ShapeDtypeStruct((depth, rows, n), F32),
        grid=(depth, n // tn),
        in_specs=[pl.BlockSpec((rows, d), lambda l, j: (0, 0)),
                  pl.BlockSpec((None, d, tn), lambda l, j: (l, 0, j)),
                  pl.BlockSpec((None, 1, tn), lambda l, j: (l, 0, j))],
        out_specs=pl.BlockSpec((None, rows, tn), lambda l, j: (l, 0, j)),
        compiler_params=_cparams(2),
        name="modulation",
    )(cvec, w_mod, b_mod.reshape(depth, 1, n))


HEAD_PAIR = 4 * HEAD_DIM
LOG2E = math.log2(math.e)
Q_SCALE = HEAD_DIM ** -0.5 * LOG2E


def _inproj_kernel(*refs, rope):
    if rope:
        (x_ref, mod_ref, g_ref, w_ref, qg_ref, kg_ref, ones_ref, perm_ref, cos_ref, sin_ref,
         cv_ref, cb_ref, u_ref, q_ref, kv_ref) = refs
    else:
        (x_ref, mod_ref, g_ref, w_ref, qg_ref, kg_ref, ones_ref,
         cv_ref, cb_ref, u_ref, q_ref, kv_ref) = refs
    mod = mod_ref[...]
    mb = _rms_modulate(x_ref[...], g_ref[...], mod[:, 0:D_MODEL], mod[:, D_MODEL:2 * D_MODEL]).astype(BF16)

    def proj(c0, width):
        return jnp.dot(mb, w_ref[:, c0:c0 + width], preferred_element_type=F32)

    conv_v = proj(2 * D_CONV, D_CONV) * proj(0, D_CONV)
    conv_b = proj(D_CONV, D_CONV)
    u = proj(3 * D_CONV, D_SSM)
    q0 = 3 * D_CONV + D_SSM
    kv0 = q0 + D_ATTN
    blocks = [(q0 + jj * HEAD_PAIR, qg_ref, q_ref, jj * HEAD_PAIR, Q_SCALE, False)
              for jj in range(D_ATTN // HEAD_PAIR)] + [(kv0, kg_ref, kv_ref, 0, 1.0, True)]
    is_k = lax.broadcasted_iota(jnp.int32, (1, HEAD_PAIR), 1) < N_KV_HEADS * HEAD_DIM
    xs = [proj(blk[0], HEAD_PAIR) for blk in blocks]
    ssqs = [jnp.dot((x * x).astype(BF16), ones_ref[...], preferred_element_type=F32) for x in xs]
    norms = [lax.rsqrt(ssq * (1.0 / HEAD_DIM) + NORM_EPS) for ssq in ssqs]
    norms = [jnp.where(is_k, nm, 1.0) if blk[5] else nm for nm, blk in zip(norms, blocks)]
    xns = [x * nm * blk[1][...] for x, nm, blk in zip(xs, norms, blocks)]
    if rope:
        partners = [jnp.dot(xn.astype(BF16), perm_ref[...], preferred_element_type=F32) for xn in xns]
        cos, sin = cos_ref[...], sin_ref[...]
        xns = [xn * (jnp.where(is_k, cos, 1.0) if blk[5] else cos) + pt * (jnp.where(is_k, sin, 0.0) if blk[5] else sin)
               for xn, pt, blk in zip(xns, partners, blocks)]
    for xn, (_, _, out_ref, o0, scale, _) in zip(xns, blocks):
        if scale != 1.0:
            xn = xn * scale
        out_ref[:, o0:o0 + HEAD_PAIR] = xn.astype(BF16)
    cv_ref[...] = conv_v
    cb_ref[...] = conv_b
    u_ref[...] = u


def _in_projection(h, mod_l, mod_row, norm_g, w_in2, qg, kg, rope_tabs, tm):
    bsz, t, d = h.shape
    rope = rope_tabs is not None
    row_map = lambda b, i: (b, i, 0)
    in_specs = [pl.BlockSpec((None, tm, d), row_map),
                pl.BlockSpec((None, 1, 6 * d), lambda b, i: (mod_row(b), 0, 0)),
                _const_spec((1, d)),
                _const_spec((d, D_PROJ)),
                _const_spec((1, HEAD_PAIR)),
                _const_spec((1, HEAD_PAIR)),
                _const_spec((HEAD_PAIR, HEAD_PAIR))]
    lane = jnp.arange(HEAD_PAIR)
    head_ones = (lane[:, None] // HEAD_DIM == lane[None, :] // HEAD_DIM).astype(BF16)
    args = [h, mod_l, norm_g, w_in2, qg, kg, head_ones]
    if rope:
        swap = (lane[:, None] == (lane[None, :] ^ AXIS_PAIRS)).astype(BF16)
        in_specs += [_const_spec((HEAD_PAIR, HEAD_PAIR))]
        in_specs += [pl.BlockSpec((tm, HEAD_PAIR), lambda b, i: (i, 0))] * 2
        args += [swap] + list(rope_tabs)
    out_shape = (jax.ShapeDtypeStruct((bsz, t, D_CONV), F32),
                 jax.ShapeDtypeStruct((bsz, t, D_CONV), F32),
                 jax.ShapeDtypeStruct((bsz, t, D_SSM), F32),
                 jax.ShapeDtypeStruct((bsz, t, D_ATTN), BF16),
                 jax.ShapeDtypeStruct((bsz, t, HEAD_PAIR), BF16))
    out_specs = (pl.BlockSpec((None, tm, D_CONV), row_map),
                 pl.BlockSpec((None, tm, D_CONV), row_map),
                 pl.BlockSpec((None, tm, D_SSM), row_map),
                 pl.BlockSpec((None, tm, D_ATTN), row_map),
                 pl.BlockSpec((None, tm, HEAD_PAIR), row_map))
    return pl.pallas_call(
        functools.partial(_inproj_kernel, rope=rope),
        out_shape=out_shape, grid=(bsz, t // tm), in_specs=in_specs, out_specs=out_specs,
        compiler_params=_cparams(2),
        name="in_projection_lat" if rope else "in_projection_ctx",
    )(*args)


def _scan_kernel(uf_ref, ur_ref, h0f_ref, h0r_ref, bmat_ref, lam_ref, cmat_ref,
                 yf_ref, yr_ref, hf_ref, hr_ref, *, steps):
    @pl.when(pl.program_id(0) == 0)
    def _():
        hf_ref[...] = h0f_ref[...]
        hr_ref[...] = h0r_ref[...]

    piece = SCAN_PIECE
    n_pieces = steps // piece

    def drive(u_ref, d, p):
        u = jnp.concatenate([u_ref[:, t, :] for t in range(p * piece, (p + 1) * piece)], axis=0)
        return jnp.dot(u.astype(BF16), bmat_ref[d], preferred_element_type=F32)

    def recur(bu, d, h, descending):
        l_re = jnp.broadcast_to(lam_ref[2 * d:2 * d + 1, :], (SUBLANES, N_STATE))
        l_im = jnp.broadcast_to(lam_ref[2 * d + 1:2 * d + 2, :], (SUBLANES, N_STATE))
        h_re, h_im = h
        states = [None] * piece
        for t in (range(piece - 1, -1, -1) if descending else range(piece)):
            rows = slice(t * SUBLANES, (t + 1) * SUBLANES)
            h_re, h_im = (l_re * h_re - l_im * h_im + bu[rows, 0:N_STATE],
                          l_re * h_im + l_im * h_re + bu[rows, N_STATE:2 * N_STATE])
            states[t] = (h_re, h_im)
        return (h_re, h_im), states

    def readout(states, d, y_ref, p):
        s_re = jnp.concatenate([s[0] for s in states], axis=0).astype(BF16)
        s_im = jnp.concatenate([s[1] for s in states], axis=0).astype(BF16)
        y = jnp.dot(jnp.concatenate([s_re, s_im], axis=1), cmat_ref[d], preferred_element_type=F32)
        for t in range(piece):
            y_ref[:, p * piece + t, :] = y[t * SUBLANES:(t + 1) * SUBLANES, :]

    hf = (hf_ref[:, 0:N_STATE], hf_ref[:, N_STATE:2 * N_STATE])
    hr = (hr_ref[:, 0:N_STATE], hr_ref[:, N_STATE:2 * N_STATE])
    for p in range(n_pieces):
        q = n_pieces - 1 - p
        hf, sf = recur(drive(uf_ref, 0, p), 0, hf, False)
        hr, sr = recur(drive(ur_ref, 1, q), 1, hr, True)
        readout(sf, 0, yf_ref, p)
        readout(sr, 1, yr_ref, q)
    hf_ref[:, 0:N_STATE], hf_ref[:, N_STATE:2 * N_STATE] = hf
    hr_ref[:, 0:N_STATE], hr_ref[:, N_STATE:2 * N_STATE] = hr


def _ssm_scan(u, h0f, h0r, bmat, lamv, cmat, steps):
    bsz, t, _ = u.shape
    n = t // steps
    blk = lambda f: pl.BlockSpec((bsz, steps, D_SSM), lambda j: (0, f(j), 0))
    fwd, rev = blk(lambda j: j), blk(lambda j: n - 1 - j)
    carry = jax.ShapeDtypeStruct((SUBLANES, 2 * N_STATE), F32)
    return pl.pallas_call(
        functools.partial(_scan_kernel, steps=steps),
        out_shape=(jax.ShapeDtypeStruct(u.shape, F32),) * 2 + (carry, carry),
        grid=(n,),
        in_specs=[fwd, rev,
                  _const_spec((SUBLANES, 2 * N_STATE)),
                  _const_spec((SUBLANES, 2 * N_STATE)),
                  _const_spec((2, D_SSM, 2 * N_STATE)),
                  _const_spec((4, N_STATE)),
                  _const_spec((2, 2 * N_STATE, D_SSM))],
        out_specs=(fwd, rev,
                   pl.BlockSpec((SUBLANES, 2 * N_STATE), lambda j: (0, 0)),
                   pl.BlockSpec((SUBLANES, 2 * N_STATE), lambda j: (0, 0))),
        compiler_params=_cparams(1),
        name="s5_scan",
    )(u, u, h0f, h0r, bmat, lamv, cmat)


MAX_Q_SUB = 8


def _attn_kernel(sink_ref, q_ref, *refs, window):
    q_sub = q_ref.shape[0] // BLOCK
    if window:
        kv_refs, kvc_ref, o_ref = refs[:q_sub + 2], refs[q_sub + 2], refs[q_sub + 3]
    else:
        kvc_ref, o_ref = refs
    n = pl.program_id(1)
    last_block = pl.num_programs(1) * q_sub - 1
    rows = 2 * BLOCK
    lane = lax.broadcasted_iota(jnp.int32, (1, LANES), 1)
    lo = lane < HEAD_DIM
    qi = lax.broadcasted_iota(jnp.int32, (rows, BLOCK), 0) & (BLOCK - 1)
    kj = lax.broadcasted_iota(jnp.int32, (rows, BLOCK), 1)
    upper_head = lax.broadcasted_iota(jnp.int32, (rows, 1), 0) >= BLOCK
    zero = jnp.zeros((), BF16)
    nt = (((1,), (1,)), ((), ()))
    k_cols, v_cols = slice(0, LANES), slice(LANES, 2 * LANES)
    units = [(a, j) for a in range(q_sub) for j in range(D_ATTN // LANES)]
    qs = []
    for a, j in units:
        qp = q_ref[a * BLOCK:(a + 1) * BLOCK, j * LANES:(j + 1) * LANES]
        qs.append(jnp.concatenate([jnp.where(lo, qp, zero), jnp.where(lo, zero, qp)], axis=0))
    s_all = []
    for (a, j), q in zip(units, qs):
        scores = []
        if window:
            block = n * q_sub + a
            s = lax.dot_general(q, kv_refs[a][:, k_cols], nt, preferred_element_type=F32)
            scores.append(jnp.where((kj >= qi) & (block > 0), s, NEG_INF))
            scores.append(lax.dot_general(q, kv_refs[a + 1][:, k_cols], nt, preferred_element_type=F32))
            s = lax.dot_general(q, kv_refs[a + 2][:, k_cols], nt, preferred_element_type=F32)
            scores.append(jnp.where((kj <= qi) & (block < last_block), s, NEG_INF))
        scores.append(lax.dot_general(q, kvc_ref[:, k_cols], nt, preferred_element_type=F32))
        s_all.append(jnp.concatenate(scores, axis=1))
    sinks = [jnp.where(upper_head, sink_ref[N_Q_HEADS // 2 + j], sink_ref[j]) * LOG2E for _, j in units]
    ms = [jnp.maximum(sk, jnp.max(s, axis=-1, keepdims=True)) for sk, s in zip(sinks, s_all)]
    es = [jnp.exp2(s - m) for s, m in zip(s_all, ms)]
    denoms = [jnp.exp2(sk - m) + jnp.sum(e, axis=-1, keepdims=True) for sk, m, e in zip(sinks, ms, es)]
    accs = []
    for (a, j), e in zip(units, es):
        values = [kv_refs[a + w][:, v_cols] for w in range(3)] if window else []
        values = jnp.concatenate(values + [kvc_ref[:, v_cols]], axis=0)
        accs.append(jnp.dot(e.astype(BF16), values, preferred_element_type=F32))
    for (a, j), acc, denom in zip(units, accs, denoms):
        out = acc / denom
        o_ref[a * BLOCK:(a + 1) * BLOCK, j * LANES:(j + 1) * LANES] = jnp.where(
            lo, out[0:BLOCK], out[BLOCK:2 * BLOCK]).astype(BF16)


def _attention(sink, q, kv, kv_ctx, window):
    bsz, t, _ = q.shape
    n_ctx = kv_ctx.shape[1]
    nb = t // BLOCK
    q_sub = min(MAX_Q_SUB, nb)
    tq = q_sub * BLOCK
    in_specs = [pl.BlockSpec(memory_space=pltpu.SMEM),
                pl.BlockSpec((None, tq, D_ATTN), lambda b, n: (b, n, 0))]
    args = [sink, q]
    if window:
        for w in range(q_sub + 2):
            in_specs.append(pl.BlockSpec(
                (None, BLOCK, HEAD_PAIR),
                lambda b, n, w=w: (b, jnp.clip(n * q_sub + w - 1, 0, nb - 1), 0)))
            args.append(kv)
    in_specs += [pl.BlockSpec((None, n_ctx, HEAD_PAIR), lambda b, n: (b, 0, 0))]
    args += [kv_ctx]
    return pl.pallas_call(
        functools.partial(_attn_kernel, window=window),
        out_shape=jax.ShapeDtypeStruct((bsz, t, D_ATTN), BF16),
        grid=(bsz, t // tq), in_specs=in_specs,
        out_specs=pl.BlockSpec((None, tq, D_ATTN), lambda b, n: (b, n, 0)),
        compiler_params=_cparams(2),
        name="window_attention" if window else "context_attention",
    )(*args)


N_MIX_INPUTS = 18


def _mixout_values(h_ref, mod_ref, cv_ref, cvp_ref, cvn_ref, cb_ref, u_ref, yf_ref, yr_ref, att_ref,
                   convw_ref, d_ref, wglu_ref, bglu_ref, wout_ref, g2_ref, rw_ref, rb_ref):
    i = pl.program_id(1)
    last = pl.num_programs(1) - 1
    tm = cv_ref.shape[0]
    cv = cv_ref[...]
    row = lax.broadcasted_iota(jnp.int32, (tm, 1), 0)
    before = jnp.where(i > 0, cvp_ref[SUBLANES - 1:SUBLANES, :], 0.0)
    after = jnp.where(i < last, cvn_ref[0:1, :], 0.0)
    down = jnp.where(row == 0, before, pltpu.roll(cv, 1, 0))
    up = jnp.where(row == tm - 1, after, pltpu.roll(cv, tm - 1, 0))
    w = convw_ref[...]
    y_conv = cb_ref[...] * (w[0:1, :] * down + w[1:2, :] * cv + w[2:3, :] * up)
    mod = mod_ref[...]
    nt = (((1,), (1,)), ((), ()))
    eidx = lax.broadcasted_iota(jnp.int32, (N_EXPERTS, 1), 0).astype(F32)

    y = d_ref[...] * u_ref[...] + yf_ref[...] + yr_ref[...]
    gl = jax.nn.gelu(y)
    z = jnp.dot(gl.astype(BF16), wglu_ref[...], preferred_element_type=F32) + bglu_ref[...]
    y_ssm = gl * jax.nn.sigmoid(z)
    mix = (jnp.dot(y_conv.astype(BF16), wout_ref[0:D_CONV, :], preferred_element_type=F32)
           + jnp.dot(y_ssm.astype(BF16), wout_ref[D_CONV:D_CONV + D_SSM, :], preferred_element_type=F32)
           + jnp.dot(att_ref[...], wout_ref[D_CONV + D_SSM:, :], preferred_element_type=F32))
    h_mid = h_ref[...] + mod[:, 2 * D_MODEL:3 * D_MODEL] * mix
    t = _rms_modulate(h_mid, g2_ref[...], mod[:, 3 * D_MODEL:4 * D_MODEL], mod[:, 4 * D_MODEL:5 * D_MODEL])
    t_hi = t.astype(BF16)
    t_lo = (t - t_hi.astype(F32)).astype(BF16)
    a = lax.dot_general(rw_ref[...], t_hi, nt, preferred_element_type=F32)
    b = lax.dot_general(rw_ref[...], t_lo, nt, preferred_element_type=F32)
    logits = a[0:N_EXPERTS] + a[N_EXPERTS:2 * N_EXPERTS] + b[0:N_EXPERTS]
    scores = jax.nn.sigmoid(logits)
    biased = scores + rb_ref[...]
    sel = jnp.zeros(scores.shape, F32)
    for _ in range(TOP_K):
        best = jnp.max(biased, axis=0, keepdims=True)
        first = jnp.min(jnp.where(biased == best, eidx, float(N_EXPERTS)), axis=0, keepdims=True)
        pick = eidx == first
        sel = jnp.where(pick, scores, sel)
        biased = jnp.where(pick, -jnp.inf, biased)
    gates = sel / jnp.sum(sel, axis=0, keepdims=True) * ROUTED_SCALE
    padded = jnp.concatenate([gates, jnp.zeros((LANES - N_EXPERTS, gates.shape[1]), F32)], axis=0)
    return h_mid, t_hi, padded.T


def _mixout_kernel(*refs):
    hmid_ref, t_ref, gates_ref = refs[N_MIX_INPUTS:]
    h_mid, t_hi, gates = _mixout_values(*refs[:N_MIX_INPUTS])
    hmid_ref[...] = h_mid
    t_ref[...] = t_hi
    gates_ref[...] = gates


def _mix_moe_kernel(*refs):
    o_ref = refs[-1]
    h_mid, t_hi, gates = _mixout_values(*refs[:N_MIX_INPUTS])
    o_ref[...] = _moe_values(t_hi, gates, h_mid, refs[1][...], *refs[N_MIX_INPUTS:-1])


def _mixer_output(h, mod_l, mod_row, cv, cb, u, yf, yr, att,
                  conv_w, ssm_d, w_glu, b_glu, w_out, norm_g, router_w, router_b, tm, experts=None):
    bsz, t, d = h.shape
    halo = tm // SUBLANES
    n_halo = t // SUBLANES
    row_map = lambda b, i: (b, i, 0)
    u_spec = pl.BlockSpec((None, tm, D_SSM), row_map)
    in_specs = [pl.BlockSpec((None, tm, d), row_map),
                pl.BlockSpec((None, 1, 6 * d), lambda b, i: (mod_row(b), 0, 0)),
                pl.BlockSpec((None, tm, D_CONV), row_map),
                pl.BlockSpec((None, SUBLANES, D_CONV), lambda b, i: (b, jnp.maximum(i * halo - 1, 0), 0)),
                pl.BlockSpec((None, SUBLANES, D_CONV), lambda b, i: (b, jnp.minimum((i + 1) * halo, n_halo - 1), 0)),
                pl.BlockSpec((None, tm, D_CONV), row_map),
                u_spec, u_spec, u_spec,
                pl.BlockSpec((None, tm, D_ATTN), row_map),
                _const_spec((3, D_CONV)),
                _const_spec((1, D_SSM)),
                _const_spec((D_SSM, D_SSM)),
                _const_spec((1, D_SSM)),
                _const_spec((d, d)),
                _const_spec((1, d)),
                _const_spec((LANES, d)),
                _const_spec((N_EXPERTS, 1))]
    args = (h, mod_l, cv, cv, cv, cb, u, yf, yr, att,
            conv_w, ssm_d, w_glu, b_glu, w_out, norm_g, router_w, router_b)
    if experts is not None:
        return pl.pallas_call(
            _mix_moe_kernel, out_shape=jax.ShapeDtypeStruct((bsz, t, d), F32), grid=(bsz, t // tm),
            in_specs=in_specs + [_const_spec(w.shape) for w in experts],
            out_specs=pl.BlockSpec((None, tm, d), row_map),
            compiler_params=_cparams(2),
            name="mixer_moe",
        )(*args, *experts)
    out_shape = (jax.ShapeDtypeStruct((bsz, t, d), F32),
                 jax.ShapeDtypeStruct((bsz, t, d), BF16),
                 jax.ShapeDtypeStruct((bsz, t, LANES), F32))
    out_specs = (pl.BlockSpec((None, tm, d), row_map),
                 pl.BlockSpec((None, tm, d), row_map),
                 pl.BlockSpec((None, tm, LANES), row_map))
    return pl.pallas_call(
        _mixout_kernel, out_shape=out_shape, grid=(bsz, t // tm),
        in_specs=in_specs, out_specs=out_specs,
        compiler_params=_cparams(2),
        name="mixer_output",
    )(*args)


EXPERTS_PER_CHUNK = 4


def _moe_values(t, gates, h_mid, mod, wg_ref, wu_ref, wd_ref, wsg_ref, wsu_ref, wsd_ref):
    hs = _silu(jnp.dot(t, wsg_ref[...], preferred_element_type=F32)) * jnp.dot(
        t, wsu_ref[...], preferred_element_type=F32)
    acc = jnp.dot(hs.astype(BF16), wsd_ref[...], preferred_element_type=F32)
    width = EXPERTS_PER_CHUNK * D_EXPERT
    n_chunks = N_EXPERTS // EXPERTS_PER_CHUNK

    def up(c):
        cols = slice(c * width, (c + 1) * width)
        return (jnp.dot(t, wg_ref[:, cols], preferred_element_type=F32),
                jnp.dot(t, wu_ref[:, cols], preferred_element_type=F32))

    nxt = up(0)
    for c in range(n_chunks):
        hg, hu = nxt
        if c + 1 < n_chunks:
            nxt = up(c + 1)
        hid = _silu(hg) * hu
        gated = []
        for e in range(EXPERTS_PER_CHUNK):
            k = c * EXPERTS_PER_CHUNK + e
            gated.append((hid[:, e * D_EXPERT:(e + 1) * D_EXPERT] * gates[:, k:k + 1]).astype(BF16))
        acc = acc + jnp.dot(jnp.concatenate(gated, axis=1), wd_ref[c * width:(c + 1) * width, :],
                            preferred_element_type=F32)
    return h_mid + mod[:, 5 * D_MODEL:6 * D_MODEL] * acc


def _moe_kernel(t_ref, gates_ref, hmid_ref, mod_ref, wg_ref, wu_ref, wd_ref,
                wsg_ref, wsu_ref, wsd_ref, o_ref):
    o_ref[...] = _moe_values(t_ref[...], gates_ref[...], hmid_ref[...], mod_ref[...],
                             wg_ref, wu_ref, wd_ref, wsg_ref, wsu_ref, wsd_ref)


def _moe(t, gates, hmid, mod_l, mod_row, wg, wu, wd, wsg, wsu, wsd, tm):
    bsz, n, d = hmid.shape
    row_map = lambda b, i: (b, i, 0)
    return pl.pallas_call(
        _moe_kernel,
        out_shape=jax.ShapeDtypeStruct((bsz, n, d), F32),
        grid=(bsz, n // tm),
        in_specs=[pl.BlockSpec((None, tm, d), row_map),
                  pl.BlockSpec((None, tm, LANES), row_map),
                  pl.BlockSpec((None, tm, d), row_map),
                  pl.BlockSpec((None, 1, 6 * d), lambda b, i: (mod_row(b), 0, 0)),
                  _const_spec(wg.shape), _const_spec(wu.shape), _const_spec(wd.shape),
                  _const_spec(wsg.shape), _const_spec(wsu.shape), _const_spec(wsd.shape)],
        out_specs=pl.BlockSpec((None, tm, d), row_map),
        compiler_params=_cparams(2),
        name="moe",
    )(t, gates, hmid, mod_l, wg, wu, wd, wsg, wsu, wsd)


def _rope_tables(n_tokens):
    rows = n_tokens // GRID_W
    row = np.repeat(np.arange(rows, dtype=np.float64), GRID_W)
    col = np.tile(np.arange(GRID_W, dtype=np.float64), rows)
    inv_freq = ROPE_BASE ** (-np.arange(AXIS_PAIRS, dtype=np.float64) / AXIS_PAIRS)
    ar, ac = row[:, None] * inv_freq, col[:, None] * inv_freq
    cos = np.concatenate([np.cos(ar), np.cos(ar), np.cos(ac), np.cos(ac)], axis=1).astype(np.float32)
    sin = np.concatenate([-np.sin(ar), np.sin(ar), -np.sin(ac), np.sin(ac)], axis=1).astype(np.float32)
    reps = (1, HEAD_PAIR // HEAD_DIM)
    return jnp.tile(jnp.asarray(cos), reps), jnp.tile(jnp.asarray(sin), reps)


def _ssm_params(lam_re, lam_im, log_dt, b_re, b_im, c_re, c_im):
    lead = lam_re.shape[:2]
    lr, li = lam_re.astype(F32), lam_im.astype(F32)
    dt = jnp.exp(log_dt.astype(F32))[..., None]
    mag = jnp.exp(lr * dt)
    ar, ai = mag * jnp.cos(li * dt), mag * jnp.sin(li * dt)
    den = lr * lr + li * li
    qr = ((ar - 1.0) * lr + ai * li) / den
    qi = (ai * lr - (ar - 1.0) * li) / den
    br, bi = b_re.astype(F32), b_im.astype(F32)
    bbar_re = qr[..., None] * br - qi[..., None] * bi
    bbar_im = qr[..., None] * bi + qi[..., None] * br
    chan_group = jnp.arange(D_SSM) // SSM_GROUP
    state_group = jnp.arange(N_STATE) // SSM_STATE

    def drive(m):
        rows = jnp.swapaxes(m, -1, -2).reshape(lead + (D_SSM, SSM_STATE))
        return jnp.where(chan_group[:, None] == state_group[None, :], jnp.tile(rows, (1, 1, 1, SSM_GROUPS)), 0.0)

    def read(m):
        cols = jnp.moveaxis(m.astype(F32), -1, 2).reshape(lead + (SSM_STATE, D_SSM))
        return jnp.where(state_group[:, None] == chan_group[None, :], jnp.tile(cols, (1, 1, SSM_GROUPS, 1)), 0.0)

    bmat = jnp.concatenate([drive(bbar_re), drive(bbar_im)], axis=-1)
    cmat = jnp.concatenate([read(c_re), -read(c_im)], axis=-2)
    lamv = jnp.stack([ar, ai], axis=2).reshape(lead[0], 4, N_STATE)
    return bmat.astype(BF16), lamv, cmat.astype(BF16)


def _pair_heads(w, axis):
    shape = w.shape
    split = shape[:axis] + (N_KV_HEADS, N_Q_HEADS // N_KV_HEADS, HEAD_DIM) + shape[axis + 1:]
    return jnp.swapaxes(w.reshape(split), axis, axis + 1).reshape(shape)


def kernel(x, c, ctx, c_ctx, w_mod, b_mod, norm1_g, norm2_g, w_in, conv_w, ssm_lam_re, ssm_lam_im, ssm_log_dt, ssm_b_re, ssm_b_im, ssm_c_re, ssm_c_im, ssm_d, w_glu, b_glu, q_norm_g, k_norm_g, attn_sink, w_out, router_w, router_bias, w_exp_gate, w_exp_up, w_exp_down, w_sh_gate, w_sh_up, w_sh_down):
    bsz, n_lat, d = x.shape
    n_ctx = ctx.shape[1]
    depth = w_mod.shape[0]
    assert bsz == SUBLANES and d == D_MODEL

    mod_rows = 2 * SUBLANES
    cvec = jnp.zeros((mod_rows, d), F32).at[:bsz].set(c).at[bsz].set(c_ctx)
    mod = _modulation(cvec, w_mod, b_mod).reshape(depth, mod_rows, 1, 6 * d)
    lat_row = lambda b: b
    ctx_row = lambda b: bsz
    rope_tabs = _rope_tables(n_lat)

    tm_lat, tm_moe, tm_ctx, scan_steps = 1024, 512, 256, 256
    q0, kv0 = 3 * D_CONV + D_SSM, 3 * D_CONV + D_SSM + D_ATTN
    bmat, lamv, cmat = _ssm_params(ssm_lam_re, ssm_lam_im, ssm_log_dt, ssm_b_re, ssm_b_im, ssm_c_re, ssm_c_im)
    h, hc = x, ctx
    for l in range(depth):
        ctx_out = l < depth - 1
        mod_l = mod[l]
        w_in2 = jnp.concatenate([w_in[l][:, :q0], _pair_heads(w_in[l][:, q0:kv0], 1), w_in[l][:, kv0:]],
                                axis=1).astype(BF16)
        qg = jnp.tile(q_norm_g[l].astype(F32), HEAD_PAIR // HEAD_DIM)[None, :]
        kg = jnp.concatenate([jnp.tile(k_norm_g[l].astype(F32), N_KV_HEADS),
                              jnp.ones((N_KV_HEADS * HEAD_DIM,), F32)])[None, :]
        n1 = norm1_g[l][None, :]
        cv, cb, u_lat, q, kv = _in_projection(h, mod_l, lat_row, n1, w_in2, qg, kg, rope_tabs, tm_lat)
        cvc, cbc, u_ctx, qc, kvc = _in_projection(hc, mod_l, ctx_row, n1, w_in2, qg, kg, None, tm_ctx)
        zero_state = jnp.zeros((bsz, 2 * N_STATE), F32)
        yfc, yrc, hf, hr = _ssm_scan(u_ctx, zero_state, zero_state, bmat[l], lamv[l], cmat[l], scan_steps)
        yf, yr, _, _ = _ssm_scan(u_lat, hf, hr, bmat[l], lamv[l], cmat[l], scan_steps)

        sink = attn_sink[l].astype(F32)
        att = _attention(sink, q, kv, kvc, window=True)

        w_out_l = jnp.concatenate([w_out[l][:D_CONV + D_SSM], _pair_heads(w_out[l][D_CONV + D_SSM:], 0)], axis=0)
        rw = router_w[l].astype(F32)
        rw_hi = rw.astype(BF16)
        rw_lo = (rw - rw_hi.astype(F32)).astype(BF16)
        rw2 = jnp.concatenate([rw_hi.T, rw_lo.T, jnp.zeros((LANES - 2 * N_EXPERTS, d), BF16)], axis=0)
        rb = router_bias[l].astype(F32)[:, None]
        post = dict(conv_w=conv_w[l], ssm_d=ssm_d[l][None, :], w_glu=w_glu[l].astype(BF16),
                    b_glu=b_glu[l][None, :], w_out=w_out_l.astype(BF16), norm_g=norm2_g[l][None, :],
                    router_w=rw2, router_b=rb)
        experts = (w_exp_gate[l].astype(BF16).reshape(d, -1), w_exp_up[l].astype(BF16).reshape(d, -1),
                   w_exp_down[l].astype(BF16).reshape(-1, d), w_sh_gate[l].astype(BF16),
                   w_sh_up[l].astype(BF16), w_sh_down[l].astype(BF16))
        h_new = _mixer_output(h, mod_l, lat_row, cv, cb, u_lat, yf, yr, att, tm=tm_moe, experts=experts, **post)
        if ctx_out:
            attc = _attention(sink, qc, None, kvc, window=False)
            hmid_c, tc, gates_c = _mixer_output(hc, mod_l, ctx_row, cvc, cbc, u_ctx, yfc, yrc, attc,
                                                tm=tm_ctx, **post)
            hc = _moe(tc, gates_c, hmid_c, mod_l, ctx_row, *experts, tm=tm_ctx)
        h = h_new
    return h
```
